```python
import jax, jax.numpy as jnp
from jax import lax
import numpy as np

D_MODEL = 1024
BATCH = 1
SEQ = 16384
DEPTH = 1

CHUNK = 64
EPS = 1e-6

GDN_HEADS = 8
GDN_DK = 64
GDN_DV = 64
CONV_K = 4
GDN_QK = GDN_HEADS * GDN_DK
GDN_V = GDN_HEADS * GDN_DV
GDN_CONV_W = 2 * GDN_QK + GDN_V

FOX_HEADS = 8
FOX_DH = 64
FOX_W = FOX_HEADS * FOX_DH
Q_BLOCK = 128

N_BRANCH = 2
SPLIT_SIZES = (GDN_CONV_W, GDN_V, GDN_HEADS, GDN_HEADS,
               FOX_W, FOX_W, FOX_W, FOX_W, FOX_HEADS,
               N_BRANCH * D_MODEL)
D_IN = sum(SPLIT_SIZES)

kernel_name = "hybrid_gdn_fox_gated_merge"


def rmsnorm(x, w):
    xf = x.astype(jnp.float32)
    y = xf * lax.rsqrt(jnp.mean(xf * xf, axis=-1, keepdims=True) + EPS)
    return (y * w.astype(jnp.float32)).astype(x.dtype)


def l2norm(x):
    return x * lax.rsqrt(jnp.sum(x * x, axis=-1, keepdims=True) + EPS)


def causal_depthwise_conv(x, w):
    c = x.shape[-1]
    return lax.conv_general_dilated(
        x, w[:, None, :].astype(x.dtype), window_strides=(1,),
        padding=[(CONV_K - 1, 0)], dimension_numbers=('NWC', 'WIO', 'NWC'),
        feature_group_count=c)


def gated_deltanet(qkv, z, b_logit, a_logit, conv_w, a_log, dt_bias, norm_w):
    B, S, _ = qkv.shape
    H, C = GDN_HEADS, CHUNK
    n = S // C
    dtype = qkv.dtype
    qkv = jax.nn.silu(causal_depthwise_conv(qkv, conv_w)).astype(jnp.float32)
    q, k, v = jnp.split(qkv, [GDN_QK, 2 * GDN_QK], axis=-1)
    q = l2norm(q.reshape(B, S, H, GDN_DK)) * (GDN_DK ** -0.5)
    k = l2norm(k.reshape(B, S, H, GDN_DK))
    v = v.reshape(B, S, H, GDN_DV)
    beta = jax.nn.sigmoid(b_logit.astype(jnp.float32))
    g = -jnp.exp(a_log.astype(jnp.float32)) * jax.nn.softplus(
        a_logit.astype(jnp.float32) + dt_bias.astype(jnp.float32))

    def chunkify(t):
        return t.reshape(B, n, C, H, -1).transpose(0, 3, 1, 2, 4)

    q, k, v = chunkify(q), chunkify(k), chunkify(v)
    beta = beta.reshape(B, n, C, H).transpose(0, 3, 1, 2)
    g = jnp.cumsum(g.reshape(B, n, C, H).transpose(0, 3, 1, 2), axis=-1)

    causal = jnp.tril(jnp.ones((C, C), dtype=bool))
    strict = jnp.tril(jnp.ones((C, C), dtype=bool), -1)
    diff = g[..., :, None] - g[..., None, :]
    decay = jnp.exp(jnp.where(causal, diff, -jnp.inf))

    kb = k * beta[..., None]
    vb = v * beta[..., None]
    m_low = jnp.where(strict, jnp.einsum('bhnid,bhnjd->bhnij', kb, k) * decay, 0.0)
    eye = jnp.eye(C, dtype=jnp.float32)
    t_inv = lax.linalg.triangular_solve(
        m_low + eye, jnp.broadcast_to(eye, m_low.shape),
        left_side=True, lower=True, unit_diagonal=True)
    w_val = jnp.einsum('bhnij,bhnjd->bhnid', t_inv, vb)
    k_cum = jnp.einsum('bhnij,bhnjd->bhnid', t_inv, kb * jnp.exp(g)[..., None])
    a_qk = jnp.einsum('bhnid,bhnjd->bhnij', q, k) * decay
    q_g = q * jnp.exp(g)[..., None]
    k_tail = k * jnp.exp(g[..., -1:] - g)[..., None]
    g_last = jnp.exp(g[..., -1])

    def step(state, xs):
        w_c, kc_c, qg_c, a_c, kt_c, gl_c = xs
        u = w_c - jnp.einsum('bhcd,bhde->bhce', kc_c, state)
        o = jnp.einsum('bhcd,bhde->bhce', qg_c, state) + jnp.einsum('bhij,bhje->bhie', a_c, u)
        state = state * gl_c[..., None, None] + jnp.einsum('bhcd,bhce->bhde', kt_c, u)
        return state, o

    mv = lambda t: jnp.moveaxis(t, 2, 0)
    state0 = jnp.zeros((B, H, GDN_DK, GDN_DV), jnp.float32)
    _, o = lax.scan(step, state0, (mv(w_val), mv(k_cum), mv(q_g), mv(a_qk), mv(k_tail), mv(g_last)))
    o = o.transpose(1, 0, 3, 2, 4).reshape(B, S, H, GDN_DV)
    zh = z.reshape(B, S, H, GDN_DV).astype(jnp.float32)
    y = rmsnorm(o, norm_w) * jax.nn.silu(zh)
    return y.reshape(B, S, GDN_V).astype(dtype)


def forgetting_attention(q, k, v, z, f_logit, f_bias, qn_w, kn_w, on_w):
    B, S, _ = q.shape
    H, dh = FOX_HEADS, FOX_DH
    nb = S // Q_BLOCK
    dtype = q.dtype
    qh = rmsnorm(q.reshape(B, S, H, dh), qn_w).transpose(0, 2, 1, 3)
    kh = rmsnorm(k.reshape(B, S, H, dh), kn_w).transpose(0, 2, 1, 3)
    vh = v.reshape(B, S, H, dh).transpose(0, 2, 1, 3)
    log_f = jax.nn.log_sigmoid(f_logit.astype(jnp.float32) + f_bias.astype(jnp.float32))
    c = lax.cumsum(log_f, axis=1).transpose(0, 2, 1)
    q_blocks = qh.reshape(B, H, nb, Q_BLOCK, dh).transpose(2, 0, 1, 3, 4)
    c_blocks = c.reshape(B, H, nb, Q_BLOCK).transpose(2, 0, 1, 3)
    key_pos = jnp.arange(S)
    scale = dh ** -0.5

    def block(args):
        qb, cb, i = args
        q_pos = i * Q_BLOCK + jnp.arange(Q_BLOCK)
        s = jnp.einsum('bhqd,bhkd->bhqk', qb, kh).astype(jnp.float32) * scale
        s = s + cb[..., :, None] - c[..., None, :]
        s = jnp.where(key_pos[None, :] <= q_pos[:, None], s, -jnp.inf)
        p = jax.nn.softmax(s, axis=-1)
        return jnp.einsum('bhqk,bhkd->bhqd', p.astype(vh.dtype), vh)

    o = lax.map(block, (q_blocks, c_blocks, jnp.arange(nb)))
    o = o.transpose(1, 0, 3, 2, 4).reshape(B, S, H, dh)
    zh = z.reshape(B, S, H, dh).astype(jnp.float32)
    y = rmsnorm(o.astype(jnp.float32), on_w) * jax.nn.silu(zh)
    return y.reshape(B, S, FOX_W).astype(dtype)


def setup_inputs(seed: int = 0) -> dict:
    key = jax.random.key(seed)
    ks = jax.random.split(key, 20)
    f32 = jnp.float32
    nrm = lambda k, shape, scale: jax.random.normal(k, shape, f32) * scale
    gain = lambda k, shape: 1.0 + 0.02 * jax.random.normal(k, shape, f32)
    dt = jnp.exp(jax.random.uniform(ks[5], (DEPTH, GDN_HEADS), f32,
                                    math_log(0.001), math_log(0.1)))
    return {
        'x': jax.random.normal(ks[0], (BATCH, SEQ, D_MODEL), f32),
        'norm_w': gain(ks[1], (DEPTH, D_MODEL)),
        'w_in': nrm(ks[2], (DEPTH, D_MODEL, D_IN), D_MODEL ** -0.5),
        'gate_b': nrm(ks[3], (DEPTH, N_BRANCH * D_MODEL), 0.02),
        'conv_w': nrm(ks[4], (DEPTH, CONV_K, GDN_CONV_W), CONV_K ** -0.5),
        'a_log': jnp.log(jax.random.uniform(ks[6], (DEPTH, GDN_HEADS), f32, 1.0, 16.0)),
        'dt_bias': dt + jnp.log(-jnp.expm1(-dt)),
        'gdn_norm_w': gain(ks[7], (DEPTH, GDN_DV)),
        'f_bias': 2.0 + 0.1 * jax.random.normal(ks[8], (DEPTH, FOX_HEADS), f32),
        'fox_qn_w': gain(ks[9], (DEPTH, FOX_DH)),
        'fox_kn_w': gain(ks[10], (DEPTH, FOX_DH)),
        'fox_on_w': gain(ks[11], (DEPTH, FOX_DH)),
        'w_up_gdn': nrm(ks[12], (DEPTH, GDN_V, D_MODEL), GDN_V ** -0.5),
        'w_up_fox': nrm(ks[13], (DEPTH, FOX_W, D_MODEL), FOX_W ** -0.5),
        'w_out': nrm(ks[14], (DEPTH, D_MODEL, D_MODEL), D_MODEL ** -0.5),
        'final_norm_w': gain(ks[15], (D_MODEL,)),
    }


def math_log(v):
    return float(np.log(v))


def reference(x, norm_w, w_in, gate_b, conv_w, a_log, dt_bias, gdn_norm_w, f_bias,
              fox_qn_w, fox_kn_w, fox_on_w, w_up_gdn, w_up_fox, w_out, final_norm_w):
    split_points = [int(s) for s in np.cumsum(SPLIT_SIZES)[:-1]]
    for l in range(DEPTH):
        h = rmsnorm(x, norm_w[l])
        proj = jnp.einsum('bsd,de->bse', h, w_in[l])
        (gdn_qkv, gdn_z, gdn_b, gdn_a, fox_q, fox_k, fox_v, fox_z, fox_f,
         gate_logit) = jnp.split(proj, split_points, axis=-1)
        y_a = gated_deltanet(gdn_qkv, gdn_z, gdn_b, gdn_a, conv_w[l], a_log[l],
                             dt_bias[l], gdn_norm_w[l])
        y_b = forgetting_attention(fox_q, fox_k, fox_v, fox_z, fox_f, f_bias[l],
                                   fox_qn_w[l], fox_kn_w[l], fox_on_w[l])
        gates = jax.nn.sigmoid(gate_logit + gate_b[l])
        g_a, g_b = jnp.split(gates, 2, axis=-1)
        merged = (g_a * jnp.einsum('bsc,cd->bsd', y_a, w_up_gdn[l])
                  + g_b * jnp.einsum('bsc,cd->bsd', y_b, w_up_fox[l]))
        x = x + jnp.einsum('bsd,de->bse', merged, w_out[l])
    return rmsnorm(x, final_norm_w)
```

```python
import functools

import numpy as np
import jax
import jax.numpy as jnp
from jax import lax
from jax.experimental import pallas as pl
from jax.experimental.pallas import tpu as pltpu

F32 = jnp.float32
BF16 = jnp.bfloat16

EPS = 1e-6
LANES = 128
HEAD_DIM = 64
N_HEADS = 8
N_PAIRS = N_HEADS // 2
QK_W = N_HEADS * HEAD_DIM
CONV_K = 4
GDN_CHUNK = 128
VMEM_LIMIT = 56 * 1024 * 1024

COL_GDN_QKV = 0
COL_GDN_Z = 1536
COL_FOX_Q = 2048
COL_FOX_K = 2560
COL_FOX_V = 3072
COL_FOX_Z = 3584
COL_GATE = 4096
COL_SMALL = 6144
P_WIDTH = 6272
SMALL_B, SMALL_A, SMALL_F = 0, 8, 16


def _dot(a, b):
    return jnp.dot(a, b, preferred_element_type=F32)


def _dot_nt(a, b):
    return lax.dot_general(a, b, (((1,), (1,)), ((), ())), preferred_element_type=F32)


def _split3(x):
    hi = x.astype(BF16)
    r = x - hi.astype(F32)
    mid = r.astype(BF16)
    lo = (r - mid.astype(F32)).astype(BF16)
    return hi, mid, lo


def _dot_sel_rhs(x, sel):
    hi, mid, lo = _split3(x)
    return _dot(hi, sel) + _dot(mid, sel) + _dot(lo, sel)


def _dot_sel_lhs(sel, x):
    hi, mid, lo = _split3(x)
    return _dot(sel, hi) + _dot(sel, mid) + _dot(sel, lo)


def _head_sumsq(x, bd):
    y = x * x
    hi = y.astype(BF16)
    lo = (y - hi.astype(F32)).astype(BF16)
    return _dot(hi, bd) + _dot(lo, bd)


def _sigmoid(x):
    return 1.0 / (1.0 + jnp.exp(-x))


def _softplus(x):
    return jnp.maximum(x, 0.0) + jnp.log1p(jnp.exp(-jnp.abs(x)))


def _lane_iota(shape):
    return lax.broadcasted_iota(jnp.int32, shape, len(shape) - 1)


def _proj_kernel(x_ref, nw_ref, w_ref, o_ref, *, n_chunk):
    x = x_ref[...]
    ms = jnp.mean(x * x, axis=-1, keepdims=True)
    h = (x * lax.rsqrt(ms + EPS) * nw_ref[...]).astype(BF16)
    width = o_ref.shape[1]
    for c0 in range(0, width, n_chunk):
        c1 = min(c0 + n_chunk, width)
        o_ref[:, c0:c1] = _dot(h, w_ref[:, c0:c1])


def _proj(x, norm_w, w_all, *, tm=256):
    s, d = x.shape
    width = w_all.shape[1]
    return pl.pallas_call(
        functools.partial(_proj_kernel, n_chunk=1024),
        grid=(s // tm,),
        in_specs=[
            pl.BlockSpec((tm, d), lambda i: (i, 0)),
            pl.BlockSpec((1, d), lambda i: (0, 0)),
            pl.BlockSpec((d, width), lambda i: (0, 0)),
        ],
        out_specs=pl.BlockSpec((tm, width), lambda i: (i, 0)),
        out_shape=jax.ShapeDtypeStruct((s, width), F32),
        compiler_params=pltpu.CompilerParams(
            dimension_semantics=("arbitrary",), vmem_limit_bytes=VMEM_LIMIT),
        name="proj",
    )(x, norm_w, w_all)


INV_BASE_LOG2 = 5


def _shr(x, n):
    return lax.shift_right_logical(x, n)


def _inverse_masks(row, col):
    c = row.shape[0]
    lb = INV_BASE_LOG2
    diag = _shr(row, lb) == _shr(col, lb)
    levels = []
    b = lb
    while (1 << b) < c:
        in_pair = _shr(row, b + 1) == _shr(col, b + 1)
        levels.append(in_pair & ((_shr(row, b) & 1) == 1) & ((_shr(col, b) & 1) == 0))
        b += 1
    return diag, levels


def _unit_lower_inverse(m, eye, masks):
    c = m.shape[0]
    diag, levels = masks
    md = jnp.where(diag, m, 0.0)
    t = eye - md
    mdb = md.astype(BF16)
    mp = _dot(mdb, mdb)
    n_sq = INV_BASE_LOG2 - 1
    for it in range(n_sq):
        mpb = mp.astype(BF16)
        if it < n_sq - 1:
            both = _dot(jnp.concatenate([t, mp], axis=0).astype(BF16), mpb)
            t = t + both[0:c, :]
            mp = both[c:2 * c, :]
        else:
            t = t + _dot(t.astype(BF16), mpb)
    for level in levels:
        low = jnp.where(level, m, 0.0).astype(BF16)
        tb = t.astype(BF16)
        t = t - _dot(_dot(tb, low).astype(BF16), tb)
    return t


def _gdn_kernel(qkv_ref, z_ref, sm_ref, cw_ref, alog_ref, dtb_ref, nw_ref,
                bd_ref, tri_ref, ex_ref, o_ref, xext_ref, state_ref):
    c = GDN_CHUNK
    step = pl.program_id(0)

    @pl.when(step == 0)
    def _():
        xext_ref[0:8, :] = jnp.zeros((8, xext_ref.shape[1]), F32)
        state_ref[...] = jnp.zeros(state_ref.shape, F32)

    xext_ref[8:8 + c, :] = qkv_ref[...]
    y = cw_ref[3:4, :] * xext_ref[8:8 + c, :]
    for j in range(CONV_K - 1):
        y = y + cw_ref[j:j + 1, :] * xext_ref[5 + j:5 + j + c, :]
    xext_ref[0:8, :] = qkv_ref[c - 8:c, :]
    y = y * _sigmoid(y)

    bd = bd_ref[...]
    q = y[:, 0:QK_W]
    k = y[:, QK_W:2 * QK_W]
    v = y[:, 2 * QK_W:3 * QK_W]
    q = q * lax.rsqrt(_head_sumsq(q, bd) + EPS) * (HEAD_DIM ** -0.5)
    k = k * lax.rsqrt(_head_sumsq(k, bd) + EPS)

    sm = sm_ref[...]
    beta_s = _sigmoid(sm)
    g_s = -jnp.exp(alog_ref[...]) * _softplus(sm + dtb_ref[...])
    g_cum = _dot_sel_lhs(tri_ref[...], g_s)
    beta = _dot_sel_rhs(beta_s, ex_ref[:, 0:QK_W])
    g = _dot_sel_rhs(g_cum, ex_ref[:, QK_W:2 * QK_W])
    g_bc = _dot_sel_rhs(g_cum, ex_ref[:, 2 * QK_W:2 * QK_W + N_HEADS * LANES])
    g_t = g_cum.T

    eg = jnp.exp(g)
    g_last = g[c - 1:c, :]
    e_last = jnp.exp(g_last)
    kb = k * beta
    vb = v * beta
    kbg = kb * eg
    qg = q * eg
    kt = k * jnp.exp(g_last - g)

    row = lax.broadcasted_iota(jnp.int32, (c, c), 0)
    col = lax.broadcasted_iota(jnp.int32, (c, c), 1)
    causal = col <= row
    strict = col < row
    eye = (col == row).astype(F32)
    inv_masks = _inverse_masks(row, col)
    lane = _lane_iota((c, LANES))
    is_even = lane < HEAD_DIM
    bd_mask = (lane < HEAD_DIM) == (lax.broadcasted_iota(jnp.int32, (c, LANES), 0) < HEAD_DIM)

    for p in range(N_PAIRS):
        ls = slice(p * LANES, (p + 1) * LANES)
        k_p = k[:, ls]
        k_pb = k_p.astype(BF16)
        kb_p = kb[:, ls]
        q_p = q[:, ls]
        rhs_wk = jnp.concatenate([vb[:, ls], kbg[:, ls]], axis=1).astype(BF16)
        w_h, kc_h, a_h = [], [], []
        for hh in range(2):
            h = 2 * p + hh
            sel = is_even if hh == 0 else jnp.logical_not(is_even)
            diff = g_bc[:, h * LANES:(h + 1) * LANES] - g_t[SMALL_A + h:SMALL_A + h + 1, :]
            dec = jnp.where(causal, jnp.exp(jnp.where(causal, diff, 0.0)), 0.0)
            kk = _dot_nt(jnp.where(sel, kb_p, 0.0).astype(BF16), k_pb)
            m = jnp.where(strict, kk * dec, 0.0)
            t = _unit_lower_inverse(m, eye, inv_masks)
            wk = _dot(t.astype(BF16), rhs_wk)
            w_h.append(wk[:, 0:LANES])
            kc_h.append(wk[:, LANES:2 * LANES])
            a_h.append(_dot_nt(jnp.where(sel, q_p, 0.0).astype(BF16), k_pb) * dec)
        w_pair = jnp.where(is_even, w_h[0], w_h[1])
        kc_pair = jnp.where(is_even, kc_h[0], kc_h[1])

        st = state_ref[p]
        stb = st.astype(BF16)
        u = w_pair - _dot(kc_pair.astype(BF16), stb)
        ub = u.astype(BF16)
        o = _dot(qg[:, ls].astype(BF16), stb) + jnp.where(
            is_even, _dot(a_h[0].astype(BF16), ub), _dot(a_h[1].astype(BF16), ub))
        upd = _dot(kt[:, ls].T.astype(BF16), ub)
        state_ref[p] = st * e_last[:, ls] + jnp.where(bd_mask, upd, 0.0)
        o_ref[:, ls] = o.astype(o_ref.dtype)

    o_all = o_ref[...].astype(F32)
    z = z_ref[...]
    ms = _head_sumsq(o_all, bd) * (1.0 / HEAD_DIM)
    yn = o_all * lax.rsqrt(ms + EPS) * nw_ref[...]
    o_ref[...] = (yn * (z * _sigmoid(z))).astype(o_ref.dtype)


def _gdn(proj, conv_w, a_log_pad, dtb_pad, nw_tiled, bd, tri, expand):
    s = proj.shape[0]
    c = GDN_CHUNK
    conv_width = 3 * QK_W
    const = lambda shape: pl.BlockSpec(shape, lambda i: (0,) * len(shape))
    return pl.pallas_call(
        _gdn_kernel,
        grid=(s // c,),
        in_specs=[
            pl.BlockSpec((c, conv_width), lambda i: (i, COL_GDN_QKV // conv_width)),
            pl.BlockSpec((c, QK_W), lambda i: (i, COL_GDN_Z // QK_W)),
            pl.BlockSpec((c, LANES), lambda i: (i, COL_SMALL // LANES)),
            const((CONV_K, conv_width)),
            const((1, LANES)),
            const((1, LANES)),
            const((1, QK_W)),
            const((QK_W, QK_W)),
            const((c, c)),
            const(expand.shape),
        ],
        out_specs=pl.BlockSpec((c, QK_W), lambda i: (i, 0)),
        out_shape=jax.ShapeDtypeStruct((s, QK_W), F32),
        scratch_shapes=[
            pltpu.VMEM((c + 8, conv_width), F32),
            pltpu.VMEM((N_PAIRS, LANES, LANES), F32),
        ],
        compiler_params=pltpu.CompilerParams(
            dimension_semantics=("arbitrary",), vmem_limit_bytes=VMEM_LIMIT),
        name="gdn",
    )(proj, proj, proj, conv_w, a_log_pad, dtb_pad, nw_tiled, bd, tri, expand)


def _foxprep_kernel(q_ref, k_ref, v_ref, sm_ref, qw_ref, kw_ref, fb_ref, bd_ref, tri_ref,
                    qm_ref, ko_ref, vo_ref, ct_ref, carry_ref):
    @pl.when(pl.program_id(0) == 0)
    def _():
        carry_ref[...] = jnp.zeros(carry_ref.shape, F32)

    bd = bd_ref[...]

    def headnorm(x, w):
        ms = _head_sumsq(x, bd) * (1.0 / HEAD_DIM)
        return x * lax.rsqrt(ms + EPS) * w

    qn = headnorm(q_ref[...], qw_ref[...]) * (HEAD_DIM ** -0.5)
    rows = qn.shape[0]
    is_even = _lane_iota((rows, LANES)) < HEAD_DIM
    for p in range(N_PAIRS):
        blk = qn[:, p * LANES:(p + 1) * LANES]
        qm_ref[p, 0] = jnp.where(is_even, blk, 0.0).astype(qm_ref.dtype)
        qm_ref[p, 1] = jnp.where(is_even, 0.0, blk).astype(qm_ref.dtype)
    ko_ref[...] = headnorm(k_ref[...], kw_ref[...]).astype(ko_ref.dtype)
    vo_ref[...] = v_ref[...].astype(vo_ref.dtype)

    xf = sm_ref[...] + fb_ref[...]
    log_f = jnp.minimum(xf, 0.0) - jnp.log1p(jnp.exp(-jnp.abs(xf)))
    cum = carry_ref[...] + _dot_sel_lhs(tri_ref[...], log_f)
    carry_ref[...] = cum[rows - 1:rows, :]
    ct_ref[0] = cum.T[SMALL_F:SMALL_F + N_HEADS, :]


def _foxprep(proj, qw_tiled, kw_tiled, fb_pad, bd, tri, *, rows):
    s = proj.shape[0]
    const = lambda shape: pl.BlockSpec(shape, lambda i: (0,) * len(shape))
    return pl.pallas_call(
        _foxprep_kernel,
        grid=(s // rows,),
        in_specs=[
            pl.BlockSpec((rows, QK_W), lambda i: (i, COL_FOX_Q // QK_W)),
            pl.BlockSpec((rows, QK_W), lambda i: (i, COL_FOX_K // QK_W)),
            pl.BlockSpec((rows, QK_W), lambda i: (i, COL_FOX_V // QK_W)),
            pl.BlockSpec((rows, LANES), lambda i: (i, COL_SMALL // LANES)),
            const((1, QK_W)),
            const((1, QK_W)),
            const((1, LANES)),
            const((QK_W, QK_W)),
            const((rows, rows)),
        ],
        out_specs=[
            pl.BlockSpec((N_PAIRS, 2, rows, LANES), lambda i: (0, 0, i, 0)),
            pl.BlockSpec((rows, QK_W), lambda i: (i, 0)),
            pl.BlockSpec((rows, QK_W), lambda i: (i, 0)),
            pl.BlockSpec((1, N_HEADS, rows), lambda i: (i, 0, 0)),
        ],
        out_shape=[
            jax.ShapeDtypeStruct((N_PAIRS, 2, s, LANES), BF16),
            jax.ShapeDtypeStruct((s, QK_W), BF16),
            jax.ShapeDtypeStruct((s, QK_W), BF16),
            jax.ShapeDtypeStruct((s // rows, N_HEADS, rows), F32),
        ],
        scratch_shapes=[pltpu.VMEM((1, LANES), F32)],
        compiler_params=pltpu.CompilerParams(
            dimension_semantics=("arbitrary",), vmem_limit_bytes=VMEM_LIMIT),
        name="foxprep",
    )(proj, proj, proj, proj, qw_tiled, kw_tiled, fb_pad, bd, tri)


def _fox_kernel(q_ref, k_ref, v_ref, c_ref, z_ref, nw_ref, o_ref, m_ref, acc_ref, *, tq, tk):
    p = pl.program_id(0)
    i = pl.program_id(1)
    n_sub = tq // tk

    m_ref[...] = jnp.full(m_ref.shape, -jnp.inf, F32)
    acc_ref[...] = jnp.zeros(acc_ref.shape, F32)
    ones = jnp.ones((tk, LANES), BF16)

    def tile(j, masked):
        start = pl.multiple_of(j * tk, tk)
        kj = k_ref[pl.ds(start, tk), :]
        vj = jnp.concatenate([v_ref[pl.ds(start, tk), :], ones], axis=1)
        if masked:
            q_pos = i * tq + lax.broadcasted_iota(jnp.int32, (tq, tk), 0)
            k_pos = j * tk + lax.broadcasted_iota(jnp.int32, (tq, tk), 1)
            keep = k_pos <= q_pos
        for hh in range(2):
            ck = c_ref[j, pl.ds(2 * p + hh, 1), :]
            s = _dot_nt(q_ref[0, hh], kj) - ck
            if masked:
                s = jnp.where(keep, s, -jnp.inf)
            m_prev = m_ref[hh]
            m_new = jnp.maximum(m_prev, jnp.max(s, axis=-1, keepdims=True))
            alpha = jnp.exp(m_prev - m_new)
            pr = jnp.exp(s - m_new)
            acc_ref[hh] = alpha * acc_ref[hh] + _dot(pr.astype(BF16), vj)
            m_ref[hh] = m_new

    def body(j, carry):
        tile(j, masked=False)
        return carry

    lax.fori_loop(0, i * n_sub, body, 0)
    for d in range(n_sub):
        tile(i * n_sub + d, masked=True)

    acc_e = acc_ref[0]
    acc_o = acc_ref[1]
    is_even = _lane_iota((tq, LANES)) < HEAD_DIM
    o = jnp.where(is_even, acc_e[:, 0:LANES] / acc_e[:, LANES:2 * LANES],
                  acc_o[:, 0:LANES] / acc_o[:, LANES:2 * LANES])
    o2 = o * o
    ss_e = jnp.sum(jnp.where(is_even, o2, 0.0), axis=-1, keepdims=True)
    ss_o = jnp.sum(jnp.where(is_even, 0.0, o2), axis=-1, keepdims=True)
    ms = jnp.where(is_even, ss_e, ss_o) * (1.0 / HEAD_DIM)
    z = z_ref[...]
    o_ref[...] = (o * lax.rsqrt(ms + EPS) * nw_ref[...] * (z * _sigmoid(z))).astype(o_ref.dtype)


def _fox(qm, kn, vb, ct, proj, nw_pair, *, tq, tk):
    s = kn.shape[0]
    return pl.pallas_call(
        functools.partial(_fox_kernel, tq=tq, tk=tk),
        grid=(N_PAIRS, s // tq),
        in_specs=[
            pl.BlockSpec((1, 2, tq, LANES), lambda p, i: (p, 0, i, 0)),
            pl.BlockSpec((s, LANES), lambda p, i: (0, p)),
            pl.BlockSpec((s, LANES), lambda p, i: (0, p)),
            pl.BlockSpec(ct.shape, lambda p, i: (0, 0, 0)),
            pl.BlockSpec((tq, LANES), lambda p, i: (i, COL_FOX_Z // LANES + p)),
            pl.BlockSpec((1, LANES), lambda p, i: (0, 0)),
        ],
        out_specs=pl.BlockSpec((tq, LANES), lambda p, i: (i, p)),
        out_shape=jax.ShapeDtypeStruct((s, QK_W), BF16),
        scratch_shapes=[
            pltpu.VMEM((2, tq, 1), F32),
            pltpu.VMEM((2, tq, 2 * LANES), F32),
        ],
        compiler_params=pltpu.CompilerParams(
            dimension_semantics=("arbitrary", "arbitrary"), vmem_limit_bytes=VMEM_LIMIT),
        name="fox",
    )(qm, kn, vb, ct, proj, nw_pair)


def _merge_kernel(x_ref, ya_ref, yb_ref, ga_ref, gb_ref, gbias_ref, pa_ref, pb_ref, wo_ref,
                  fw_ref, o_ref):
    d = x_ref.shape[1]
    ua = _dot(ya_ref[...].astype(BF16), pa_ref[...])
    ub = _dot(yb_ref[...].astype(BF16), pb_ref[...])
    ga = _sigmoid(ga_ref[...] + gbias_ref[:, 0:d])
    gb = _sigmoid(gb_ref[...] + gbias_ref[:, d:2 * d])
    merged = ga * ua + gb * ub
    xn = x_ref[...] + _dot(merged.astype(BF16), wo_ref[...])
    ms = jnp.mean(xn * xn, axis=-1, keepdims=True)
    o_ref[...] = xn * lax.rsqrt(ms + EPS) * fw_ref[...]


def _merge(x, ya, yb, proj, gate_b, w_up_a, w_up_b, w_out, out_w, *, tm=512):
    s, d = x.shape
    const = lambda shape: pl.BlockSpec(shape, lambda i: (0,) * len(shape))
    return pl.pallas_call(
        _merge_kernel,
        grid=(s // tm,),
        in_specs=[
            pl.BlockSpec((tm, d), lambda i: (i, 0)),
            pl.BlockSpec((tm, QK_W), lambda i: (i, 0)),
            pl.BlockSpec((tm, QK_W), lambda i: (i, 0)),
            pl.BlockSpec((tm, d), lambda i: (i, COL_GATE // d)),
            pl.BlockSpec((tm, d), lambda i: (i, COL_GATE // d + 1)),
            const((1, 2 * d)),
            const((QK_W, d)),
            const((QK_W, d)),
            const((d, d)),
            const((1, d)),
        ],
        out_specs=pl.BlockSpec((tm, d), lambda i: (i, 0)),
        out_shape=jax.ShapeDtypeStruct((s, d), F32),
        compiler_params=pltpu.CompilerParams(
            dimension_semantics=("arbitrary",), vmem_limit_bytes=VMEM_LIMIT),
        name="merge",
    )(x, ya, yb, proj, proj, gate_b, w_up_a, w_up_b, w_out, out_w)


def _block_diag_ones(n, blk):
    idx = np.arange(n) // blk
    return jnp.asarray(idx[:, None] == idx[None, :], dtype=BF16)


def _lower_tri_ones(n):
    idx = np.arange(n)
    return jnp.asarray(idx[None, :] <= idx[:, None], dtype=BF16)


def _expand_matrix():
    e = np.zeros((LANES, 2 * QK_W + N_HEADS * LANES), np.float32)
    for h in range(N_HEADS):
        e[SMALL_B + h, h * HEAD_DIM:(h + 1) * HEAD_DIM] = 1.0
        e[SMALL_A + h, QK_W + h * HEAD_DIM:QK_W + (h + 1) * HEAD_DIM] = 1.0
        e[SMALL_A + h, 2 * QK_W + h * LANES:2 * QK_W + (h + 1) * LANES] = 1.0
    return jnp.asarray(e, dtype=BF16)


def _pad_lanes(vec, offset):
    return jnp.zeros((1, LANES), F32).at[0, offset:offset + vec.shape[0]].set(vec.astype(F32))


def _regroup_w_in(w):
    d = w.shape[0]
    o_z = 3 * QK_W
    o_b = o_z + QK_W
    o_a = o_b + N_HEADS
    o_fq = o_a + N_HEADS
    o_ff = o_fq + 4 * QK_W
    o_gate = o_ff + N_HEADS
    small = jnp.concatenate(
        [w[:, o_b:o_a], w[:, o_a:o_fq], w[:, o_ff:o_gate],
         jnp.zeros((d, LANES - 3 * N_HEADS), w.dtype)], axis=1)
    return jnp.concatenate([w[:, 0:o_b], w[:, o_fq:o_ff], w[:, o_gate:], small], axis=1).astype(BF16)


FOX_TQ = 512
FOX_TK = 512


def kernel(x, norm_w, w_in, gate_b, conv_w, a_log, dt_bias, gdn_norm_w, f_bias,
           fox_qn_w, fox_kn_w, fox_on_w, w_up_gdn, w_up_fox, w_out, final_norm_w):
    batch, s, d = x.shape
    assert norm_w.shape[0] == 1, "the merge call fuses the final RMSNorm: single-layer trunk only"
    assert s % FOX_TQ == 0 and s % GDN_CHUNK == 0
    assert w_in.shape[2] == P_WIDTH - LANES + 3 * N_HEADS
    l = 0

    bd = _block_diag_ones(QK_W, HEAD_DIM)
    tri_gdn = _lower_tri_ones(GDN_CHUNK)
    tri_fox = _lower_tri_ones(FOX_TK)
    expand = _expand_matrix()
    tile_heads = lambda w: jnp.tile(w.astype(F32), N_HEADS)[None, :]

    outs = []
    for b in range(batch):
        xb = x[b]
        proj = _proj(xb, norm_w[l][None, :], _regroup_w_in(w_in[l]))
        ya = _gdn(proj, conv_w[l], _pad_lanes(a_log[l], SMALL_A), _pad_lanes(dt_bias[l], SMALL_A),
                  tile_heads(gdn_norm_w[l]), bd, tri_gdn, expand)
        qm, kn, vb, ct = _foxprep(proj, tile_heads(fox_qn_w[l]), tile_heads(fox_kn_w[l]),
                                  _pad_lanes(f_bias[l], SMALL_F), bd, tri_fox, rows=FOX_TK)
        yb = _fox(qm, kn, vb, ct, proj, jnp.tile(fox_on_w[l].astype(F32), 2)[None, :],
                  tq=FOX_TQ, tk=FOX_TK)
        outs.append(_merge(xb, ya, yb, proj, gate_b[l][None, :].astype(F32),
                           w_up_gdn[l].astype(BF16), w_up_fox[l].astype(BF16),
                           w_out[l].astype(BF16), final_norm_w[None, :].astype(F32)))
    return jnp.stack(outs, axis=0)
```

```python
import functools

import numpy as np
import jax
import jax.numpy as jnp
from jax import lax
from jax.experimental import pallas as pl
from jax.experimental.pallas import tpu as pltpu

F32 = jnp.float32
BF16 = jnp.bfloat16

EPS = 1e-6
LANES = 128
HEAD_DIM = 64
N_HEADS = 8
N_PAIRS = N_HEADS // 2
QK_W = N_HEADS * HEAD_DIM
CONV_K = 4
GDN_CHUNK = 128
VMEM_LIMIT = 56 * 1024 * 1024

COL_GDN_QKV = 0
COL_GDN_Z = 1536
COL_FOX_Q = 2048
COL_FOX_K = 2560
COL_FOX_V = 3072
COL_FOX_Z = 3584
COL_GATE = 4096
COL_SMALL = 6144
P_WIDTH = 6272
SMALL_B, SMALL_A, SMALL_F = 0, 8, 16


def _dot(a, b):
    return jnp.dot(a, b, preferred_element_type=F32)


def _dot_nt(a, b):
    return lax.dot_general(a, b, (((1,), (1,)), ((), ())), preferred_element_type=F32)


def _split3(x):
    hi = x.astype(BF16)
    r = x - hi.astype(F32)
    mid = r.astype(BF16)
    lo = (r - mid.astype(F32)).astype(BF16)
    return hi, mid, lo


def _dot_sel_rhs(x, sel):
    hi, mid, lo = _split3(x)
    return _dot(hi, sel) + _dot(mid, sel) + _dot(lo, sel)


def _dot_sel_lhs(sel, x):
    hi, mid, lo = _split3(x)
    return _dot(sel, hi) + _dot(sel, mid) + _dot(sel, lo)


def _head_sumsq(x, bd):
    y = x * x
    hi = y.astype(BF16)
    lo = (y - hi.astype(F32)).astype(BF16)
    return _dot(hi, bd) + _dot(lo, bd)


def _sigmoid(x):
    return 1.0 / (1.0 + jnp.exp(-x))


def _softplus(x):
    return jnp.maximum(x, 0.0) + jnp.log1p(jnp.exp(-jnp.abs(x)))


def _lane_iota(shape):
    return lax.broadcasted_iota(jnp.int32, shape, len(shape) - 1)


def _proj_kernel(x_ref, nw_ref, w_ref, o_ref, *, n_chunk):
    x = x_ref[...]
    ms = jnp.mean(x * x, axis=-1, keepdims=True)
    h = (x * lax.rsqrt(ms + EPS) * nw_ref[...]).astype(BF16)
    width = o_ref.shape[1]
    for c0 in range(0, width, n_chunk):
        c1 = min(c0 + n_chunk, width)
        o_ref[:, c0:c1] = _dot(h, w_ref[:, c0:c1])


def _proj(x, norm_w, w_all, *, tm=256):
    s, d = x.shape
    width = w_all.shape[1]
    return pl.pallas_call(
        functools.partial(_proj_kernel, n_chunk=1024),
        grid=(s // tm,),
        in_specs=[
            pl.BlockSpec((tm, d), lambda i: (i, 0)),
            pl.BlockSpec((1, d), lambda i: (0, 0)),
            pl.BlockSpec((d, width), lambda i: (0, 0)),
        ],
        out_specs=pl.BlockSpec((tm, width), lambda i: (i, 0)),
        out_shape=jax.ShapeDtypeStruct((s, width), F32),
        compiler_params=pltpu.CompilerParams(
            dimension_semantics=("arbitrary",), vmem_limit_bytes=VMEM_LIMIT),
        name="proj",
    )(x, norm_w, w_all)


INV_BASE_LOG2 = 5


def _shr(x, n):
    return lax.shift_right_logical(x, n)


def _inverse_masks(row, col):
    c = row.shape[0]
    lb = INV_BASE_LOG2
    diag = _shr(row, lb) == _shr(col, lb)
    levels = []
    b = lb
    while (1 << b) < c:
        in_pair = _shr(row, b + 1) == _shr(col, b + 1)
        levels.append(in_pair & ((_shr(row, b) & 1) == 1) & ((_shr(col, b) & 1) == 0))
        b += 1
    return diag, levels


def _unit_lower_inverse(m, eye, masks):
    c = m.shape[0]
    diag, levels = masks
    md = jnp.where(diag, m, 0.0)
    t = eye - md
    mdb = md.astype(BF16)
    mp = _dot(mdb, mdb)
    n_sq = INV_BASE_LOG2 - 1
    for it in range(n_sq):
        mpb = mp.astype(BF16)
        if it < n_sq - 1:
            both = _dot(jnp.concatenate([t, mp], axis=0).astype(BF16), mpb)
            t = t + both[0:c, :]
            mp = both[c:2 * c, :]
        else:
            t = t + _dot(t.astype(BF16), mpb)
    for level in levels:
        low = jnp.where(level, m, 0.0).astype(BF16)
        tb = t.astype(BF16)
        t = t - _dot(_dot(tb, low).astype(BF16), tb)
    return t


def _gdn_kernel(qkv_ref, z_ref, sm_ref, cw_ref, alog_ref, dtb_ref, nw_ref,
                bd_ref, tri_ref, ex_ref, o_ref, xext_ref, state_ref):
    c = GDN_CHUNK
    step = pl.program_id(0)

    @pl.when(step == 0)
    def _():
        xext_ref[0:8, :] = jnp.zeros((8, xext_ref.shape[1]), F32)
        state_ref[...] = jnp.zeros(state_ref.shape, F32)

    xext_ref[8:8 + c, :] = qkv_ref[...]
    y = cw_ref[3:4, :] * xext_ref[8:8 + c, :]
    for j in range(CONV_K - 1):
        y = y + cw_ref[j:j + 1, :] * xext_ref[5 + j:5 + j + c, :]
    xext_ref[0:8, :] = qkv_ref[c - 8:c, :]
    y = y * _sigmoid(y)

    bd = bd_ref[...]
    q = y[:, 0:QK_W]
    k = y[:, QK_W:2 * QK_W]
    v = y[:, 2 * QK_W:3 * QK_W]
    q = q * lax.rsqrt(_head_sumsq(q, bd) + EPS) * (HEAD_DIM ** -0.5)
    k = k * lax.rsqrt(_head_sumsq(k, bd) + EPS)

    sm = sm_ref[...]
    beta_s = _sigmoid(sm)
    g_s = -jnp.exp(alog_ref[...]) * _softplus(sm + dtb_ref[...])
    g_cum = _dot_sel_lhs(tri_ref[...], g_s)
    beta = _dot_sel_rhs(beta_s, ex_ref[:, 0:QK_W])
    g = _dot_sel_rhs(g_cum, ex_ref[:, QK_W:2 * QK_W])
    g_bc = _dot_sel_rhs(g_cum, ex_ref[:, 2 * QK_W:2 * QK_W + N_HEADS * LANES])
    g_t = g_cum.T

    eg = jnp.exp(g)
    g_last = g[c - 1:c, :]
    e_last = jnp.exp(g_last)
    kb = k * beta
    vb = v * beta
    kbg = kb * eg
    qg = q * eg
    kt = k * jnp.exp(g_last - g)

    row = lax.broadcasted_iota(jnp.int32, (c, c), 0)
    col = lax.broadcasted_iota(jnp.int32, (c, c), 1)
    causal = col <= row
    strict = col < row
    eye = (col == row).astype(F32)
    inv_masks = _inverse_masks(row, col)
    lane = _lane_iota((c, LANES))
    is_even = lane < HEAD_DIM
    bd_mask = (lane < HEAD_DIM) == (lax.broadcasted_iota(jnp.int32, (c, LANES), 0) < HEAD_DIM)

    for p in range(N_PAIRS):
        ls = slice(p * LANES, (p + 1) * LANES)
        k_p = k[:, ls]
        k_pb = k_p.astype(BF16)
        kb_p = kb[:, ls]
        q_p = q[:, ls]
        rhs_wk = jnp.concatenate([vb[:, ls], kbg[:, ls]], axis=1).astype(BF16)
        w_h, kc_h, a_h = [], [], []
        for hh in range(2):
            h = 2 * p + hh
            sel = is_even if hh == 0 else jnp.logical_not(is_even)
            diff = g_bc[:, h * LANES:(h + 1) * LANES] - g_t[SMALL_A + h:SMALL_A + h + 1, :]
            dec = jnp.where(causal, jnp.exp(jnp.where(causal, diff, 0.0)), 0.0)
            kk = _dot_nt(jnp.where(sel, kb_p, 0.0).astype(BF16), k_pb)
            m = jnp.where(strict, kk * dec, 0.0)
            t = _unit_lower_inverse(m, eye, inv_masks)
            wk = _dot(t.astype(BF16), rhs_wk)
            w_h.append(wk[:, 0:LANES])
            kc_h.append(wk[:, LANES:2 * LANES])
            a_h.append(_dot_nt(jnp.where(sel, q_p, 0.0).astype(BF16), k_pb) * dec)
        w_pair = jnp.where(is_even, w_h[0], w_h[1])
        kc_pair = jnp.where(is_even, kc_h[0], kc_h[1])

        st = state_ref[p]
        stb = st.astype(BF16)
        u = w_pair - _dot(kc_pair.astype(BF16), stb)
        ub = u.astype(BF16)
        o = _dot(qg[:, ls].astype(BF16), stb) + jnp.where(
            is_even, _dot(a_h[0].astype(BF16), ub), _dot(a_h[1].astype(BF16), ub))
        upd = _dot(kt[:, ls].T.astype(BF16), ub)
        state_ref[p] = st * e_last[:, ls] + jnp.where(bd_mask, upd, 0.0)
        o_ref[:, ls] = o.astype(o_ref.dtype)

    o_all = o_ref[...].astype(F32)
    z = z_ref[...]
    ms = _head_sumsq(o_all, bd) * (1.0 / HEAD_DIM)
    yn = o_all * lax.rsqrt(ms + EPS) * nw_ref[...]
    o_ref[...] = (yn * (z * _sigmoid(z))).astype(o_ref.dtype)


def _gdn(proj, conv_w, a_log_pad, dtb_pad, nw_tiled, bd, tri, expand):
    s = proj.shape[0]
    c = GDN_CHUNK
    conv_width = 3 * QK_W
    const = lambda shape: pl.BlockSpec(shape, lambda i: (0,) * len(shape))
    return pl.pallas_call(
        _gdn_kernel,
        grid=(s // c,),
        in_specs=[
            pl.BlockSpec((c, conv_width), lambda i: (i, COL_GDN_QKV // conv_width)),
            pl.BlockSpec((c, QK_W), lambda i: (i, COL_GDN_Z // QK_W)),
            pl.BlockSpec((c, LANES), lambda i: (i, COL_SMALL // LANES)),
            const((CONV_K, conv_width)),
            const((1, LANES)),
            const((1, LANES)),
            const((1, QK_W)),
            const((QK_W, QK_W)),
            const((c, c)),
            const(expand.shape),
        ],
        out_specs=pl.BlockSpec((c, QK_W), lambda i: (i, 0)),
        out_shape=jax.ShapeDtypeStruct((s, QK_W), F32),
        scratch_shapes=[
            pltpu.VMEM((c + 8, conv_width), F32),
            pltpu.VMEM((N_PAIRS, LANES, LANES), F32),
        ],
        compiler_params=pltpu.CompilerParams(
            dimension_semantics=("arbitrary",), vmem_limit_bytes=VMEM_LIMIT),
        name="gdn",
    )(proj, proj, proj, conv_w, a_log_pad, dtb_pad, nw_tiled, bd, tri, expand)


X_ROW, X_EVEN, X_ODD = 0, 3, 6


def _lane_terms(lane, base, terms, otherwise):
    out = otherwise
    for t, col in enumerate(terms):
        out = jnp.where(lane == base + t, col, out)
    return out


def _split3_f32(x):
    hi, mid, lo = _split3(x)
    return hi.astype(F32), mid.astype(F32), lo.astype(F32)


def _foxprep_kernel(q_ref, k_ref, v_ref, sm_ref, qw_ref, kw_ref, fb_ref, bd_ref, tri_ref,
                    qa_ref, ka_ref, vo_ref, ct_ref, bmax_ref, carry_ref, kmax_ref):
    @pl.when(pl.program_id(0) == 0)
    def _():
        carry_ref[...] = jnp.zeros(carry_ref.shape, F32)
        kmax_ref[...] = jnp.zeros(kmax_ref.shape, F32)
        bmax_ref[...] = jnp.zeros(bmax_ref.shape, F32)

    bd = bd_ref[...]

    def headnorm(x, w):
        ms = _head_sumsq(x, bd) * (1.0 / HEAD_DIM)
        return x * lax.rsqrt(ms + EPS) * w

    qb = (headnorm(q_ref[...], qw_ref[...]) * (HEAD_DIM ** -0.5)).astype(BF16)
    kb = headnorm(k_ref[...], kw_ref[...]).astype(BF16)
    vo_ref[...] = v_ref[...].astype(vo_ref.dtype)
    rows = qb.shape[0]

    xf = sm_ref[...] + fb_ref[...]
    log_f = jnp.minimum(xf, 0.0) - jnp.log1p(jnp.exp(-jnp.abs(xf)))
    cum = carry_ref[...] + _dot_sel_lhs(tri_ref[...], log_f)
    carry_ref[...] = cum[rows - 1:rows, :]
    ct_ref[0] = cum.T[SMALL_F:SMALL_F + N_HEADS, :]

    qf = qb.astype(F32)
    kf = kb.astype(F32)
    kmax = jnp.maximum(kmax_ref[...], jnp.max(_head_sumsq(kf, bd), axis=0, keepdims=True))
    kmax_ref[...] = kmax
    bound = jnp.sqrt(_head_sumsq(qf, bd) * kmax)
    bmax_ref[...] = jnp.maximum(bmax_ref[...], jnp.max(bound, axis=0, keepdims=True))

    lane = _lane_iota((rows, LANES))
    is_even = lane < HEAD_DIM
    ones3 =lambda base: ((lane >= base) & (lane < base + 3)).astype(F32)
    for p in range(N_PAIRS):
        ls = slice(p * LANES, (p + 1) * LANES)
        neg_c = []
        for hh in range(2):
            h = 2 * p + hh
            c_col = cum[:, SMALL_F + h:SMALL_F + h + 1]
            b_col = bound[:, h * HEAD_DIM:h * HEAD_DIM + 1]
            neg_c.append(_split3_f32(-c_col))
            qx = _lane_terms(lane, X_ROW, _split3_f32(c_col - b_col),
                             ones3(X_EVEN if hh == 0 else X_ODD))
            keep = is_even if hh == 0 else jnp.logical_not(is_even)
            qa_ref[p, hh, :, 0:LANES] = jnp.where(keep, qb[:, ls], jnp.zeros_like(qb[:, ls]))
            qa_ref[p, hh, :, LANES:2 * LANES] = qx.astype(qa_ref.dtype)
        kx = _lane_terms(lane, X_EVEN, neg_c[0], ones3(X_ROW))
        kx = _lane_terms(lane, X_ODD, neg_c[1], kx)
        ka_ref[:, 2 * p * LANES:(2 * p + 1) * LANES] = kb[:, ls]
        ka_ref[:, (2 * p + 1) * LANES:(2 * p + 2) * LANES] = kx.astype(ka_ref.dtype)


def _foxprep(proj, qw_tiled, kw_tiled, fb_pad, bd, tri, *, rows):
    s = proj.shape[0]
    const = lambda shape: pl.BlockSpec(shape, lambda i: (0,) * len(shape))
    return pl.pallas_call(
        _foxprep_kernel,
        grid=(s // rows,),
        in_specs=[
            pl.BlockSpec((rows, QK_W), lambda i: (i, COL_FOX_Q // QK_W)),
            pl.BlockSpec((rows, QK_W), lambda i: (i, COL_FOX_K // QK_W)),
            pl.BlockSpec((rows, QK_W), lambda i: (i, COL_FOX_V // QK_W)),
            pl.BlockSpec((rows, LANES), lambda i: (i, COL_SMALL // LANES)),
            const((1, QK_W)),
            const((1, QK_W)),
            const((1, LANES)),
            const((QK_W, QK_W)),
            const((rows, rows)),
        ],
        out_specs=[
            pl.BlockSpec((N_PAIRS, 2, rows, 2 * LANES), lambda i: (0, 0, i, 0)),
            pl.BlockSpec((rows, 2 * QK_W), lambda i: (i, 0)),
            pl.BlockSpec((rows, QK_W), lambda i: (i, 0)),
            pl.BlockSpec((1, N_HEADS, rows), lambda i: (i, 0, 0)),
            pl.BlockSpec((1, QK_W), lambda i: (0, 0)),
        ],
        out_shape=[
            jax.ShapeDtypeStruct((N_PAIRS, 2, s, 2 * LANES), BF16),
            jax.ShapeDtypeStruct((s, 2 * QK_W), BF16),
            jax.ShapeDtypeStruct((s, QK_W), BF16),
            jax.ShapeDtypeStruct((s // rows, N_HEADS, rows), F32),
            jax.ShapeDtypeStruct((1, QK_W), F32),
        ],
        scratch_shapes=[pltpu.VMEM((1, LANES), F32), pltpu.VMEM((1, QK_W), F32)],
        compiler_params=pltpu.CompilerParams(
            dimension_semantics=("arbitrary",), vmem_limit_bytes=VMEM_LIMIT),
        name="foxprep",
    )(proj, proj, proj, proj, qw_tiled, kw_tiled, fb_pad, bd, tri)


FAST_BOUND_MAX = 40.0


def _fox_kernel(fast_ref, q_ref, k_ref, v_ref, c_ref, z_ref, nw_ref, o_ref, m_ref, acc_ref,
                *, tq, tk):
    p = pl.program_id(0)
    i = pl.program_id(1)
    n_sub = tq // tk

    acc_ref[...] = jnp.zeros(acc_ref.shape, F32)
    ones = jnp.ones((tk, LANES), BF16)

    def kv_tile(j):
        start = pl.multiple_of(j * tk, tk)
        vj = jnp.concatenate([v_ref[pl.ds(start, tk), :], ones], axis=1)
        return k_ref[pl.ds(start, tk), :], vj

    def causal_keep(j):
        q_pos = i * tq + lax.broadcasted_iota(jnp.int32, (tq, tk), 0)
        k_pos = j * tk + lax.broadcasted_iota(jnp.int32, (tq, tk), 1)
        return k_pos <= q_pos

    def fast_tile(j, masked):
        kj, vj = kv_tile(j)
        for hh in range(2):
            s = _dot_nt(q_ref[0, hh], kj)
            if masked:
                s = jnp.where(causal_keep(j), s, -jnp.inf)
            acc_ref[hh] += _dot(jnp.exp(s).astype(BF16), vj)

    def slow_tile(j, masked):
        kj, vj = kv_tile(j)
        for hh in range(2):
            ck = c_ref[j, pl.ds(2 * p + hh, 1), :]
            s = _dot_nt(q_ref[0, hh, :, 0:LANES], kj[:, 0:LANES]) - ck
            if masked:
                s = jnp.where(causal_keep(j), s, -jnp.inf)
            m_prev = m_ref[hh]
            m_new = jnp.maximum(m_prev, jnp.max(s, axis=-1, keepdims=True))
            alpha = jnp.exp(m_prev - m_new)
            pr = jnp.exp(s - m_new)
            acc_ref[hh] = alpha * acc_ref[hh] + _dot(pr.astype(BF16), vj)
            m_ref[hh] = m_new

    def run(tile):
        def body(j, carry):
            tile(j, False)
            return carry
        lax.fori_loop(0, i * n_sub, body, 0)
        for d in range(n_sub):
            tile(i * n_sub + d, True)

    @pl.when(fast_ref[0] == 1)
    def _():
        run(fast_tile)

    @pl.when(fast_ref[0] != 1)
    def _():
        m_ref[...] = jnp.full(m_ref.shape, -jnp.inf, F32)
        run(slow_tile)

    acc_e = acc_ref[0]
    acc_o = acc_ref[1]
    is_even = _lane_iota((tq, LANES)) < HEAD_DIM
    o = jnp.where(is_even, acc_e[:, 0:LANES] / acc_e[:, LANES:2 * LANES],
                  acc_o[:, 0:LANES] / acc_o[:, LANES:2 * LANES])
    o2 = o * o
    ss_e = jnp.sum(jnp.where(is_even, o2, 0.0), axis=-1, keepdims=True)
    ss_o = jnp.sum(jnp.where(is_even, 0.0, o2), axis=-1, keepdims=True)
    ms = jnp.where(is_even, ss_e, ss_o) * (1.0 / HEAD_DIM)
    z = z_ref[...]
    o_ref[...] = (o * lax.rsqrt(ms + EPS) * nw_ref[...] * (z * _sigmoid(z))).astype(o_ref.dtype)


def _fox(fast, qa, ka, vb, ct, proj, nw_pair, *, tq, tk):
    s = vb.shape[0]
    grid_spec = pltpu.PrefetchScalarGridSpec(
        num_scalar_prefetch=1,
        grid=(N_PAIRS, s // tq),
        in_specs=[
            pl.BlockSpec((1, 2, tq, 2 * LANES), lambda p, i, f: (p, 0, i, 0)),
            pl.BlockSpec((s, 2 * LANES), lambda p, i, f: (0, p)),
            pl.BlockSpec((s, LANES), lambda p, i, f: (0, p)),
            pl.BlockSpec(ct.shape, lambda p, i, f: (0, 0, 0)),
            pl.BlockSpec((tq, LANES), lambda p, i, f: (i, COL_FOX_Z // LANES + p)),
            pl.BlockSpec((1, LANES), lambda p, i, f: (0, 0)),
        ],
        out_specs=pl.BlockSpec((tq, LANES), lambda p, i, f: (i, p)),
        scratch_shapes=[
            pltpu.VMEM((2, tq, 1), F32),
            pltpu.VMEM((2, tq, 2 * LANES), F32),
        ],
    )
    return pl.pallas_call(
        functools.partial(_fox_kernel, tq=tq, tk=tk),
        grid_spec=grid_spec,
        out_shape=jax.ShapeDtypeStruct((s, QK_W), BF16),
        compiler_params=pltpu.CompilerParams(
            dimension_semantics=("arbitrary", "arbitrary"), vmem_limit_bytes=VMEM_LIMIT),
        name="fox",
    )(fast, qa, ka, vb, ct, proj, nw_pair)


def _merge_kernel(x_ref, ya_ref, yb_ref, ga_ref, gb_ref, gbias_ref, pa_ref, pb_ref, wo_ref,
                  fw_ref, o_ref):
    d = x_ref.shape[1]
    ua = _dot(ya_ref[...].astype(BF16), pa_ref[...])
    ub = _dot(yb_ref[...].astype(BF16), pb_ref[...])
    ga = _sigmoid(ga_ref[...] + gbias_ref[:, 0:d])
    gb = _sigmoid(gb_ref[...] + gbias_ref[:, d:2 * d])
    merged = ga * ua + gb * ub
    xn = x_ref[...] + _dot(merged.astype(BF16), wo_ref[...])
    ms = jnp.mean(xn * xn, axis=-1, keepdims=True)
    o_ref[...] = xn * lax.rsqrt(ms + EPS) * fw_ref[...]


def _merge(x, ya, yb, proj, gate_b, w_up_a, w_up_b, w_out, out_w, *, tm=512):
    s, d = x.shape
    const = lambda shape: pl.BlockSpec(shape, lambda i: (0,) * len(shape))
    return pl.pallas_call(
        _merge_kernel,
        grid=(s // tm,),
        in_specs=[
            pl.BlockSpec((tm, d), lambda i: (i, 0)),
            pl.BlockSpec((tm, QK_W), lambda i: (i, 0)),
            pl.BlockSpec((tm, QK_W), lambda i: (i, 0)),
            pl.BlockSpec((tm, d), lambda i: (i, COL_GATE // d)),
            pl.BlockSpec((tm, d), lambda i: (i, COL_GATE // d + 1)),
            const((1, 2 * d)),
            const((QK_W, d)),
            const((QK_W, d)),
            const((d, d)),
            const((1, d)),
        ],
        out_specs=pl.BlockSpec((tm, d), lambda i: (i, 0)),
        out_shape=jax.ShapeDtypeStruct((s, d), F32),
        compiler_params=pltpu.CompilerParams(
            dimension_semantics=("arbitrary",), vmem_limit_bytes=VMEM_LIMIT),
        name="merge",
    )(x, ya, yb, proj, proj, gate_b, w_up_a, w_up_b, w_out, out_w)


def _block_diag_ones(n, blk):
    idx = np.arange(n) // blk
    return jnp.asarray(idx[:, None] == idx[None, :], dtype=BF16)


def _lower_tri_ones(n):
    idx = np.arange(n)
    return jnp.asarray(idx[None, :] <= idx[:, None], dtype=BF16)


def _expand_matrix():
    e = np.zeros((LANES, 2 * QK_W + N_HEADS * LANES), np.float32)
    for h in range(N_HEADS):
        e[SMALL_B + h, h * HEAD_DIM:(h + 1) * HEAD_DIM] = 1.0
        e[SMALL_A + h, QK_W + h * HEAD_DIM:QK_W + (h + 1) * HEAD_DIM] = 1.0
        e[SMALL_A + h, 2 * QK_W + h * LANES:2 * QK_W + (h + 1) * LANES] = 1.0
    return jnp.asarray(e, dtype=BF16)


def _pad_lanes(vec, offset):
    return jnp.zeros((1, LANES), F32).at[0, offset:offset + vec.shape[0]].set(vec.astype(F32))


def _regroup_w_in(w):
    d = w.shape[0]
    o_z = 3 * QK_W
    o_b = o_z + QK_W
    o_a = o_b + N_HEADS
    o_fq = o_a + N_HEADS
    o_ff = o_fq + 4 * QK_W
    o_gate = o_ff + N_HEADS
    small = jnp.concatenate(
        [w[:, o_b:o_a], w[:, o_a:o_fq], w[:, o_ff:o_gate],
         jnp.zeros((d, LANES - 3 * N_HEADS), w.dtype)], axis=1)
    return jnp.concatenate([w[:, 0:o_b], w[:, o_fq:o_ff], w[:, o_gate:], small], axis=1).astype(BF16)


FOX_TQ = 512
FOX_TK = 512


def kernel(x, norm_w, w_in, gate_b, conv_w, a_log, dt_bias, gdn_norm_w, f_bias,
           fox_qn_w, fox_kn_w, fox_on_w, w_up_gdn, w_up_fox, w_out, final_norm_w):
    batch, s, d = x.shape
    assert norm_w.shape[0] == 1, "the merge call fuses the final RMSNorm: single-layer trunk only"
    assert s % FOX_TQ == 0 and s % GDN_CHUNK == 0
    assert w_in.shape[2] == P_WIDTH - LANES + 3 * N_HEADS
    l = 0

    bd = _block_diag_ones(QK_W, HEAD_DIM)
    tri_gdn = _lower_tri_ones(GDN_CHUNK)
    tri_fox = _lower_tri_ones(FOX_TK)
    expand = _expand_matrix()
    tile_heads = lambda w: jnp.tile(w.astype(F32), N_HEADS)[None, :]

    outs = []
    for b in range(batch):
        xb = x[b]
        proj = _proj(xb, norm_w[l][None, :], _regroup_w_in(w_in[l]))
        ya = _gdn(proj, conv_w[l], _pad_lanes(a_log[l], SMALL_A), _pad_lanes(dt_bias[l], SMALL_A),
                  tile_heads(gdn_norm_w[l]), bd, tri_gdn, expand)
        qa, ka, vb, ct, bmax = _foxprep(proj, tile_heads(fox_qn_w[l]), tile_heads(fox_kn_w[l]),
                                        _pad_lanes(f_bias[l], SMALL_F), bd, tri_fox, rows=FOX_TK)
        fast = (jnp.max(bmax) <= FAST_BOUND_MAX).astype(jnp.int32).reshape(1)
        yb = _fox(fast, qa, ka, vb, ct, proj, jnp.tile(fox_on_w[l].astype(F32), 2)[None, :],
                  tq=FOX_TQ, tk=FOX_TK)
        outs.append(_merge(xb, ya, yb, proj, gate_b[l][None, :].astype(F32),
                           w_up_gdn[l].astype(BF16), w_up_fox[l].astype(BF16),
                           w_out[l].astype(BF16), final_norm_w[None, :].astype(F32)))
    return jnp.stack(outs, axis=0)
```

```python
import functools

import numpy as np
import jax
import jax.numpy as jnp
from jax import lax
from jax.experimental import pallas as pl
from jax.experimental.pallas import tpu as pltpu

F32 = jnp.float32
BF16 = jnp.bfloat16

EPS = 1e-6
LANES = 128
HEAD_DIM = 64
N_HEADS = 8
N_PAIRS = N_HEADS // 2
QK_W = N_HEADS * HEAD_DIM
CONV_K = 4
GDN_CHUNK = 128
VMEM_LIMIT = 56 * 1024 * 1024

COL_GDN_QKV = 0
COL_GDN_Z = 1536
COL_FOX_Q = 2048
COL_FOX_K = 2560
COL_FOX_V = 3072
COL_FOX_Z = 3584
COL_GATE = 4096
COL_SMALL = 6144
P_WIDTH = 6272
SMALL_B, SMALL_A, SMALL_F = 0, 8, 16


def _dot(a, b):
    return jnp.dot(a, b, preferred_element_type=F32)


def _dot_nt(a, b):
    return lax.dot_general(a, b, (((1,), (1,)), ((), ())), preferred_element_type=F32)


def _split3(x):
    hi = x.astype(BF16)
    r = x - hi.astype(F32)
    mid = r.astype(BF16)
    lo = (r - mid.astype(F32)).astype(BF16)
    return hi, mid, lo


def _dot_sel_rhs(x, sel):
    hi, mid, lo = _split3(x)
    return _dot(hi, sel) + _dot(mid, sel) + _dot(lo, sel)


def _dot_sel_lhs(sel, x):
    hi, mid, lo = _split3(x)
    return _dot(sel, hi) + _dot(sel, mid) + _dot(sel, lo)


def _head_sumsq(x, bd):
    y = x * x
    hi = y.astype(BF16)
    lo = (y - hi.astype(F32)).astype(BF16)
    return _dot(hi, bd) + _dot(lo, bd)


def _sigmoid(x):
    return 1.0 / (1.0 + jnp.exp(-x))


def _softplus(x):
    return jnp.maximum(x, 0.0) + jnp.log1p(jnp.exp(-jnp.abs(x)))


def _lane_iota(shape):
    return lax.broadcasted_iota(jnp.int32, shape, len(shape) - 1)


def _proj_kernel(x_ref, nw_ref, w_ref, o_ref, *, n_chunk):
    x = x_ref[...]
    ms = jnp.mean(x * x, axis=-1, keepdims=True)
    h = (x * lax.rsqrt(ms + EPS) * nw_ref[...]).astype(BF16)
    width = o_ref.shape[1]
    for c0 in range(0, width, n_chunk):
        c1 = min(c0 + n_chunk, width)
        o_ref[:, c0:c1] = _dot(h, w_ref[:, c0:c1])


def _proj(x, norm_w, w_all, *, tm=256):
    s, d = x.shape
    width = w_all.shape[1]
    return pl.pallas_call(
        functools.partial(_proj_kernel, n_chunk=1024),
        grid=(s // tm,),
        in_specs=[
            pl.BlockSpec((tm, d), lambda i: (i, 0)),
            pl.BlockSpec((1, d), lambda i: (0, 0)),
            pl.BlockSpec((d, width), lambda i: (0, 0)),
        ],
        out_specs=pl.BlockSpec((tm, width), lambda i: (i, 0)),
        out_shape=jax.ShapeDtypeStruct((s, width), F32),
        compiler_params=pltpu.CompilerParams(
            dimension_semantics=("arbitrary",), vmem_limit_bytes=VMEM_LIMIT),
        name="proj",
    )(x, norm_w, w_all)


INV_BASE_LOG2 = 5


def _shr(x, n):
    return lax.shift_right_logical(x, n)


def _inverse_masks(row, col):
    c = row.shape[0]
    lb = INV_BASE_LOG2
    diag = _shr(row, lb) == _shr(col, lb)
    levels = []
    b = lb
    while (1 << b) < c:
        in_pair = _shr(row, b + 1) == _shr(col, b + 1)
        levels.append(in_pair & ((_shr(row, b) & 1) == 1) & ((_shr(col, b) & 1) == 0))
        b += 1
    return diag, levels


def _unit_lower_inverse(m, eye, masks):
    c = m.shape[0]
    diag, levels = masks
    md = jnp.where(diag, m, 0.0)
    t = eye - md
    mdb = md.astype(BF16)
    mp = _dot(mdb, mdb)
    n_sq = INV_BASE_LOG2 - 1
    for it in range(n_sq):
        mpb = mp.astype(BF16)
        if it < n_sq - 1:
            both = _dot(jnp.concatenate([t, mp], axis=0).astype(BF16), mpb)
            t = t + both[0:c, :]
            mp = both[c:2 * c, :]
        else:
            t = t + _dot(t.astype(BF16), mpb)
    for level in levels:
        low = jnp.where(level, m, 0.0).astype(BF16)
        tb = t.astype(BF16)
        t = t - _dot(_dot(tb, low).astype(BF16), tb)
    return t


def _gdn_kernel(qkv_ref, z_ref, sm_ref, cw_ref, alog_ref, dtb_ref, nw_ref,
                bd_ref, tri_ref, ex_ref, o_ref, xext_ref, state_ref):
    c = GDN_CHUNK
    step = pl.program_id(0)

    @pl.when(step == 0)
    def _():
        xext_ref[0:8, :] = jnp.zeros((8, xext_ref.shape[1]), F32)
        state_ref[...] = jnp.zeros(state_ref.shape, F32)

    xext_ref[8:8 + c, :] = qkv_ref[...]
    y = cw_ref[3:4, :] * xext_ref[8:8 + c, :]
    for j in range(CONV_K - 1):
        y = y + cw_ref[j:j + 1, :] * xext_ref[5 + j:5 + j + c, :]
    xext_ref[0:8, :] = qkv_ref[c - 8:c, :]
    y = y * _sigmoid(y)

    bd = bd_ref[...]
    q = y[:, 0:QK_W]
    k = y[:, QK_W:2 * QK_W]
    v = y[:, 2 * QK_W:3 * QK_W]
    q = q * lax.rsqrt(_head_sumsq(q, bd) + EPS) * (HEAD_DIM ** -0.5)
    k = k * lax.rsqrt(_head_sumsq(k, bd) + EPS)

    sm = sm_ref[...]
    beta_s = _sigmoid(sm)
    g_s = -jnp.exp(alog_ref[...]) * _softplus(sm + dtb_ref[...])
    g_cum = _dot_sel_lhs(tri_ref[...], g_s)
    beta = _dot_sel_rhs(beta_s, ex_ref[:, 0:QK_W])
    g = _dot_sel_rhs(g_cum, ex_ref[:, QK_W:2 * QK_W])
    g_bc = _dot_sel_rhs(g_cum, ex_ref[:, 2 * QK_W:2 * QK_W + N_HEADS * LANES])
    g_t = g_cum.T

    eg = jnp.exp(g)
    g_last = g[c - 1:c, :]
    e_last = jnp.exp(g_last)
    kb = k * beta
    vb = v * beta
    kbg = kb * eg
    qg = q * eg
    kt = k * jnp.exp(g_last - g)

    row = lax.broadcasted_iota(jnp.int32, (c, c), 0)
    col = lax.broadcasted_iota(jnp.int32, (c, c), 1)
    causal = col <= row
    strict = col < row
    eye = (col == row).astype(F32)
    inv_masks = _inverse_masks(row, col)
    lane = _lane_iota((c, LANES))
    is_even = lane < HEAD_DIM
    bd_mask = (lane < HEAD_DIM) == (lax.broadcasted_iota(jnp.int32, (c, LANES), 0) < HEAD_DIM)

    for p in range(N_PAIRS):
        ls = slice(p * LANES, (p + 1) * LANES)
        k_p = k[:, ls]
        k_pb = k_p.astype(BF16)
        kb_p = kb[:, ls]
        q_p = q[:, ls]
        rhs_wk = jnp.concatenate([vb[:, ls], kbg[:, ls]], axis=1).astype(BF16)
        w_h, kc_h, a_h = [], [], []
        for hh in range(2):
            h = 2 * p + hh
            sel = is_even if hh == 0 else jnp.logical_not(is_even)
            diff = g_bc[:, h * LANES:(h + 1) * LANES] - g_t[SMALL_A + h:SMALL_A + h + 1, :]
            dec = jnp.where(causal, jnp.exp(jnp.where(causal, diff, 0.0)), 0.0)
            kk = _dot_nt(jnp.where(sel, kb_p, 0.0).astype(BF16), k_pb)
            m = jnp.where(strict, kk * dec, 0.0)
            t = _unit_lower_inverse(m, eye, inv_masks)
            wk = _dot(t.astype(BF16), rhs_wk)
            w_h.append(wk[:, 0:LANES])
            kc_h.append(wk[:, LANES:2 * LANES])
            a_h.append(_dot_nt(jnp.where(sel, q_p, 0.0).astype(BF16), k_pb) * dec)
        w_pair = jnp.where(is_even, w_h[0], w_h[1])
        kc_pair = jnp.where(is_even, kc_h[0], kc_h[1])

        st = state_ref[p]
        stb = st.astype(BF16)
        u = w_pair - _dot(kc_pair.astype(BF16), stb)
        ub = u.astype(BF16)
        o = _dot(qg[:, ls].astype(BF16), stb) + jnp.where(
            is_even, _dot(a_h[0].astype(BF16), ub), _dot(a_h[1].astype(BF16), ub))
        upd = _dot(kt[:, ls].T.astype(BF16), ub)
        state_ref[p] = st * e_last[:, ls] + jnp.where(bd_mask, upd, 0.0)
        o_ref[:, ls] = o.astype(o_ref.dtype)

    o_all = o_ref[...].astype(F32)
    z = z_ref[...]
    ms = _head_sumsq(o_all, bd) * (1.0 / HEAD_DIM)
    yn = o_all * lax.rsqrt(ms + EPS) * nw_ref[...]
    o_ref[...] = (yn * (z * _sigmoid(z))).astype(o_ref.dtype)


def _gdn(proj, conv_w, a_log_pad, dtb_pad, nw_tiled, bd, tri, expand):
    s = proj.shape[0]
    c = GDN_CHUNK
    conv_width = 3 * QK_W
    const = lambda shape: pl.BlockSpec(shape, lambda i: (0,) * len(shape))
    return pl.pallas_call(
        _gdn_kernel,
        grid=(s // c,),
        in_specs=[
            pl.BlockSpec((c, conv_width), lambda i: (i, COL_GDN_QKV // conv_width)),
            pl.BlockSpec((c, QK_W), lambda i: (i, COL_GDN_Z // QK_W)),
            pl.BlockSpec((c, LANES), lambda i: (i, COL_SMALL // LANES)),
            const((CONV_K, conv_width)),
            const((1, LANES)),
            const((1, LANES)),
            const((1, QK_W)),
            const((QK_W, QK_W)),
            const((c, c)),
            const(expand.shape),
        ],
        out_specs=pl.BlockSpec((c, QK_W), lambda i: (i, 0)),
        out_shape=jax.ShapeDtypeStruct((s, QK_W), F32),
        scratch_shapes=[
            pltpu.VMEM((c + 8, conv_width), F32),
            pltpu.VMEM((N_PAIRS, LANES, LANES), F32),
        ],
        compiler_params=pltpu.CompilerParams(
            dimension_semantics=("arbitrary",), vmem_limit_bytes=VMEM_LIMIT),
        name="gdn",
    )(proj, proj, proj, conv_w, a_log_pad, dtb_pad, nw_tiled, bd, tri, expand)


X_ROW, X_EVEN, X_ODD = 0, 3, 6


def _lane_terms(lane, base, terms, otherwise):
    out = otherwise
    for t, col in enumerate(terms):
        out = jnp.where(lane == base + t, col, out)
    return out


def _split3_f32(x):
    hi, mid, lo = _split3(x)
    return hi.astype(F32), mid.astype(F32), lo.astype(F32)


def _foxprep_kernel(q_ref, k_ref, v_ref, sm_ref, qw_ref, kw_ref, fb_ref, bd_ref, tri_ref,
                    qa_ref, ka_ref, vo_ref, ct_ref, bmax_ref, crange_ref, carry_ref, kmax_ref):
    @pl.when(pl.program_id(0) == 0)
    def _():
        carry_ref[...] = jnp.zeros(carry_ref.shape, F32)
        kmax_ref[...] = jnp.zeros(kmax_ref.shape, F32)
        bmax_ref[...] = jnp.zeros(bmax_ref.shape, F32)

    bd = bd_ref[...]

    def headnorm(x, w):
        ms = _head_sumsq(x, bd) * (1.0 / HEAD_DIM)
        return x * lax.rsqrt(ms + EPS) * w

    qb = (headnorm(q_ref[...], qw_ref[...]) * (HEAD_DIM ** -0.5)).astype(BF16)
    kb = headnorm(k_ref[...], kw_ref[...]).astype(BF16)
    vo_ref[...] = v_ref[...].astype(vo_ref.dtype)
    rows = qb.shape[0]

    xf = sm_ref[...] + fb_ref[...]
    log_f = jnp.minimum(xf, 0.0) - jnp.log1p(jnp.exp(-jnp.abs(xf)))
    cum = carry_ref[...] + _dot_sel_lhs(tri_ref[...], log_f)
    carry_ref[...] = cum[rows - 1:rows, :]
    ct_ref[0] = cum.T[SMALL_F:SMALL_F + N_HEADS, :]
    crange_ref[0, 0:1, :] = jnp.max(cum, axis=0, keepdims=True)
    crange_ref[0, 1:2, :] = jnp.min(cum, axis=0, keepdims=True)

    qf = qb.astype(F32)
    kf = kb.astype(F32)
    kmax = jnp.maximum(kmax_ref[...], jnp.max(_head_sumsq(kf, bd), axis=0, keepdims=True))
    kmax_ref[...] = kmax
    bound = jnp.sqrt(_head_sumsq(qf, bd) * kmax)
    bmax_ref[...] = jnp.maximum(bmax_ref[...], jnp.max(bound, axis=0, keepdims=True))

    lane = _lane_iota((rows, LANES))
    is_even = lane < HEAD_DIM
    ones3 =lambda base: ((lane >= base) & (lane < base + 3)).astype(F32)
    for p in range(N_PAIRS):
        ls = slice(p * LANES, (p + 1) * LANES)
        neg_c = []
        for hh in range(2):
            h = 2 * p + hh
            c_col = cum[:, SMALL_F + h:SMALL_F + h + 1]
            b_col = bound[:, h * HEAD_DIM:h * HEAD_DIM + 1]
            neg_c.append(_split3_f32(-c_col))
            qx = _lane_terms(lane, X_ROW, _split3_f32(c_col - b_col),
                             ones3(X_EVEN if hh == 0 else X_ODD))
            keep = is_even if hh == 0 else jnp.logical_not(is_even)
            qa_ref[p, hh, :, 0:LANES] = jnp.where(keep, qb[:, ls], jnp.zeros_like(qb[:, ls]))
            qa_ref[p, hh, :, LANES:2 * LANES] = qx.astype(qa_ref.dtype)
        kx = _lane_terms(lane, X_EVEN, neg_c[0], ones3(X_ROW))
        kx = _lane_terms(lane, X_ODD, neg_c[1], kx)
        ka_ref[:, 2 * p * LANES:(2 * p + 1) * LANES] = kb[:, ls]
        ka_ref[:, (2 * p + 1) * LANES:(2 * p + 2) * LANES] = kx.astype(ka_ref.dtype)


def _foxprep(proj, qw_tiled, kw_tiled, fb_pad, bd, tri, *, rows):
    s = proj.shape[0]
    const = lambda shape: pl.BlockSpec(shape, lambda i: (0,) * len(shape))
    return pl.pallas_call(
        _foxprep_kernel,
        grid=(s // rows,),
        in_specs=[
            pl.BlockSpec((rows, QK_W), lambda i: (i, COL_FOX_Q // QK_W)),
            pl.BlockSpec((rows, QK_W), lambda i: (i, COL_FOX_K // QK_W)),
            pl.BlockSpec((rows, QK_W), lambda i: (i, COL_FOX_V // QK_W)),
            pl.BlockSpec((rows, LANES), lambda i: (i, COL_SMALL // LANES)),
            const((1, QK_W)),
            const((1, QK_W)),
            const((1, LANES)),
            const((QK_W, QK_W)),
            const((rows, rows)),
        ],
        out_specs=[
            pl.BlockSpec((N_PAIRS, 2, rows, 2 * LANES), lambda i: (0, 0, i, 0)),
            pl.BlockSpec((rows, 2 * QK_W), lambda i: (i, 0)),
            pl.BlockSpec((rows, QK_W), lambda i: (i, 0)),
            pl.BlockSpec((1, N_HEADS, rows), lambda i: (i, 0, 0)),
            pl.BlockSpec((1, QK_W), lambda i: (0, 0)),
            pl.BlockSpec((1, 2, LANES), lambda i: (i, 0, 0)),
        ],
        out_shape=[
            jax.ShapeDtypeStruct((N_PAIRS, 2, s, 2 * LANES), BF16),
            jax.ShapeDtypeStruct((s, 2 * QK_W), BF16),
            jax.ShapeDtypeStruct((s, QK_W), BF16),
            jax.ShapeDtypeStruct((s // rows, N_HEADS, rows), F32),
            jax.ShapeDtypeStruct((1, QK_W), F32),
            jax.ShapeDtypeStruct((s // rows, 2, LANES), F32),
        ],
        scratch_shapes=[pltpu.VMEM((1, LANES), F32), pltpu.VMEM((1, QK_W), F32)],
        compiler_params=pltpu.CompilerParams(
            dimension_semantics=("arbitrary",), vmem_limit_bytes=VMEM_LIMIT),
        name="foxprep",
    )(proj, proj, proj, proj, qw_tiled, kw_tiled, fb_pad, bd, tri)


FAST_BOUND_MAX = 40.0


def _fox_kernel(fast_ref, first_ref, q_ref, k_ref, v_ref, c_ref, z_ref, nw_ref, o_ref, m_ref,
                acc_ref, *, tq, tk):
    p = pl.program_id(0)
    i = pl.program_id(1)
    first = first_ref[p * pl.num_programs(1) + i]
    n_sub = tq // tk

    acc_ref[...] = jnp.zeros(acc_ref.shape, F32)
    ones = jnp.ones((tk, LANES), BF16)

    def kv_tile(j):
        start = pl.multiple_of(j * tk, tk)
        vj = jnp.concatenate([v_ref[pl.ds(start, tk), :], ones], axis=1)
        return k_ref[pl.ds(start, tk), :], vj

    def causal_keep(j):
        q_pos = i * tq + lax.broadcasted_iota(jnp.int32, (tq, tk), 0)
        k_pos = j * tk + lax.broadcasted_iota(jnp.int32, (tq, tk), 1)
        return k_pos <= q_pos

    def fast_tile(j, masked):
        kj, vj = kv_tile(j)
        for hh in range(2):
            s = _dot_nt(q_ref[0, hh], kj)
            if masked:
                s = jnp.where(causal_keep(j), s, -jnp.inf)
            acc_ref[hh] += _dot(jnp.exp(s).astype(BF16), vj)

    def slow_tile(j, masked):
        kj, vj = kv_tile(j)
        for hh in range(2):
            ck = c_ref[j, pl.ds(2 * p + hh, 1), :]
            s = _dot_nt(q_ref[0, hh, :, 0:LANES], kj[:, 0:LANES]) - ck
            if masked:
                s = jnp.where(causal_keep(j), s, -jnp.inf)
            m_prev = m_ref[hh]
            m_new = jnp.maximum(m_prev, jnp.max(s, axis=-1, keepdims=True))
            alpha = jnp.exp(m_prev - m_new)
            pr = jnp.exp(s - m_new)
            acc_ref[hh] = alpha * acc_ref[hh] + _dot(pr.astype(BF16), vj)
            m_ref[hh] = m_new

    def run(tile):
        def body(j, carry):
            tile(j, False)
            return carry
        lax.fori_loop(first, i * n_sub, body, 0)
        for d in range(n_sub):
            tile(i * n_sub + d, True)

    @pl.when(fast_ref[0] == 1)
    def _():
        run(fast_tile)

    @pl.when(fast_ref[0] != 1)
    def _():
        m_ref[...] = jnp.full(m_ref.shape, -jnp.inf, F32)
        run(slow_tile)

    acc_e = acc_ref[0]
    acc_o = acc_ref[1]
    is_even = _lane_iota((tq, LANES)) < HEAD_DIM
    o = jnp.where(is_even, acc_e[:, 0:LANES] / acc_e[:, LANES:2 * LANES],
                  acc_o[:, 0:LANES] / acc_o[:, LANES:2 * LANES])
    o2 = o * o
    ss_e = jnp.sum(jnp.where(is_even, o2, 0.0), axis=-1, keepdims=True)
    ss_o = jnp.sum(jnp.where(is_even, 0.0, o2), axis=-1, keepdims=True)
    ms = jnp.where(is_even, ss_e, ss_o) * (1.0 / HEAD_DIM)
    z = z_ref[...]
    o_ref[...] = (o * lax.rsqrt(ms + EPS) * nw_ref[...] * (z * _sigmoid(z))).astype(o_ref.dtype)


SKIP_LOG = 110.0


def _first_key_tile(crange, bmax, *, tq, tk):
    n_k = crange.shape[0]
    n_sub = tq // tk
    c_max = crange[:, 0, SMALL_F:SMALL_F + N_HEADS].T
    c_min = crange[:, 1, SMALL_F:SMALL_F + N_HEADS].T
    cq_max = c_max.reshape(N_HEADS, n_k // n_sub, n_sub).max(axis=-1)
    bound = bmax.reshape(N_HEADS, HEAD_DIM)[:, 0]
    gap = cq_max[:, :, None] - c_min[:, None, :] + 2.0 * bound[:, None, None]
    dead = gap <= -SKIP_LOG
    tile_idx = lax.broadcasted_iota(jnp.int32, dead.shape, 2)
    first = jnp.min(jnp.where(dead, n_k, tile_idx), axis=-1)
    first = jnp.min(first.reshape(N_PAIRS, 2, -1), axis=1)
    first = jnp.minimum(first, jnp.arange(first.shape[1], dtype=jnp.int32) * n_sub)
    return first.reshape(-1).astype(jnp.int32)


def _fox(fast, first, qa, ka, vb, ct, proj, nw_pair, *, tq, tk):
    s = vb.shape[0]
    grid_spec = pltpu.PrefetchScalarGridSpec(
        num_scalar_prefetch=2,
        grid=(N_PAIRS, s // tq),
        in_specs=[
            pl.BlockSpec((1, 2, tq, 2 * LANES), lambda p, i, *_: (p, 0, i, 0)),
            pl.BlockSpec((s, 2 * LANES), lambda p, i, *_: (0, p)),
            pl.BlockSpec((s, LANES), lambda p, i, *_: (0, p)),
            pl.BlockSpec(ct.shape, lambda p, i, *_: (0, 0, 0)),
            pl.BlockSpec((tq, LANES), lambda p, i, *_: (i, COL_FOX_Z // LANES + p)),
            pl.BlockSpec((1, LANES), lambda p, i, *_: (0, 0)),
        ],
        out_specs=pl.BlockSpec((tq, LANES), lambda p, i, *_: (i, p)),
        scratch_shapes=[
            pltpu.VMEM((2, tq, 1), F32),
            pltpu.VMEM((2, tq, 2 * LANES), F32),
        ],
    )
    return pl.pallas_call(
        functools.partial(_fox_kernel, tq=tq, tk=tk),
        grid_spec=grid_spec,
        out_shape=jax.ShapeDtypeStruct((s, QK_W), BF16),
        compiler_params=pltpu.CompilerParams(
            dimension_semantics=("arbitrary", "arbitrary"), vmem_limit_bytes=VMEM_LIMIT),
        name="fox",
    )(fast, first, qa, ka, vb, ct, proj, nw_pair)


def _merge_kernel(x_ref, ya_ref, yb_ref, ga_ref, gb_ref, gbias_ref, pa_ref, pb_ref, wo_ref,
                  fw_ref, o_ref):
    d = x_ref.shape[1]
    ua = _dot(ya_ref[...].astype(BF16), pa_ref[...])
    ub = _dot(yb_ref[...].astype(BF16), pb_ref[...])
    ga = _sigmoid(ga_ref[...] + gbias_ref[:, 0:d])
    gb = _sigmoid(gb_ref[...] + gbias_ref[:, d:2 * d])
    merged = ga * ua + gb * ub
    xn = x_ref[...] + _dot(merged.astype(BF16), wo_ref[...])
    ms = jnp.mean(xn * xn, axis=-1, keepdims=True)
    o_ref[...] = xn * lax.rsqrt(ms + EPS) * fw_ref[...]


def _merge(x, ya, yb, proj, gate_b, w_up_a, w_up_b, w_out, out_w, *, tm=512):
    s, d = x.shape
    const = lambda shape: pl.BlockSpec(shape, lambda i: (0,) * len(shape))
    return pl.pallas_call(
        _merge_kernel,
        grid=(s // tm,),
        in_specs=[
            pl.BlockSpec((tm, d), lambda i: (i, 0)),
            pl.BlockSpec((tm, QK_W), lambda i: (i, 0)),
            pl.BlockSpec((tm, QK_W), lambda i: (i, 0)),
            pl.BlockSpec((tm, d), lambda i: (i, COL_GATE // d)),
            pl.BlockSpec((tm, d), lambda i: (i, COL_GATE // d + 1)),
            const((1, 2 * d)),
            const((QK_W, d)),
            const((QK_W, d)),
            const((d, d)),
            const((1, d)),
        ],
        out_specs=pl.BlockSpec((tm, d), lambda i: (i, 0)),
        out_shape=jax.ShapeDtypeStruct((s, d), F32),
        compiler_params=pltpu.CompilerParams(
            dimension_semantics=("arbitrary",), vmem_limit_bytes=VMEM_LIMIT),
        name="merge",
    )(x, ya, yb, proj, proj, gate_b, w_up_a, w_up_b, w_out, out_w)


def _block_diag_ones(n, blk):
    idx = np.arange(n) // blk
    return jnp.asarray(idx[:, None] == idx[None, :], dtype=BF16)


def _lower_tri_ones(n):
    idx = np.arange(n)
    return jnp.asarray(idx[None, :] <= idx[:, None], dtype=BF16)


def _expand_matrix():
    e = np.zeros((LANES, 2 * QK_W + N_HEADS * LANES), np.float32)
    for h in range(N_HEADS):
        e[SMALL_B + h, h * HEAD_DIM:(h + 1) * HEAD_DIM] = 1.0
        e[SMALL_A + h, QK_W + h * HEAD_DIM:QK_W + (h + 1) * HEAD_DIM] = 1.0
        e[SMALL_A + h, 2 * QK_W + h * LANES:2 * QK_W + (h + 1) * LANES] = 1.0
    return jnp.asarray(e, dtype=BF16)


def _pad_lanes(vec, offset):
    return jnp.zeros((1, LANES), F32).at[0, offset:offset + vec.shape[0]].set(vec.astype(F32))


def _regroup_w_in(w):
    d = w.shape[0]
    o_z = 3 * QK_W
    o_b = o_z + QK_W
    o_a = o_b + N_HEADS
    o_fq = o_a + N_HEADS
    o_ff = o_fq + 4 * QK_W
    o_gate = o_ff + N_HEADS
    small = jnp.concatenate(
        [w[:, o_b:o_a], w[:, o_a:o_fq], w[:, o_ff:o_gate],
         jnp.zeros((d, LANES - 3 * N_HEADS), w.dtype)], axis=1)
    return jnp.concatenate([w[:, 0:o_b], w[:, o_fq:o_ff], w[:, o_gate:], small], axis=1).astype(BF16)


FOX_TQ = 512
FOX_TK = 512


def kernel(x, norm_w, w_in, gate_b, conv_w, a_log, dt_bias, gdn_norm_w, f_bias,
           fox_qn_w, fox_kn_w, fox_on_w, w_up_gdn, w_up_fox, w_out, final_norm_w):
    batch, s, d = x.shape
    assert norm_w.shape[0] == 1, "the merge call fuses the final RMSNorm: single-layer trunk only"
    assert s % FOX_TQ == 0 and s % GDN_CHUNK == 0
    assert w_in.shape[2] == P_WIDTH - LANES + 3 * N_HEADS
    l = 0

    bd = _block_diag_ones(QK_W, HEAD_DIM)
    tri_gdn = _lower_tri_ones(GDN_CHUNK)
    tri_fox = _lower_tri_ones(FOX_TK)
    expand = _expand_matrix()
    tile_heads = lambda w: jnp.tile(w.astype(F32), N_HEADS)[None, :]

    outs = []
    for b in range(batch):
        xb = x[b]
        proj = _proj(xb, norm_w[l][None, :], _regroup_w_in(w_in[l]))
        ya = _gdn(proj, conv_w[l], _pad_lanes(a_log[l], SMALL_A), _pad_lanes(dt_bias[l], SMALL_A),
                  tile_heads(gdn_norm_w[l]), bd, tri_gdn, expand)
        qa, ka, vb, ct, bmax, crange = _foxprep(
            proj, tile_heads(fox_qn_w[l]), tile_heads(fox_kn_w[l]),
            _pad_lanes(f_bias[l], SMALL_F), bd, tri_fox, rows=FOX_TK)
        fast = (jnp.max(bmax) <= FAST_BOUND_MAX).astype(jnp.int32).reshape(1)
        first = _first_key_tile(crange, bmax, tq=FOX_TQ, tk=FOX_TK)
        yb = _fox(fast, first, qa, ka, vb, ct, proj,
                  jnp.tile(fox_on_w[l].astype(F32), 2)[None, :], tq=FOX_TQ, tk=FOX_TK)
        outs.append(_merge(xb, ya, yb, proj, gate_b[l][None, :].astype(F32),
                           w_up_gdn[l].astype(BF16), w_up_fox[l].astype(BF16),
                           w_out[l].astype(BF16), final_norm_w[None, :].astype(F32)))
    return jnp.stack(outs, axis=0)
```

```python
import functools

import numpy as np
import jax
import jax.numpy as jnp
from jax import lax
from jax.experimental import pallas as pl
from jax.experimental.pallas import tpu as pltpu

F32 = jnp.float32
BF16 = jnp.bfloat16

EPS = 1e-6
LANES = 128
HEAD_DIM = 64
N_HEADS = 8
N_PAIRS = N_HEADS // 2
QK_W = N_HEADS * HEAD_DIM
CONV_K = 4
GDN_CHUNK = 128
VMEM_LIMIT = 56 * 1024 * 1024

COL_GDN_QKV = 0
COL_GDN_Z = 1536
COL_FOX_Q = 2048
COL_FOX_K = 2560
COL_FOX_V = 3072
COL_FOX_Z = 3584
COL_GATE = 4096
COL_SMALL = 6144
P_WIDTH = 6272
SMALL_B, SMALL_A, SMALL_F = 0, 8, 16


def _dot(a, b):
    return jnp.dot(a, b, preferred_element_type=F32)


def _dot_nt(a, b):
    return lax.dot_general(a, b, (((1,), (1,)), ((), ())), preferred_element_type=F32)


def _split3(x):
    hi = x.astype(BF16)
    r = x - hi.astype(F32)
    mid = r.astype(BF16)
    lo = (r - mid.astype(F32)).astype(BF16)
    return hi, mid, lo


def _dot_sel_rhs(x, sel):
    hi, mid, lo = _split3(x)
    return _dot(hi, sel) + _dot(mid, sel) + _dot(lo, sel)


def _dot_sel_lhs(sel, x):
    hi, mid, lo = _split3(x)
    return _dot(sel, hi) + _dot(sel, mid) + _dot(sel, lo)


def _head_sumsq(x, bd):
    y = x * x
    hi = y.astype(BF16)
    lo = (y - hi.astype(F32)).astype(BF16)
    return _dot(hi, bd) + _dot(lo, bd)


def _sigmoid(x):
    return 1.0 / (1.0 + jnp.exp(-x))


def _softplus(x):
    return jnp.maximum(x, 0.0) + jnp.log1p(jnp.exp(-jnp.abs(x)))


def _lane_iota(shape):
    return lax.broadcasted_iota(jnp.int32, shape, len(shape) - 1)


def _proj_kernel(x_ref, nw_ref, w_ref, o_ref, *, n_chunk):
    x = x_ref[...]
    ms = jnp.mean(x * x, axis=-1, keepdims=True)
    h = (x * lax.rsqrt(ms + EPS) * nw_ref[...]).astype(BF16)
    width = o_ref.shape[1]
    for c0 in range(0, width, n_chunk):
        c1 = min(c0 + n_chunk, width)
        o_ref[:, c0:c1] = _dot(h, w_ref[:, c0:c1])


def _proj(x, norm_w, w_all, *, tm=256):
    s, d = x.shape
    width = w_all.shape[1]
    return pl.pallas_call(
        functools.partial(_proj_kernel, n_chunk=1024),
        grid=(s // tm,),
        in_specs=[
            pl.BlockSpec((tm, d), lambda i: (i, 0)),
            pl.BlockSpec((1, d), lambda i: (0, 0)),
            pl.BlockSpec((d, width), lambda i: (0, 0)),
        ],
        out_specs=pl.BlockSpec((tm, width), lambda i: (i, 0)),
        out_shape=jax.ShapeDtypeStruct((s, width), F32),
        compiler_params=pltpu.CompilerParams(
            dimension_semantics=("arbitrary",), vmem_limit_bytes=VMEM_LIMIT),
        name="proj",
    )(x, norm_w, w_all)


INV_BASE_LOG2 = 5


def _shr(x, n):
    return lax.shift_right_logical(x, n)


def _inverse_masks(row, col):
    c = row.shape[0]
    lb = INV_BASE_LOG2
    diag = _shr(row, lb) == _shr(col, lb)
    levels = []
    b = lb
    while (1 << b) < c:
        in_pair = _shr(row, b + 1) == _shr(col, b + 1)
        levels.append(in_pair & ((_shr(row, b) & 1) == 1) & ((_shr(col, b) & 1) == 0))
        b += 1
    return diag, levels


def _unit_lower_inverses(ms, eye, masks):
    c = ms[0].shape[0]
    diag, levels = masks
    mds = [jnp.where(diag, m, 0.0) for m in ms]
    ts = [eye - md for md in mds]
    mps = [_dot(md.astype(BF16), md.astype(BF16)) for md in mds]
    n_sq = INV_BASE_LOG2 - 1
    for it in range(n_sq):
        if it < n_sq - 1:
            boths = [_dot(jnp.concatenate([t, mp], axis=0).astype(BF16), mp.astype(BF16))
                     for t, mp in zip(ts, mps)]
            ts = [t + both[0:c, :] for t, both in zip(ts, boths)]
            mps = [both[c:2 * c, :] for both in boths]
        else:
            ts = [t + _dot(t.astype(BF16), mp.astype(BF16)) for t, mp in zip(ts, mps)]
    for level in levels:
        tls = [_dot(t.astype(BF16), jnp.where(level, m, 0.0).astype(BF16)) for t, m in zip(ts, ms)]
        ts = [t - _dot(tl.astype(BF16), t.astype(BF16)) for t, tl in zip(ts, tls)]
    return ts


def _gdn_kernel(qkv_ref, z_ref, sm_ref, cw_ref, alog_ref, dtb_ref, nw_ref,
                bd_ref, tri_ref, ex_ref, o_ref, xext_ref, state_ref):
    c = GDN_CHUNK
    step = pl.program_id(0)

    @pl.when(step == 0)
    def _():
        xext_ref[0:8, :] = jnp.zeros((8, xext_ref.shape[1]), F32)
        state_ref[...] = jnp.zeros(state_ref.shape, F32)

    xext_ref[8:8 + c, :] = qkv_ref[...]
    y = cw_ref[3:4, :] * xext_ref[8:8 + c, :]
    for j in range(CONV_K - 1):
        y = y + cw_ref[j:j + 1, :] * xext_ref[5 + j:5 + j + c, :]
    xext_ref[0:8, :] = qkv_ref[c - 8:c, :]
    y = y * _sigmoid(y)

    bd = bd_ref[...]
    q = y[:, 0:QK_W]
    k = y[:, QK_W:2 * QK_W]
    v = y[:, 2 * QK_W:3 * QK_W]
    q = q * lax.rsqrt(_head_sumsq(q, bd) + EPS) * (HEAD_DIM ** -0.5)
    k = k * lax.rsqrt(_head_sumsq(k, bd) + EPS)

    sm = sm_ref[...]
    beta_s = _sigmoid(sm)
    g_s = -jnp.exp(alog_ref[...]) * _softplus(sm + dtb_ref[...])
    g_cum = _dot_sel_lhs(tri_ref[...], g_s)
    beta = _dot_sel_rhs(beta_s, ex_ref[:, 0:QK_W])
    g = _dot_sel_rhs(g_cum, ex_ref[:, QK_W:2 * QK_W])
    g_bc = _dot_sel_rhs(g_cum, ex_ref[:, 2 * QK_W:2 * QK_W + N_HEADS * LANES])
    g_t = g_cum.T

    eg = jnp.exp(g)
    g_last = g[c - 1:c, :]
    e_last = jnp.exp(g_last)
    kb = k * beta
    vb = v * beta
    kbg = kb * eg
    qg = q * eg
    kt = k * jnp.exp(g_last - g)

    row = lax.broadcasted_iota(jnp.int32, (c, c), 0)
    col = lax.broadcasted_iota(jnp.int32, (c, c), 1)
    causal = col <= row
    strict = col < row
    eye = (col == row).astype(F32)
    inv_masks = _inverse_masks(row, col)
    lane = _lane_iota((c, LANES))
    is_even = lane < HEAD_DIM
    bd_mask = (lane < HEAD_DIM) == (lax.broadcasted_iota(jnp.int32, (c, LANES), 0) < HEAD_DIM)

    pair_lanes = [slice(p * LANES, (p + 1) * LANES) for p in range(N_PAIRS)]
    odd = jnp.logical_not(is_even)

    kq = []
    for ls in pair_lanes:
        lhs = jnp.concatenate([jnp.where(is_even, kb[:, ls], 0.0), jnp.where(odd, kb[:, ls], 0.0),
                               jnp.where(is_even, q[:, ls], 0.0), jnp.where(odd, q[:, ls], 0.0)],
                              axis=0)
        kq.append(_dot_nt(lhs.astype(BF16), k[:, ls].astype(BF16)))
    ms, a_h = [], []
    for h in range(N_HEADS):
        p, hh = divmod(h, 2)
        diff = g_bc[:, h * LANES:(h + 1) * LANES] - g_t[SMALL_A + h:SMALL_A + h + 1, :]
        dec = jnp.where(causal, jnp.exp(jnp.where(causal, diff, 0.0)), 0.0)
        ms.append(jnp.where(strict, kq[p][hh * c:(hh + 1) * c, :] * dec, 0.0))
        a_h.append((kq[p][(2 + hh) * c:(3 + hh) * c, :] * dec).astype(BF16))
    ts = _unit_lower_inverses(ms, eye, inv_masks)
    rhs_wk = [jnp.concatenate([vb[:, ls], kbg[:, ls]], axis=1).astype(BF16) for ls in pair_lanes]
    wk = [_dot(ts[h].astype(BF16), rhs_wk[h // 2]) for h in range(N_HEADS)]

    sts = [state_ref[p] for p in range(N_PAIRS)]
    stb = [st.astype(BF16) for st in sts]
    ub = []
    for p in range(N_PAIRS):
        w_pair = jnp.where(is_even, wk[2 * p][:, 0:LANES], wk[2 * p + 1][:, 0:LANES])
        kc_pair = jnp.where(is_even, wk[2 * p][:, LANES:2 * LANES], wk[2 * p + 1][:, LANES:2 * LANES])
        ub.append((w_pair - _dot(kc_pair.astype(BF16), stb[p])).astype(BF16))
    for p, ls in enumerate(pair_lanes):
        o = _dot(qg[:, ls].astype(BF16), stb[p]) + jnp.where(
            is_even, _dot(a_h[2 * p], ub[p]), _dot(a_h[2 * p + 1], ub[p]))
        upd = _dot(kt[:, ls].T.astype(BF16), ub[p])
        state_ref[p] = sts[p] * e_last[:, ls] + jnp.where(bd_mask, upd, 0.0)
        o_ref[:, ls] = o.astype(o_ref.dtype)

    o_all = o_ref[...].astype(F32)
    z = z_ref[...]
    ms = _head_sumsq(o_all, bd) * (1.0 / HEAD_DIM)
    yn = o_all * lax.rsqrt(ms + EPS) * nw_ref[...]
    o_ref[...] = (yn * (z * _sigmoid(z))).astype(o_ref.dtype)


def _gdn(proj, conv_w, a_log_pad, dtb_pad, nw_tiled, bd, tri, expand):
    s = proj.shape[0]
    c = GDN_CHUNK
    conv_width = 3 * QK_W
    const = lambda shape: pl.BlockSpec(shape, lambda i: (0,) * len(shape))
    return pl.pallas_call(
        _gdn_kernel,
        grid=(s // c,),
        in_specs=[
            pl.BlockSpec((c, conv_width), lambda i: (i, COL_GDN_QKV // conv_width)),
            pl.BlockSpec((c, QK_W), lambda i: (i, COL_GDN_Z // QK_W)),
            pl.BlockSpec((c, LANES), lambda i: (i, COL_SMALL // LANES)),
            const((CONV_K, conv_width)),
            const((1, LANES)),
            const((1, LANES)),
            const((1, QK_W)),
            const((QK_W, QK_W)),
            const((c, c)),
            const(expand.shape),
        ],
        out_specs=pl.BlockSpec((c, QK_W), lambda i: (i, 0)),
        out_shape=jax.ShapeDtypeStruct((s, QK_W), F32),
        scratch_shapes=[
            pltpu.VMEM((c + 8, conv_width), F32),
            pltpu.VMEM((N_PAIRS, LANES, LANES), F32),
        ],
        compiler_params=pltpu.CompilerParams(
            dimension_semantics=("arbitrary",), vmem_limit_bytes=VMEM_LIMIT),
        name="gdn",
    )(proj, proj, proj, conv_w, a_log_pad, dtb_pad, nw_tiled, bd, tri, expand)


X_ROW, X_EVEN, X_ODD = 0, 3, 6


def _lane_terms(lane, base, terms, otherwise):
    out = otherwise
    for t, col in enumerate(terms):
        out = jnp.where(lane == base + t, col, out)
    return out


def _split3_f32(x):
    hi, mid, lo = _split3(x)
    return hi.astype(F32), mid.astype(F32), lo.astype(F32)


def _foxprep_kernel(q_ref, k_ref, v_ref, sm_ref, qw_ref, kw_ref, fb_ref, bd_ref, tri_ref,
                    qa_ref, ka_ref, vo_ref, ct_ref, bmax_ref, crange_ref, carry_ref, kmax_ref):
    @pl.when(pl.program_id(0) == 0)
    def _():
        carry_ref[...] = jnp.zeros(carry_ref.shape, F32)
        kmax_ref[...] = jnp.zeros(kmax_ref.shape, F32)
        bmax_ref[...] = jnp.zeros(bmax_ref.shape, F32)

    bd = bd_ref[...]

    def headnorm(x, w):
        ms = _head_sumsq(x, bd) * (1.0 / HEAD_DIM)
        return x * lax.rsqrt(ms + EPS) * w

    qb = (headnorm(q_ref[...], qw_ref[...]) * (HEAD_DIM ** -0.5)).astype(BF16)
    kb = headnorm(k_ref[...], kw_ref[...]).astype(BF16)
    vo_ref[...] = v_ref[...].astype(vo_ref.dtype)
    rows = qb.shape[0]

    xf = sm_ref[...] + fb_ref[...]
    log_f = jnp.minimum(xf, 0.0) - jnp.log1p(jnp.exp(-jnp.abs(xf)))
    cum = carry_ref[...] + _dot_sel_lhs(tri_ref[...], log_f)
    carry_ref[...] = cum[rows - 1:rows, :]
    ct_ref[0] = cum.T[SMALL_F:SMALL_F + N_HEADS, :]
    crange_ref[0, 0:1, :] = jnp.max(cum, axis=0, keepdims=True)
    crange_ref[0, 1:2, :] = jnp.min(cum, axis=0, keepdims=True)

    qf = qb.astype(F32)
    kf = kb.astype(F32)
    kmax = jnp.maximum(kmax_ref[...], jnp.max(_head_sumsq(kf, bd), axis=0, keepdims=True))
    kmax_ref[...] = kmax
    bound = jnp.sqrt(_head_sumsq(qf, bd) * kmax)
    bmax_ref[...] = jnp.maximum(bmax_ref[...], jnp.max(bound, axis=0, keepdims=True))

    lane = _lane_iota((rows, LANES))
    is_even = lane < HEAD_DIM
    ones3 =lambda base: ((lane >= base) & (lane < base + 3)).astype(F32)
    for p in range(N_PAIRS):
        ls = slice(p * LANES, (p + 1) * LANES)
        neg_c = []
        for hh in range(2):
            h = 2 * p + hh
            c_col = cum[:, SMALL_F + h:SMALL_F + h + 1]
            b_col = bound[:, h * HEAD_DIM:h * HEAD_DIM + 1]
            neg_c.append(_split3_f32(-c_col))
            qx = _lane_terms(lane, X_ROW, _split3_f32(c_col - b_col),
                             ones3(X_EVEN if hh == 0 else X_ODD))
            keep = is_even if hh == 0 else jnp.logical_not(is_even)
            qa_ref[p, hh, :, 0:LANES] = jnp.where(keep, qb[:, ls], jnp.zeros_like(qb[:, ls]))
            qa_ref[p, hh, :, LANES:2 * LANES] = qx.astype(qa_ref.dtype)
        kx = _lane_terms(lane, X_EVEN, neg_c[0], ones3(X_ROW))
        kx = _lane_terms(lane, X_ODD, neg_c[1], kx)
        ka_ref[:, 2 * p * LANES:(2 * p + 1) * LANES] = kb[:, ls]
        ka_ref[:, (2 * p + 1) * LANES:(2 * p + 2) * LANES] = kx.astype(ka_ref.dtype)


def _foxprep(proj, qw_tiled, kw_tiled, fb_pad, bd, tri, *, rows):
    s = proj.shape[0]
    const = lambda shape: pl.BlockSpec(shape, lambda i: (0,) * len(shape))
    return pl.pallas_call(
        _foxprep_kernel,
        grid=(s // rows,),
        in_specs=[
            pl.BlockSpec((rows, QK_W), lambda i: (i, COL_FOX_Q // QK_W)),
            pl.BlockSpec((rows, QK_W), lambda i: (i, COL_FOX_K // QK_W)),
            pl.BlockSpec((rows, QK_W), lambda i: (i, COL_FOX_V // QK_W)),
            pl.BlockSpec((rows, LANES), lambda i: (i, COL_SMALL // LANES)),
            const((1, QK_W)),
            const((1, QK_W)),
            const((1, LANES)),
            const((QK_W, QK_W)),
            const((rows, rows)),
        ],
        out_specs=[
            pl.BlockSpec((N_PAIRS, 2, rows, 2 * LANES), lambda i: (0, 0, i, 0)),
            pl.BlockSpec((rows, 2 * QK_W), lambda i: (i, 0)),
            pl.BlockSpec((rows, QK_W), lambda i: (i, 0)),
            pl.BlockSpec((1, N_HEADS, rows), lambda i: (i, 0, 0)),
            pl.BlockSpec((1, QK_W), lambda i: (0, 0)),
            pl.BlockSpec((1, 2, LANES), lambda i: (i, 0, 0)),
        ],
        out_shape=[
            jax.ShapeDtypeStruct((N_PAIRS, 2, s, 2 * LANES), BF16),
            jax.ShapeDtypeStruct((s, 2 * QK_W), BF16),
            jax.ShapeDtypeStruct((s, QK_W), BF16),
            jax.ShapeDtypeStruct((s // rows, N_HEADS, rows), F32),
            jax.ShapeDtypeStruct((1, QK_W), F32),
            jax.ShapeDtypeStruct((s // rows, 2, LANES), F32),
        ],
        scratch_shapes=[pltpu.VMEM((1, LANES), F32), pltpu.VMEM((1, QK_W), F32)],
        compiler_params=pltpu.CompilerParams(
            dimension_semantics=("arbitrary",), vmem_limit_bytes=VMEM_LIMIT),
        name="foxprep",
    )(proj, proj, proj, proj, qw_tiled, kw_tiled, fb_pad, bd, tri)


FAST_BOUND_MAX = 40.0


def _fox_kernel(fast_ref, first_ref, q_ref, k_ref, v_ref, c_ref, z_ref, nw_ref, o_ref, m_ref,
                acc_ref, *, tq, tk):
    p = pl.program_id(0)
    i = pl.program_id(1)
    first = first_ref[p * pl.num_programs(1) + i]
    n_sub = tq // tk

    acc_ref[...] = jnp.zeros(acc_ref.shape, F32)
    ones = jnp.ones((tk, LANES), BF16)

    def kv_tile(j):
        start = pl.multiple_of(j * tk, tk)
        vj = jnp.concatenate([v_ref[pl.ds(start, tk), :], ones], axis=1)
        return k_ref[pl.ds(start, tk), :], vj

    def causal_keep(j):
        q_pos = i * tq + lax.broadcasted_iota(jnp.int32, (tq, tk), 0)
        k_pos = j * tk + lax.broadcasted_iota(jnp.int32, (tq, tk), 1)
        return k_pos <= q_pos

    def fast_tile(j, masked):
        kj, vj = kv_tile(j)
        for hh in range(2):
            s = _dot_nt(q_ref[0, hh], kj)
            if masked:
                s = jnp.where(causal_keep(j), s, -jnp.inf)
            acc_ref[hh] += _dot(jnp.exp(s).astype(BF16), vj)

    def slow_tile(j, masked):
        kj, vj = kv_tile(j)
        for hh in range(2):
            ck = c_ref[j, pl.ds(2 * p + hh, 1), :]
            s = _dot_nt(q_ref[0, hh, :, 0:LANES], kj[:, 0:LANES]) - ck
            if masked:
                s = jnp.where(causal_keep(j), s, -jnp.inf)
            m_prev = m_ref[hh]
            m_new = jnp.maximum(m_prev, jnp.max(s, axis=-1, keepdims=True))
            alpha = jnp.exp(m_prev - m_new)
            pr = jnp.exp(s - m_new)
            acc_ref[hh] = alpha * acc_ref[hh] + _dot(pr.astype(BF16), vj)
            m_ref[hh] = m_new

    def run(tile):
        def body(j, carry):
            tile(j, False)
            return carry
        lax.fori_loop(first, i * n_sub, body, 0)
        for d in range(n_sub):
            tile(i * n_sub + d, True)

    @pl.when(fast_ref[0] == 1)
    def _():
        run(fast_tile)

    @pl.when(fast_ref[0] != 1)
    def _():
        m_ref[...] = jnp.full(m_ref.shape, -jnp.inf, F32)
        run(slow_tile)

    acc_e = acc_ref[0]
    acc_o = acc_ref[1]
    is_even = _lane_iota((tq, LANES)) < HEAD_DIM
    o = jnp.where(is_even, acc_e[:, 0:LANES] / acc_e[:, LANES:2 * LANES],
                  acc_o[:, 0:LANES] / acc_o[:, LANES:2 * LANES])
    o2 = o * o
    ss_e = jnp.sum(jnp.where(is_even, o2, 0.0), axis=-1, keepdims=True)
    ss_o = jnp.sum(jnp.where(is_even, 0.0, o2), axis=-1, keepdims=True)
    ms = jnp.where(is_even, ss_e, ss_o) * (1.0 / HEAD_DIM)
    z = z_ref[...]
    o_ref[...] = (o * lax.rsqrt(ms + EPS) * nw_ref[...] * (z * _sigmoid(z))).astype(o_ref.dtype)


SKIP_LOG = 110.0


def _first_key_tile(crange, bmax, *, tq, tk):
    n_k = crange.shape[0]
    n_sub = tq // tk
    c_max = crange[:, 0, SMALL_F:SMALL_F + N_HEADS].T
    c_min = crange[:, 1, SMALL_F:SMALL_F + N_HEADS].T
    cq_max = c_max.reshape(N_HEADS, n_k // n_sub, n_sub).max(axis=-1)
    bound = bmax.reshape(N_HEADS, HEAD_DIM)[:, 0]
    gap = cq_max[:, :, None] - c_min[:, None, :] + 2.0 * bound[:, None, None]
    dead = gap <= -SKIP_LOG
    tile_idx = lax.broadcasted_iota(jnp.int32, dead.shape, 2)
    first = jnp.min(jnp.where(dead, n_k, tile_idx), axis=-1)
    first = jnp.min(first.reshape(N_PAIRS, 2, -1), axis=1)
    first = jnp.minimum(first, jnp.arange(first.shape[1], dtype=jnp.int32) * n_sub)
    return first.reshape(-1).astype(jnp.int32)


def _fox(fast, first, qa, ka, vb, ct, proj, nw_pair, *, tq, tk):
    s = vb.shape[0]
    grid_spec = pltpu.PrefetchScalarGridSpec(
        num_scalar_prefetch=2,
        grid=(N_PAIRS, s // tq),
        in_specs=[
            pl.BlockSpec((1, 2, tq, 2 * LANES), lambda p, i, *_: (p, 0, i, 0)),
            pl.BlockSpec((s, 2 * LANES), lambda p, i, *_: (0, p)),
            pl.BlockSpec((s, LANES), lambda p, i, *_: (0, p)),
            pl.BlockSpec(ct.shape, lambda p, i, *_: (0, 0, 0)),
            pl.BlockSpec((tq, LANES), lambda p, i, *_: (i, COL_FOX_Z // LANES + p)),
            pl.BlockSpec((1, LANES), lambda p, i, *_: (0, 0)),
        ],
        out_specs=pl.BlockSpec((tq, LANES), lambda p, i, *_: (i, p)),
        scratch_shapes=[
            pltpu.VMEM((2, tq, 1), F32),
            pltpu.VMEM((2, tq, 2 * LANES), F32),
        ],
    )
    return pl.pallas_call(
        functools.partial(_fox_kernel, tq=tq, tk=tk),
        grid_spec=grid_spec,
        out_shape=jax.ShapeDtypeStruct((s, QK_W), BF16),
        compiler_params=pltpu.CompilerParams(
            dimension_semantics=("arbitrary", "arbitrary"), vmem_limit_bytes=VMEM_LIMIT),
        name="fox",
    )(fast, first, qa, ka, vb, ct, proj, nw_pair)


def _merge_kernel(x_ref, ya_ref, yb_ref, ga_ref, gb_ref, gbias_ref, pa_ref, pb_ref, wo_ref,
                  fw_ref, o_ref):
    d = x_ref.shape[1]
    ua = _dot(ya_ref[...].astype(BF16), pa_ref[...])
    ub = _dot(yb_ref[...].astype(BF16), pb_ref[...])
    ga = _sigmoid(ga_ref[...] + gbias_ref[:, 0:d])
    gb = _sigmoid(gb_ref[...] + gbias_ref[:, d:2 * d])
    merged = ga * ua + gb * ub
    xn = x_ref[...] + _dot(merged.astype(BF16), wo_ref[...])
    ms = jnp.mean(xn * xn, axis=-1, keepdims=True)
    o_ref[...] = xn * lax.rsqrt(ms + EPS) * fw_ref[...]


def _merge(x, ya, yb, proj, gate_b, w_up_a, w_up_b, w_out, out_w, *, tm=512):
    s, d = x.shape
    const = lambda shape: pl.BlockSpec(shape, lambda i: (0,) * len(shape))
    return pl.pallas_call(
        _merge_kernel,
        grid=(s // tm,),
        in_specs=[
            pl.BlockSpec((tm, d), lambda i: (i, 0)),
            pl.BlockSpec((tm, QK_W), lambda i: (i, 0)),
            pl.BlockSpec((tm, QK_W), lambda i: (i, 0)),
            pl.BlockSpec((tm, d), lambda i: (i, COL_GATE // d)),
            pl.BlockSpec((tm, d), lambda i: (i, COL_GATE // d + 1)),
            const((1, 2 * d)),
            const((QK_W, d)),
            const((QK_W, d)),
            const((d, d)),
            const((1, d)),
        ],
        out_specs=pl.BlockSpec((tm, d), lambda i: (i, 0)),
        out_shape=jax.ShapeDtypeStruct((s, d), F32),
        compiler_params=pltpu.CompilerParams(
            dimension_semantics=("arbitrary",), vmem_limit_bytes=VMEM_LIMIT),
        name="merge",
    )(x, ya, yb, proj, proj, gate_b, w_up_a, w_up_b, w_out, out_w)


def _block_diag_ones(n, blk):
    idx = np.arange(n) // blk
    return jnp.asarray(idx[:, None] == idx[None, :], dtype=BF16)


def _lower_tri_ones(n):
    idx = np.arange(n)
    return jnp.asarray(idx[None, :] <= idx[:, None], dtype=BF16)


def _expand_matrix():
    e = np.zeros((LANES, 2 * QK_W + N_HEADS * LANES), np.float32)
    for h in range(N_HEADS):
        e[SMALL_B + h, h * HEAD_DIM:(h + 1) * HEAD_DIM] = 1.0
        e[SMALL_A + h, QK_W + h * HEAD_DIM:QK_W + (h + 1) * HEAD_DIM] = 1.0
        e[SMALL_A + h, 2 * QK_W + h * LANES:2 * QK_W + (h + 1) * LANES] = 1.0
    return jnp.asarray(e, dtype=BF16)


def _pad_lanes(vec, offset):
    return jnp.zeros((1, LANES), F32).at[0, offset:offset + vec.shape[0]].set(vec.astype(F32))


def _regroup_w_in(w):
    d = w.shape[0]
    o_z = 3 * QK_W
    o_b = o_z + QK_W
    o_a = o_b + N_HEADS
    o_fq = o_a + N_HEADS
    o_ff = o_fq + 4 * QK_W
    o_gate = o_ff + N_HEADS
    small = jnp.concatenate(
        [w[:, o_b:o_a], w[:, o_a:o_fq], w[:, o_ff:o_gate],
         jnp.zeros((d, LANES - 3 * N_HEADS), w.dtype)], axis=1)
    return jnp.concatenate([w[:, 0:o_b], w[:, o_fq:o_ff], w[:, o_gate:], small], axis=1).astype(BF16)


FOX_TQ = 512
FOX_TK = 512


def kernel(x, norm_w, w_in, gate_b, conv_w, a_log, dt_bias, gdn_norm_w, f_bias,
           fox_qn_w, fox_kn_w, fox_on_w, w_up_gdn, w_up_fox, w_out, final_norm_w):
    batch, s, d = x.shape
    assert norm_w.shape[0] == 1, "the merge call fuses the final RMSNorm: single-layer trunk only"
    assert s % FOX_TQ == 0 and s % GDN_CHUNK == 0
    assert w_in.shape[2] == P_WIDTH - LANES + 3 * N_HEADS
    l = 0

    bd = _block_diag_ones(QK_W, HEAD_DIM)
    tri_gdn = _lower_tri_ones(GDN_CHUNK)
    tri_fox = _lower_tri_ones(FOX_TK)
    expand = _expand_matrix()
    tile_heads = lambda w: jnp.tile(w.astype(F32), N_HEADS)[None, :]

    outs = []
    x_rows = x.reshape(batch * s, d)
    for b in range(batch):
        xb = x_rows if batch == 1 else x_rows[b * s:(b + 1) * s]
        proj = _proj(xb, norm_w[l][None, :], _regroup_w_in(w_in[l]))
        ya = _gdn(proj, conv_w[l], _pad_lanes(a_log[l], SMALL_A), _pad_lanes(dt_bias[l], SMALL_A),
                  tile_heads(gdn_norm_w[l]), bd, tri_gdn, expand)
        qa, ka, vb, ct, bmax, crange = _foxprep(
            proj, tile_heads(fox_qn_w[l]), tile_heads(fox_kn_w[l]),
            _pad_lanes(f_bias[l], SMALL_F), bd, tri_fox, rows=FOX_TK)
        fast = (jnp.max(bmax) <= FAST_BOUND_MAX).astype(jnp.int32).reshape(1)
        first = _first_key_tile(crange, bmax, tq=FOX_TQ, tk=FOX_TK)
        yb = _fox(fast, first, qa, ka, vb, ct, proj,
                  jnp.tile(fox_on_w[l].astype(F32), 2)[None, :], tq=FOX_TQ, tk=FOX_TK)
        outs.append(_merge(xb, ya, yb, proj, gate_b[l][None, :].astype(F32),
                           w_up_gdn[l].astype(BF16), w_up_fox[l].astype(BF16),
                           w_out[l].astype(BF16), final_norm_w[None, :].astype(F32)))
    out = outs[0] if batch == 1 else jnp.concatenate(outs, axis=0)
    return out.reshape(batch, s, d)
```

```python
import functools

import numpy as np
import jax
import jax.numpy as jnp
from jax import lax
from jax.experimental import pallas as pl
from jax.experimental.pallas import tpu as pltpu

F32 = jnp.float32
BF16 = jnp.bfloat16

EPS = 1e-6
LANES = 128
HEAD_DIM = 64
N_HEADS = 8
N_PAIRS = N_HEADS // 2
QK_W = N_HEADS * HEAD_DIM
CONV_K = 4
GDN_CHUNK = 128
VMEM_LIMIT = 56 * 1024 * 1024

COL_GDN_QKV = 0
COL_GDN_Z = 1536
COL_FOX_Q = 2048
COL_FOX_K = 2560
COL_FOX_V = 3072
COL_FOX_Z = 3584
COL_GATE = 4096
COL_SMALL = 6144
P_WIDTH = 6272
SMALL_B, SMALL_A, SMALL_F = 0, 8, 16


def _dot(a, b):
    return jnp.dot(a, b, preferred_element_type=F32)


def _dot_nt(a, b):
    return lax.dot_general(a, b, (((1,), (1,)), ((), ())), preferred_element_type=F32)


def _split3(x):
    hi = x.astype(BF16)
    r = x - hi.astype(F32)
    mid = r.astype(BF16)
    lo = (r - mid.astype(F32)).astype(BF16)
    return hi, mid, lo


def _dot_sel_rhs(x, sel):
    hi, mid, lo = _split3(x)
    return _dot(hi, sel) + _dot(mid, sel) + _dot(lo, sel)


def _dot_sel_lhs(sel, x):
    hi, mid, lo = _split3(x)
    return _dot(sel, hi) + _dot(sel, mid) + _dot(sel, lo)


def _head_sumsq(x, bd):
    return _dot((x * x).astype(BF16), bd)


def _sigmoid(x):
    return 1.0 / (1.0 + jnp.exp(-x))


def _softplus(x):
    return jnp.maximum(x, 0.0) + jnp.log1p(jnp.exp(-jnp.abs(x)))


def _lane_iota(shape):
    return lax.broadcasted_iota(jnp.int32, shape, len(shape) - 1)


def _proj_kernel(x_ref, nw_ref, w_ref, o_ref, *, n_chunk):
    x = x_ref[...]
    ms = jnp.mean(x * x, axis=-1, keepdims=True)
    h = (x * lax.rsqrt(ms + EPS) * nw_ref[...]).astype(BF16)
    width = o_ref.shape[1]
    for c0 in range(0, width, n_chunk):
        c1 = min(c0 + n_chunk, width)
        o_ref[:, c0:c1] = _dot(h, w_ref[:, c0:c1])


def _proj(x, norm_w, w_all, *, tm=256):
    s, d = x.shape
    width = w_all.shape[1]
    return pl.pallas_call(
        functools.partial(_proj_kernel, n_chunk=1024),
        grid=(s // tm,),
        in_specs=[
            pl.BlockSpec((tm, d), lambda i: (i, 0)),
            pl.BlockSpec((1, d), lambda i: (0, 0)),
            pl.BlockSpec((d, width), lambda i: (0, 0)),
        ],
        out_specs=pl.BlockSpec((tm, width), lambda i: (i, 0)),
        out_shape=jax.ShapeDtypeStruct((s, width), F32),
        compiler_params=pltpu.CompilerParams(
            dimension_semantics=("arbitrary",), vmem_limit_bytes=VMEM_LIMIT),
        name="proj",
    )(x, norm_w, w_all)


INV_BASE_LOG2 = 5


def _shr(x, n):
    return lax.shift_right_logical(x, n)


def _inverse_masks(row, col):
    c = row.shape[0]
    lb = INV_BASE_LOG2
    diag = _shr(row, lb) == _shr(col, lb)
    levels = []
    b = lb
    while (1 << b) < c:
        in_pair = _shr(row, b + 1) == _shr(col, b + 1)
        levels.append(in_pair & ((_shr(row, b) & 1) == 1) & ((_shr(col, b) & 1) == 0))
        b += 1
    return diag, levels


def _unit_lower_inverses(ms, eye, masks):
    c = ms[0].shape[0]
    diag, levels = masks
    mds = [jnp.where(diag, m, 0.0) for m in ms]
    ts = [eye - md for md in mds]
    mps = [_dot(md.astype(BF16), md.astype(BF16)) for md in mds]
    n_sq = INV_BASE_LOG2 - 1
    for it in range(n_sq):
        if it < n_sq - 1:
            boths = [_dot(jnp.concatenate([t, mp], axis=0).astype(BF16), mp.astype(BF16))
                     for t, mp in zip(ts, mps)]
            ts = [t + both[0:c, :] for t, both in zip(ts, boths)]
            mps = [both[c:2 * c, :] for both in boths]
        else:
            ts = [t + _dot(t.astype(BF16), mp.astype(BF16)) for t, mp in zip(ts, mps)]
    for level in levels:
        tls = [_dot(t.astype(BF16), jnp.where(level, m, 0.0).astype(BF16)) for t, m in zip(ts, ms)]
        ts = [t - _dot(tl.astype(BF16), t.astype(BF16)) for t, tl in zip(ts, tls)]
    return ts


def _gdn_kernel(qkv_ref, z_ref, sm_ref, cw_ref, alog_ref, dtb_ref, nw_ref,
                bd_ref, tri_ref, ex_ref, o_ref, xext_ref, state_ref):
    c = GDN_CHUNK
    step = pl.program_id(0)

    @pl.when(step == 0)
    def _():
        xext_ref[0:8, :] = jnp.zeros((8, xext_ref.shape[1]), F32)
        state_ref[...] = jnp.zeros(state_ref.shape, F32)

    xext_ref[8:8 + c, :] = qkv_ref[...]
    y = cw_ref[3:4, :] * xext_ref[8:8 + c, :]
    for j in range(CONV_K - 1):
        y = y + cw_ref[j:j + 1, :] * xext_ref[5 + j:5 + j + c, :]
    xext_ref[0:8, :] = qkv_ref[c - 8:c, :]
    y = y * _sigmoid(y)

    bd = bd_ref[...]
    q = y[:, 0:QK_W]
    k = y[:, QK_W:2 * QK_W]
    v = y[:, 2 * QK_W:3 * QK_W]
    q = q * lax.rsqrt(_head_sumsq(q, bd) + EPS) * (HEAD_DIM ** -0.5)
    k = k * lax.rsqrt(_head_sumsq(k, bd) + EPS)

    sm = sm_ref[...]
    beta_s = _sigmoid(sm)
    g_s = -jnp.exp(alog_ref[...]) * _softplus(sm + dtb_ref[...])
    g_cum = _dot_sel_lhs(tri_ref[...], g_s)
    beta = _dot(beta_s.astype(BF16), ex_ref[:, 0:QK_W])
    g = _dot_sel_rhs(g_cum, ex_ref[:, QK_W:2 * QK_W])
    g_bc = _dot_sel_rhs(g_cum, ex_ref[:, 2 * QK_W:2 * QK_W + N_HEADS * LANES])
    g_t = g_cum.T

    eg = jnp.exp(g)
    g_last = g[c - 1:c, :]
    e_last = jnp.exp(g_last)
    kb = k * beta
    vb = v * beta
    kbg = kb * eg
    qg = q * eg
    kt = k * jnp.exp(g_last - g)

    row = lax.broadcasted_iota(jnp.int32, (c, c), 0)
    col = lax.broadcasted_iota(jnp.int32, (c, c), 1)
    causal = col <= row
    strict = col < row
    eye = (col == row).astype(F32)
    inv_masks = _inverse_masks(row, col)
    lane = _lane_iota((c, LANES))
    is_even = lane < HEAD_DIM
    bd_mask = (lane < HEAD_DIM) == (lax.broadcasted_iota(jnp.int32, (c, LANES), 0) < HEAD_DIM)

    pair_lanes = [slice(p * LANES, (p + 1) * LANES) for p in range(N_PAIRS)]
    odd = jnp.logical_not(is_even)

    kq = []
    for ls in pair_lanes:
        lhs = jnp.concatenate([jnp.where(is_even, kb[:, ls], 0.0), jnp.where(odd, kb[:, ls], 0.0),
                               jnp.where(is_even, q[:, ls], 0.0), jnp.where(odd, q[:, ls], 0.0)],
                              axis=0)
        kq.append(_dot_nt(lhs.astype(BF16), k[:, ls].astype(BF16)))
    ms, a_h = [], []
    for h in range(N_HEADS):
        p, hh = divmod(h, 2)
        diff = g_bc[:, h * LANES:(h + 1) * LANES] - g_t[SMALL_A + h:SMALL_A + h + 1, :]
        dec = jnp.where(causal, jnp.exp(jnp.where(causal, diff, 0.0)), 0.0)
        ms.append(jnp.where(strict, kq[p][hh * c:(hh + 1) * c, :] * dec, 0.0))
        a_h.append((kq[p][(2 + hh) * c:(3 + hh) * c, :] * dec).astype(BF16))
    ts = _unit_lower_inverses(ms, eye, inv_masks)
    rhs_wk = [jnp.concatenate([vb[:, ls], kbg[:, ls]], axis=1).astype(BF16) for ls in pair_lanes]
    wk = [_dot(ts[h].astype(BF16), rhs_wk[h // 2]) for h in range(N_HEADS)]

    sts = [state_ref[p] for p in range(N_PAIRS)]
    stb = [st.astype(BF16) for st in sts]
    ub = []
    for p in range(N_PAIRS):
        w_pair = jnp.where(is_even, wk[2 * p][:, 0:LANES], wk[2 * p + 1][:, 0:LANES])
        kc_pair = jnp.where(is_even, wk[2 * p][:, LANES:2 * LANES], wk[2 * p + 1][:, LANES:2 * LANES])
        ub.append((w_pair - _dot(kc_pair.astype(BF16), stb[p])).astype(BF16))
    for p, ls in enumerate(pair_lanes):
        o = _dot(qg[:, ls].astype(BF16), stb[p]) + jnp.where(
            is_even, _dot(a_h[2 * p], ub[p]), _dot(a_h[2 * p + 1], ub[p]))
        upd = _dot(kt[:, ls].T.astype(BF16), ub[p])
        state_ref[p] = sts[p] * e_last[:, ls] + jnp.where(bd_mask, upd, 0.0)
        o_ref[:, ls] = o.astype(o_ref.dtype)

    o_all = o_ref[...].astype(F32)
    z = z_ref[...]
    ms = _head_sumsq(o_all, bd) * (1.0 / HEAD_DIM)
    yn = o_all * lax.rsqrt(ms + EPS) * nw_ref[...]
    o_ref[...] = (yn * (z * _sigmoid(z))).astype(o_ref.dtype)


def _gdn(proj, conv_w, a_log_pad, dtb_pad, nw_tiled, bd, tri, expand):
    s = proj.shape[0]
    c = GDN_CHUNK
    conv_width = 3 * QK_W
    const = lambda shape: pl.BlockSpec(shape, lambda i: (0,) * len(shape))
    return pl.pallas_call(
        _gdn_kernel,
        grid=(s // c,),
        in_specs=[
            pl.BlockSpec((c, conv_width), lambda i: (i, COL_GDN_QKV // conv_width)),
            pl.BlockSpec((c, QK_W), lambda i: (i, COL_GDN_Z // QK_W)),
            pl.BlockSpec((c, LANES), lambda i: (i, COL_SMALL // LANES)),
            const((CONV_K, conv_width)),
            const((1, LANES)),
            const((1, LANES)),
            const((1, QK_W)),
            const((QK_W, QK_W)),
            const((c, c)),
            const(expand.shape),
        ],
        out_specs=pl.BlockSpec((c, QK_W), lambda i: (i, 0)),
        out_shape=jax.ShapeDtypeStruct((s, QK_W), F32),
        scratch_shapes=[
            pltpu.VMEM((c + 8, conv_width), F32),
            pltpu.VMEM((N_PAIRS, LANES, LANES), F32),
        ],
        compiler_params=pltpu.CompilerParams(
            dimension_semantics=("arbitrary",), vmem_limit_bytes=VMEM_LIMIT),
        name="gdn",
    )(proj, proj, proj, conv_w, a_log_pad, dtb_pad, nw_tiled, bd, tri, expand)


X_C, X_EVEN, X_ODD, X_BOUND = 0, 3, 6, 9
SMALL_BOUND = 24
N_EXTRA_COLS = (N_HEADS + N_PAIRS) * LANES


def _extras_selector():
    sel = np.zeros((3 * LANES, N_EXTRA_COLS), np.float32)
    pat = np.zeros((1, N_EXTRA_COLS), np.float32)
    k0 = N_HEADS * LANES
    for t in range(3):
        r0 = t * LANES
        for h in range(N_HEADS):
            sel[r0 + SMALL_F + h, h * LANES + X_C + t] = 1.0
            sel[r0 + SMALL_BOUND + h, h * LANES + X_BOUND + t] = 1.0
            pat[0, h * LANES + (X_EVEN if h % 2 == 0 else X_ODD) + t] = 1.0
        for p in range(N_PAIRS):
            sel[r0 + SMALL_F + 2 * p, k0 + p * LANES + X_EVEN + t] = -1.0
            sel[r0 + SMALL_F + 2 * p + 1, k0 + p * LANES + X_ODD + t] = -1.0
            pat[0, k0 + p * LANES + X_C + t] = 1.0
            pat[0, k0 + p * LANES + X_BOUND + t] = 1.0
    return jnp.asarray(sel, dtype=BF16), jnp.asarray(pat, dtype=F32)


def _sumsq_selector():
    sel = np.zeros((QK_W, LANES), np.float32)
    for h in range(N_HEADS):
        sel[h * HEAD_DIM:(h + 1) * HEAD_DIM, SMALL_BOUND + h] = 1.0
    return jnp.asarray(sel, dtype=BF16)


def _foxprep_kernel(q_ref, k_ref, v_ref, sm_ref, qw_ref, kw_ref, fb_ref, bd_ref, tri_ref, ss_ref,
                    sel_ref, pat_ref,
                    qa_ref, ka_ref, vo_ref, ct_ref, bmax_ref, crange_ref, carry_ref, kmax_ref):
    @pl.when(pl.program_id(0) == 0)
    def _():
        carry_ref[...] = jnp.zeros(carry_ref.shape, F32)
        kmax_ref[...] = jnp.zeros(kmax_ref.shape, F32)
        bmax_ref[...] = jnp.zeros(bmax_ref.shape, F32)

    bd = bd_ref[...]

    def headnorm(x, w):
        ms = _head_sumsq(x, bd) * (1.0 / HEAD_DIM)
        return x * lax.rsqrt(ms + EPS) * w

    qb = (headnorm(q_ref[...], qw_ref[...]) * (HEAD_DIM ** -0.5)).astype(BF16)
    kb = headnorm(k_ref[...], kw_ref[...]).astype(BF16)
    vo_ref[...] = v_ref[...].astype(vo_ref.dtype)
    rows = qb.shape[0]

    xf = sm_ref[...] + fb_ref[...]
    log_f = jnp.minimum(xf, 0.0) - jnp.log1p(jnp.exp(-jnp.abs(xf)))
    cum = carry_ref[...] + _dot_sel_lhs(tri_ref[...], log_f)
    carry_ref[...] = cum[rows - 1:rows, :]
    ct_ref[0] = cum.T[SMALL_F:SMALL_F + N_HEADS, :]
    crange_ref[0, 0:1, :] = jnp.max(cum, axis=0, keepdims=True)
    crange_ref[0, 1:2, :] = jnp.min(cum, axis=0, keepdims=True)

    qf = qb.astype(F32)
    kf = kb.astype(F32)
    ss = ss_ref[...]
    kss = _dot((kf * kf).astype(BF16), ss)
    kmax = jnp.maximum(kmax_ref[...], jnp.max(kss, axis=0, keepdims=True))
    kmax_ref[...] = kmax
    bound = jnp.sqrt(_dot((qf * qf).astype(BF16), ss) * kmax)
    bmax_ref[...] = jnp.maximum(bmax_ref[...], jnp.max(bound, axis=0, keepdims=True))

    lane = _lane_iota((rows, LANES))
    terms = jnp.concatenate(_split3(jnp.where(lane < SMALL_BOUND, cum, -bound)), axis=1)
    extras = (_dot(terms, sel_ref[...]) + pat_ref[...]).astype(BF16)
    is_even = lane < HEAD_DIM
    for p in range(N_PAIRS):
        ls = slice(p * LANES, (p + 1) * LANES)
        for hh in range(2):
            h = 2 * p + hh
            keep = is_even if hh == 0 else jnp.logical_not(is_even)
            qa_ref[p, hh, :, 0:LANES] = jnp.where(keep, qb[:, ls], jnp.zeros_like(qb[:, ls]))
            qa_ref[p, hh, :, LANES:2 * LANES] = extras[:, h * LANES:(h + 1) * LANES]
        ka_ref[:, 2 * p * LANES:(2 * p + 1) * LANES] = kb[:, ls]
        ka_ref[:, (2 * p + 1) * LANES:(2 * p + 2) * LANES] = extras[
            :, (N_HEADS + p) * LANES:(N_HEADS + p + 1) * LANES]


def _foxprep(proj, qw_tiled, kw_tiled, fb_pad, bd, tri, *, rows):
    s = proj.shape[0]
    const = lambda shape: pl.BlockSpec(shape, lambda i: (0,) * len(shape))
    ss_sel = _sumsq_selector()
    ex_sel, ex_pat = _extras_selector()
    return pl.pallas_call(
        _foxprep_kernel,
        grid=(s // rows,),
        in_specs=[
            pl.BlockSpec((rows, QK_W), lambda i: (i, COL_FOX_Q // QK_W)),
            pl.BlockSpec((rows, QK_W), lambda i: (i, COL_FOX_K // QK_W)),
            pl.BlockSpec((rows, QK_W), lambda i: (i, COL_FOX_V // QK_W)),
            pl.BlockSpec((rows, LANES), lambda i: (i, COL_SMALL // LANES)),
            const((1, QK_W)),
            const((1, QK_W)),
            const((1, LANES)),
            const((QK_W, QK_W)),
            const((rows, rows)),
            const(ss_sel.shape),
            const(ex_sel.shape),
            const(ex_pat.shape),
        ],
        out_specs=[
            pl.BlockSpec((N_PAIRS, 2, rows, 2 * LANES), lambda i: (0, 0, i, 0)),
            pl.BlockSpec((rows, 2 * QK_W), lambda i: (i, 0)),
            pl.BlockSpec((rows, QK_W), lambda i: (i, 0)),
            pl.BlockSpec((1, N_HEADS, rows), lambda i: (i, 0, 0)),
            pl.BlockSpec((1, LANES), lambda i: (0, 0)),
            pl.BlockSpec((1, 2, LANES), lambda i: (i, 0, 0)),
        ],
        out_shape=[
            jax.ShapeDtypeStruct((N_PAIRS, 2, s, 2 * LANES), BF16),
            jax.ShapeDtypeStruct((s, 2 * QK_W), BF16),
            jax.ShapeDtypeStruct((s, QK_W), BF16),
            jax.ShapeDtypeStruct((s // rows, N_HEADS, rows), F32),
            jax.ShapeDtypeStruct((1, LANES), F32),
            jax.ShapeDtypeStruct((s // rows, 2, LANES), F32),
        ],
        scratch_shapes=[pltpu.VMEM((1, LANES), F32), pltpu.VMEM((1, LANES), F32)],
        compiler_params=pltpu.CompilerParams(
            dimension_semantics=("arbitrary",), vmem_limit_bytes=VMEM_LIMIT),
        name="foxprep",
    )(proj, proj, proj, proj, qw_tiled, kw_tiled, fb_pad, bd, tri, ss_sel, ex_sel, ex_pat)


FAST_BOUND_MAX = 40.0


def _fox_kernel(fast_ref, first_ref, q_ref, k_ref, v_ref, c_ref, z_ref, nw_ref, o_ref, m_ref,
                acc_ref, *, tq, tk):
    p = pl.program_id(0)
    i = pl.program_id(1)
    first = first_ref[p * pl.num_programs(1) + i]
    n_sub = tq // tk

    acc_ref[...] = jnp.zeros(acc_ref.shape, F32)
    ones = jnp.ones((tk, LANES), BF16)

    def kv_tile(j):
        start = pl.multiple_of(j * tk, tk)
        vj = jnp.concatenate([v_ref[pl.ds(start, tk), :], ones], axis=1)
        return k_ref[pl.ds(start, tk), :], vj

    def causal_keep(j):
        q_pos = i * tq + lax.broadcasted_iota(jnp.int32, (tq, tk), 0)
        k_pos = j * tk + lax.broadcasted_iota(jnp.int32, (tq, tk), 1)
        return k_pos <= q_pos

    def fast_tile(j, masked):
        kj, vj = kv_tile(j)
        for hh in range(2):
            s = _dot_nt(q_ref[0, hh], kj)
            if masked:
                s = jnp.where(causal_keep(j), s, -jnp.inf)
            acc_ref[hh] += _dot(jnp.exp(s).astype(BF16), vj)

    def slow_tile(j, masked):
        kj, vj = kv_tile(j)
        for hh in range(2):
            ck = c_ref[j, pl.ds(2 * p + hh, 1), :]
            s = _dot_nt(q_ref[0, hh, :, 0:LANES], kj[:, 0:LANES]) - ck
            if masked:
                s = jnp.where(causal_keep(j), s, -jnp.inf)
            m_prev = m_ref[hh]
            m_new = jnp.maximum(m_prev, jnp.max(s, axis=-1, keepdims=True))
            alpha = jnp.exp(m_prev - m_new)
            pr = jnp.exp(s - m_new)
            acc_ref[hh] = alpha * acc_ref[hh] + _dot(pr.astype(BF16), vj)
            m_ref[hh] = m_new

    def run(tile):
        def body(j, carry):
            tile(j, False)
            return carry
        lax.fori_loop(first, i * n_sub, body, 0)
        for d in range(n_sub):
            tile(i * n_sub + d, True)

    @pl.when(fast_ref[0] == 1)
    def _():
        run(fast_tile)

    @pl.when(fast_ref[0] != 1)
    def _():
        m_ref[...] = jnp.full(m_ref.shape, -jnp.inf, F32)
        run(slow_tile)

    acc_e = acc_ref[0]
    acc_o = acc_ref[1]
    is_even = _lane_iota((tq, LANES)) < HEAD_DIM
    o = jnp.where(is_even, acc_e[:, 0:LANES] / acc_e[:, LANES:2 * LANES],
                  acc_o[:, 0:LANES] / acc_o[:, LANES:2 * LANES])
    o2 = o * o
    ss_e = jnp.sum(jnp.where(is_even, o2, 0.0), axis=-1, keepdims=True)
    ss_o = jnp.sum(jnp.where(is_even, 0.0, o2), axis=-1, keepdims=True)
    ms = jnp.where(is_even, ss_e, ss_o) * (1.0 / HEAD_DIM)
    z = z_ref[...]
    o_ref[...] = (o * lax.rsqrt(ms + EPS) * nw_ref[...] * (z * _sigmoid(z))).astype(o_ref.dtype)


SKIP_LOG = 110.0


def _first_key_tile(crange, bmax, *, tq, tk):
    n_k = crange.shape[0]
    n_sub = tq // tk
    c_max = crange[:, 0, SMALL_F:SMALL_F + N_HEADS].T
    c_min = crange[:, 1, SMALL_F:SMALL_F + N_HEADS].T
    cq_max = c_max.reshape(N_HEADS, n_k // n_sub, n_sub).max(axis=-1)
    bound = bmax[0, SMALL_BOUND:SMALL_BOUND + N_HEADS]
    gap = cq_max[:, :, None] - c_min[:, None, :] + 2.0 * bound[:, None, None]
    dead = gap <= -SKIP_LOG
    tile_idx = lax.broadcasted_iota(jnp.int32, dead.shape, 2)
    first = jnp.min(jnp.where(dead, n_k, tile_idx), axis=-1)
    first = jnp.min(first.reshape(N_PAIRS, 2, -1), axis=1)
    first = jnp.minimum(first, jnp.arange(first.shape[1], dtype=jnp.int32) * n_sub)
    return first.reshape(-1).astype(jnp.int32)


def _fox(fast, first, qa, ka, vb, ct, proj, nw_pair, *, tq, tk):
    s = vb.shape[0]
    grid_spec = pltpu.PrefetchScalarGridSpec(
        num_scalar_prefetch=2,
        grid=(N_PAIRS, s // tq),
        in_specs=[
            pl.BlockSpec((1, 2, tq, 2 * LANES), lambda p, i, *_: (p, 0, i, 0)),
            pl.BlockSpec((s, 2 * LANES), lambda p, i, *_: (0, p)),
            pl.BlockSpec((s, LANES), lambda p, i, *_: (0, p)),
            pl.BlockSpec(ct.shape, lambda p, i, *_: (0, 0, 0)),
            pl.BlockSpec((tq, LANES), lambda p, i, *_: (i, COL_FOX_Z // LANES + p)),
            pl.BlockSpec((1, LANES), lambda p, i, *_: (0, 0)),
        ],
        out_specs=pl.BlockSpec((tq, LANES), lambda p, i, *_: (i, p)),
        scratch_shapes=[
            pltpu.VMEM((2, tq, 1), F32),
            pltpu.VMEM((2, tq, 2 * LANES), F32),
        ],
    )
    return pl.pallas_call(
        functools.partial(_fox_kernel, tq=tq, tk=tk),
        grid_spec=grid_spec,
        out_shape=jax.ShapeDtypeStruct((s, QK_W), BF16),
        compiler_params=pltpu.CompilerParams(
            dimension_semantics=("arbitrary", "arbitrary"), vmem_limit_bytes=VMEM_LIMIT),
        name="fox",
    )(fast, first, qa, ka, vb, ct, proj, nw_pair)


def _merge_kernel(x_ref, ya_ref, yb_ref, ga_ref, gb_ref, gbias_ref, pa_ref, pb_ref, wo_ref,
                  fw_ref, o_ref):
    d = x_ref.shape[1]
    ua = _dot(ya_ref[...].astype(BF16), pa_ref[...])
    ub = _dot(yb_ref[...].astype(BF16), pb_ref[...])
    ga = _sigmoid(ga_ref[...] + gbias_ref[:, 0:d])
    gb = _sigmoid(gb_ref[...] + gbias_ref[:, d:2 * d])
    merged = ga * ua + gb * ub
    xn = x_ref[...] + _dot(merged.astype(BF16), wo_ref[...])
    ms = jnp.mean(xn * xn, axis=-1, keepdims=True)
    o_ref[...] = xn * lax.rsqrt(ms + EPS) * fw_ref[...]


def _merge(x, ya, yb, proj, gate_b, w_up_a, w_up_b, w_out, out_w, *, tm=512):
    s, d = x.shape
    const = lambda shape: pl.BlockSpec(shape, lambda i: (0,) * len(shape))
    return pl.pallas_call(
        _merge_kernel,
        grid=(s // tm,),
        in_specs=[
            pl.BlockSpec((tm, d), lambda i: (i, 0)),
            pl.BlockSpec((tm, QK_W), lambda i: (i, 0)),
            pl.BlockSpec((tm, QK_W), lambda i: (i, 0)),
            pl.BlockSpec((tm, d), lambda i: (i, COL_GATE // d)),
            pl.BlockSpec((tm, d), lambda i: (i, COL_GATE // d + 1)),
            const((1, 2 * d)),
            const((QK_W, d)),
            const((QK_W, d)),
            const((d, d)),
            const((1, d)),
        ],
        out_specs=pl.BlockSpec((tm, d), lambda i: (i, 0)),
        out_shape=jax.ShapeDtypeStruct((s, d), F32),
        compiler_params=pltpu.CompilerParams(
            dimension_semantics=("arbitrary",), vmem_limit_bytes=VMEM_LIMIT),
        name="merge",
    )(x, ya, yb, proj, proj, gate_b, w_up_a, w_up_b, w_out, out_w)


def _block_diag_ones(n, blk):
    idx = np.arange(n) // blk
    return jnp.asarray(idx[:, None] == idx[None, :], dtype=BF16)


def _lower_tri_ones(n):
    idx = np.arange(n)
    return jnp.asarray(idx[None, :] <= idx[:, None], dtype=BF16)


def _expand_matrix():
    e = np.zeros((LANES, 2 * QK_W + N_HEADS * LANES), np.float32)
    for h in range(N_HEADS):
        e[SMALL_B + h, h * HEAD_DIM:(h + 1) * HEAD_DIM] = 1.0
        e[SMALL_A + h, QK_W + h * HEAD_DIM:QK_W + (h + 1) * HEAD_DIM] = 1.0
        e[SMALL_A + h, 2 * QK_W + h * LANES:2 * QK_W + (h + 1) * LANES] = 1.0
    return jnp.asarray(e, dtype=BF16)


def _pad_lanes(vec, offset):
    return jnp.zeros((1, LANES), F32).at[0, offset:offset + vec.shape[0]].set(vec.astype(F32))


def _regroup_w_in(w):
    d = w.shape[0]
    o_z = 3 * QK_W
    o_b = o_z + QK_W
    o_a = o_b + N_HEADS
    o_fq = o_a + N_HEADS
    o_ff = o_fq + 4 * QK_W
    o_gate = o_ff + N_HEADS
    small = jnp.concatenate(
        [w[:, o_b:o_a], w[:, o_a:o_fq], w[:, o_ff:o_gate],
         jnp.zeros((d, LANES - 3 * N_HEADS), w.dtype)], axis=1)
    return jnp.concatenate([w[:, 0:o_b], w[:, o_fq:o_ff], w[:, o_gate:], small], axis=1).astype(BF16)


FOX_TQ = 512
FOX_TK = 512


def kernel(x, norm_w, w_in, gate_b, conv_w, a_log, dt_bias, gdn_norm_w, f_bias,
           fox_qn_w, fox_kn_w, fox_on_w, w_up_gdn, w_up_fox, w_out, final_norm_w):
    batch, s, d = x.shape
    assert norm_w.shape[0] == 1, "the merge call fuses the final RMSNorm: single-layer trunk only"
    assert s % FOX_TQ == 0 and s % GDN_CHUNK == 0
    assert w_in.shape[2] == P_WIDTH - LANES + 3 * N_HEADS
    l = 0

    bd = _block_diag_ones(QK_W, HEAD_DIM)
    tri_gdn = _lower_tri_ones(GDN_CHUNK)
    tri_fox = _lower_tri_ones(FOX_TK)
    expand = _expand_matrix()
    tile_heads = lambda w: jnp.tile(w.astype(F32), N_HEADS)[None, :]

    outs = []
    x_rows = x.reshape(batch * s, d)
    for b in range(batch):
        xb = x_rows if batch == 1 else x_rows[b * s:(b + 1) * s]
        proj = _proj(xb, norm_w[l][None, :], _regroup_w_in(w_in.reshape(w_in.shape[1:])))
        ya = _gdn(proj, conv_w[l], _pad_lanes(a_log[l], SMALL_A), _pad_lanes(dt_bias[l], SMALL_A),
                  tile_heads(gdn_norm_w[l]), bd, tri_gdn, expand)
        qa, ka, vb, ct, bmax, crange = _foxprep(
            proj, tile_heads(fox_qn_w[l]), tile_heads(fox_kn_w[l]),
            _pad_lanes(f_bias[l], SMALL_F), bd, tri_fox, rows=FOX_TK)
        bound_max = jnp.max(bmax[0, SMALL_BOUND:SMALL_BOUND + N_HEADS])
        fast = (bound_max <= FAST_BOUND_MAX).astype(jnp.int32).reshape(1)
        first = _first_key_tile(crange, bmax, tq=FOX_TQ, tk=FOX_TK)
        yb = _fox(fast, first, qa, ka, vb, ct, proj,
                  jnp.tile(fox_on_w[l].astype(F32), 2)[None, :], tq=FOX_TQ, tk=FOX_TK)
        outs.append(_merge(xb, ya, yb, proj, gate_b[l][None, :].astype(F32),
                           w_up_gdn[l].astype(BF16), w_up_fox[l].astype(BF16),
                           w_out[l].astype(BF16), final_norm_w[None, :].astype(F32)))
    out = outs[0] if batch == 1 else jnp.concatenate(outs, axis=0)
    return out.reshape(batch, s, d)
```

```python
import functools

import numpy as np
import jax
import jax.numpy as jnp
from jax import lax
from jax.experimental import pallas as pl
from jax.experimental.pallas import tpu as pltpu

F32 = jnp.float32
BF16 = jnp.bfloat16

EPS = 1e-6
LANES = 128
HEAD_DIM = 64
N_HEADS = 8
N_PAIRS = N_HEADS // 2
QK_W = N_HEADS * HEAD_DIM
CONV_K = 4
GDN_CHUNK = 128
VMEM_LIMIT = 56 * 1024 * 1024

COL_GDN_QKV = 0
COL_GDN_Z = 1536
COL_FOX_Q = 2048
COL_FOX_K = 2560
COL_FOX_V = 3072
COL_FOX_Z = 3584
COL_GATE = 4096
COL_SMALL = 6144
P_WIDTH = 6272
SMALL_B, SMALL_A, SMALL_F = 0, 8, 16


def _dot(a, b):
    return jnp.dot(a, b, preferred_element_type=F32)


def _dot_nt(a, b):
    return lax.dot_general(a, b, (((1,), (1,)), ((), ())), preferred_element_type=F32)


def _split3(x):
    hi = x.astype(BF16)
    r = x - hi.astype(F32)
    mid = r.astype(BF16)
    lo = (r - mid.astype(F32)).astype(BF16)
    return hi, mid, lo


def _dot_sel_rhs(x, sel):
    hi, mid, lo = _split3(x)
    return _dot(hi, sel) + _dot(mid, sel) + _dot(lo, sel)


def _dot_sel_lhs(sel, x):
    hi, mid, lo = _split3(x)
    return _dot(sel, hi) + _dot(sel, mid) + _dot(sel, lo)


def _head_sumsq(x, bd):
    return _dot((x * x).astype(BF16), bd)


def _sigmoid(x):
    return 1.0 / (1.0 + jnp.exp(-x))


def _softplus(x):
    return jnp.maximum(x, 0.0) + jnp.log1p(jnp.exp(-jnp.abs(x)))


def _lane_iota(shape):
    return lax.broadcasted_iota(jnp.int32, shape, len(shape) - 1)


def _proj_kernel(x_ref, nw_ref, w_ref, o_ref, *, n_chunk):
    x = x_ref[...]
    ms = jnp.mean(x * x, axis=-1, keepdims=True)
    h = (x * lax.rsqrt(ms + EPS) * nw_ref[...]).astype(BF16)
    width = o_ref.shape[1]
    for c0 in range(0, width, n_chunk):
        c1 = min(c0 + n_chunk, width)
        o_ref[:, c0:c1] = _dot(h, w_ref[:, c0:c1])


def _proj(x, norm_w, w_all, *, tm=256):
    s, d = x.shape
    width = w_all.shape[1]
    return pl.pallas_call(
        functools.partial(_proj_kernel, n_chunk=1024),
        grid=(s // tm,),
        in_specs=[
            pl.BlockSpec((tm, d), lambda i: (i, 0)),
            pl.BlockSpec((1, d), lambda i: (0, 0)),
            pl.BlockSpec((d, width), lambda i: (0, 0)),
        ],
        out_specs=pl.BlockSpec((tm, width), lambda i: (i, 0)),
        out_shape=jax.ShapeDtypeStruct((s, width), F32),
        compiler_params=pltpu.CompilerParams(
            dimension_semantics=("arbitrary",), vmem_limit_bytes=VMEM_LIMIT),
        name="proj",
    )(x, norm_w, w_all)


INV_BASE_LOG2 = 5


def _shr(x, n):
    return lax.shift_right_logical(x, n)


def _inverse_masks(row, col):
    c = row.shape[0]
    lb = INV_BASE_LOG2
    diag = _shr(row, lb) == _shr(col, lb)
    levels = []
    b = lb
    while (1 << b) < c:
        in_pair = _shr(row, b + 1) == _shr(col, b + 1)
        levels.append(in_pair & ((_shr(row, b) & 1) == 1) & ((_shr(col, b) & 1) == 0))
        b += 1
    return diag, levels


def _unit_lower_inverses(ms, eye, masks):
    c = ms[0].shape[0]
    diag, levels = masks
    mds = [jnp.where(diag, m, 0.0) for m in ms]
    ts = [eye - md for md in mds]
    mps = [_dot(md.astype(BF16), md.astype(BF16)) for md in mds]
    n_sq = INV_BASE_LOG2 - 1
    for it in range(n_sq):
        if it < n_sq - 1:
            boths = [_dot(jnp.concatenate([t, mp], axis=0).astype(BF16), mp.astype(BF16))
                     for t, mp in zip(ts, mps)]
            ts = [t + both[0:c, :] for t, both in zip(ts, boths)]
            mps = [both[c:2 * c, :] for both in boths]
        else:
            ts = [t + _dot(t.astype(BF16), mp.astype(BF16)) for t, mp in zip(ts, mps)]
    for level in levels:
        tls = [_dot(t.astype(BF16), jnp.where(level, m, 0.0).astype(BF16)) for t, m in zip(ts, ms)]
        ts = [t - _dot(tl.astype(BF16), t.astype(BF16)) for t, tl in zip(ts, tls)]
    return ts


def _gdn_kernel(qkv_ref, z_ref, sm_ref, cw_ref, alog_ref, dtb_ref, nw_ref,
                bd_ref, tri_ref, ex_ref, o_ref, xext_ref, state_ref):
    c = GDN_CHUNK
    step = pl.program_id(0)

    @pl.when(step == 0)
    def _():
        xext_ref[0:8, :] = jnp.zeros((8, xext_ref.shape[1]), F32)
        state_ref[...] = jnp.zeros(state_ref.shape, F32)

    xext_ref[8:8 + c, :] = qkv_ref[...]
    y = cw_ref[3:4, :] * xext_ref[8:8 + c, :]
    for j in range(CONV_K - 1):
        y = y + cw_ref[j:j + 1, :] * xext_ref[5 + j:5 + j + c, :]
    xext_ref[0:8, :] = qkv_ref[c - 8:c, :]
    y = y * _sigmoid(y)

    bd = bd_ref[...]
    q = y[:, 0:QK_W]
    k = y[:, QK_W:2 * QK_W]
    v = y[:, 2 * QK_W:3 * QK_W]
    q = q * lax.rsqrt(_head_sumsq(q, bd) + EPS) * (HEAD_DIM ** -0.5)
    k = k * lax.rsqrt(_head_sumsq(k, bd) + EPS)

    sm = sm_ref[...]
    beta_s = _sigmoid(sm)
    g_s = -jnp.exp(alog_ref[...]) * _softplus(sm + dtb_ref[...])
    g_cum = _dot_sel_lhs(tri_ref[...], g_s)
    beta = _dot(beta_s.astype(BF16), ex_ref[:, 0:QK_W])
    g = _dot_sel_rhs(g_cum, ex_ref[:, QK_W:2 * QK_W])
    g_bc = _dot_sel_rhs(g_cum, ex_ref[:, 2 * QK_W:2 * QK_W + N_HEADS * LANES])
    g_t = g_cum.T

    eg = jnp.exp(g)
    g_last = g[c - 1:c, :]
    e_last = jnp.exp(g_last)
    kb = k * beta
    vb = v * beta
    kbg = kb * eg
    qg = q * eg
    kt = k * jnp.exp(g_last - g)

    row = lax.broadcasted_iota(jnp.int32, (c, c), 0)
    col = lax.broadcasted_iota(jnp.int32, (c, c), 1)
    causal = col <= row
    strict = col < row
    eye = (col == row).astype(F32)
    inv_masks = _inverse_masks(row, col)
    lane = _lane_iota((c, LANES))
    is_even = lane < HEAD_DIM
    bd_mask = (lane < HEAD_DIM) == (lax.broadcasted_iota(jnp.int32, (c, LANES), 0) < HEAD_DIM)

    pair_lanes = [slice(p * LANES, (p + 1) * LANES) for p in range(N_PAIRS)]
    odd = jnp.logical_not(is_even)

    kq = []
    for ls in pair_lanes:
        lhs = jnp.concatenate([jnp.where(is_even, kb[:, ls], 0.0), jnp.where(odd, kb[:, ls], 0.0),
                               jnp.where(is_even, q[:, ls], 0.0), jnp.where(odd, q[:, ls], 0.0)],
                              axis=0)
        kq.append(_dot_nt(lhs.astype(BF16), k[:, ls].astype(BF16)))
    ms, a_h = [], []
    for h in range(N_HEADS):
        p, hh = divmod(h, 2)
        diff = g_bc[:, h * LANES:(h + 1) * LANES] - g_t[SMALL_A + h:SMALL_A + h + 1, :]
        dec = jnp.where(causal, jnp.exp(jnp.where(causal, diff, 0.0)), 0.0)
        ms.append(jnp.where(strict, kq[p][hh * c:(hh + 1) * c, :] * dec, 0.0))
        a_h.append((kq[p][(2 + hh) * c:(3 + hh) * c, :] * dec).astype(BF16))
    ts = _unit_lower_inverses(ms, eye, inv_masks)
    rhs_wk = [jnp.concatenate([vb[:, ls], kbg[:, ls]], axis=1).astype(BF16) for ls in pair_lanes]
    wk = [_dot(ts[h].astype(BF16), rhs_wk[h // 2]) for h in range(N_HEADS)]

    sts = [state_ref[p] for p in range(N_PAIRS)]
    stb = [st.astype(BF16) for st in sts]
    ub = []
    for p in range(N_PAIRS):
        w_pair = jnp.where(is_even, wk[2 * p][:, 0:LANES], wk[2 * p + 1][:, 0:LANES])
        kc_pair = jnp.where(is_even, wk[2 * p][:, LANES:2 * LANES], wk[2 * p + 1][:, LANES:2 * LANES])
        ub.append((w_pair - _dot(kc_pair.astype(BF16), stb[p])).astype(BF16))
    for p, ls in enumerate(pair_lanes):
        o = _dot(qg[:, ls].astype(BF16), stb[p]) + jnp.where(
            is_even, _dot(a_h[2 * p], ub[p]), _dot(a_h[2 * p + 1], ub[p]))
        upd = _dot(kt[:, ls].T.astype(BF16), ub[p])
        state_ref[p] = sts[p] * e_last[:, ls] + jnp.where(bd_mask, upd, 0.0)
        o_ref[:, ls] = o.astype(o_ref.dtype)

    o_all = o_ref[...].astype(F32)
    z = z_ref[...]
    ms = _head_sumsq(o_all, bd) * (1.0 / HEAD_DIM)
    yn = o_all * lax.rsqrt(ms + EPS) * nw_ref[...]
    o_ref[...] = (yn * (z * _sigmoid(z))).astype(o_ref.dtype)


def _gdn(proj, conv_w, a_log_pad, dtb_pad, nw_tiled, bd, tri, expand):
    s = proj.shape[0]
    c = GDN_CHUNK
    conv_width = 3 * QK_W
    const = lambda shape: pl.BlockSpec(shape, lambda i: (0,) * len(shape))
    return pl.pallas_call(
        _gdn_kernel,
        grid=(s // c,),
        in_specs=[
            pl.BlockSpec((c, conv_width), lambda i: (i, COL_GDN_QKV // conv_width)),
            pl.BlockSpec((c, QK_W), lambda i: (i, COL_GDN_Z // QK_W)),
            pl.BlockSpec((c, LANES), lambda i: (i, COL_SMALL // LANES)),
            const((CONV_K, conv_width)),
            const((1, LANES)),
            const((1, LANES)),
            const((1, QK_W)),
            const((QK_W, QK_W)),
            const((c, c)),
            const(expand.shape),
        ],
        out_specs=pl.BlockSpec((c, QK_W), lambda i: (i, 0)),
        out_shape=jax.ShapeDtypeStruct((s, QK_W), F32),
        scratch_shapes=[
            pltpu.VMEM((c + 8, conv_width), F32),
            pltpu.VMEM((N_PAIRS, LANES, LANES), F32),
        ],
        compiler_params=pltpu.CompilerParams(
            dimension_semantics=("arbitrary",), vmem_limit_bytes=VMEM_LIMIT),
        name="gdn",
    )(proj, proj, proj, conv_w, a_log_pad, dtb_pad, nw_tiled, bd, tri, expand)


X_C, X_EVEN, X_ODD, X_BOUND = 0, 3, 6, 9
SMALL_BOUND = 24
N_EXTRA_COLS = (N_HEADS + N_PAIRS) * LANES


def _extras_selector():
    sel = np.zeros((3 * LANES, N_EXTRA_COLS), np.float32)
    pat = np.zeros((1, N_EXTRA_COLS), np.float32)
    k0 = N_HEADS * LANES
    for t in range(3):
        r0 = t * LANES
        for h in range(N_HEADS):
            sel[r0 + SMALL_F + h, h * LANES + X_C + t] = 1.0
            sel[r0 + SMALL_BOUND + h, h * LANES + X_BOUND + t] = 1.0
            pat[0, h * LANES + (X_EVEN if h % 2 == 0 else X_ODD) + t] = 1.0
        for p in range(N_PAIRS):
            sel[r0 + SMALL_F + 2 * p, k0 + p * LANES + X_EVEN + t] = -1.0
            sel[r0 + SMALL_F + 2 * p + 1, k0 + p * LANES + X_ODD + t] = -1.0
            pat[0, k0 + p * LANES + X_C + t] = 1.0
            pat[0, k0 + p * LANES + X_BOUND + t] = 1.0
    return jnp.asarray(sel, dtype=BF16), jnp.asarray(pat, dtype=F32)


def _sumsq_selector():
    sel = np.zeros((QK_W, LANES), np.float32)
    for h in range(N_HEADS):
        sel[h * HEAD_DIM:(h + 1) * HEAD_DIM, SMALL_BOUND + h] = 1.0
    return jnp.asarray(sel, dtype=BF16)


def _foxprep_kernel(q_ref, k_ref, v_ref, sm_ref, qw_ref, kw_ref, fb_ref, bd_ref, tri_ref, ss_ref,
                    sel_ref, pat_ref,
                    qa_ref, ka_ref, vo_ref, ct_ref, bmax_ref, crange_ref, carry_ref, kmax_ref):
    @pl.when(pl.program_id(0) == 0)
    def _():
        carry_ref[...] = jnp.zeros(carry_ref.shape, F32)
        kmax_ref[...] = jnp.zeros(kmax_ref.shape, F32)
        bmax_ref[...] = jnp.zeros(bmax_ref.shape, F32)

    bd = bd_ref[...]

    def headnorm(x, w):
        ms = _head_sumsq(x, bd) * (1.0 / HEAD_DIM)
        return x * lax.rsqrt(ms + EPS) * w

    qb = (headnorm(q_ref[...], qw_ref[...]) * (HEAD_DIM ** -0.5)).astype(BF16)
    kb = headnorm(k_ref[...], kw_ref[...]).astype(BF16)
    vo_ref[...] = v_ref[...].astype(vo_ref.dtype)
    rows = qb.shape[0]

    xf = sm_ref[...] + fb_ref[...]
    log_f = jnp.minimum(xf, 0.0) - jnp.log1p(jnp.exp(-jnp.abs(xf)))
    cum = carry_ref[...] + _dot_sel_lhs(tri_ref[...], log_f)
    carry_ref[...] = cum[rows - 1:rows, :]
    ct_ref[0] = cum.T[SMALL_F:SMALL_F + N_HEADS, :]
    crange_ref[0, 0:1, :] = jnp.max(cum, axis=0, keepdims=True)
    crange_ref[0, 1:2, :] = jnp.min(cum, axis=0, keepdims=True)

    qf = qb.astype(F32)
    kf = kb.astype(F32)
    ss = ss_ref[...]
    kss = _dot((kf * kf).astype(BF16), ss)
    kmax = jnp.maximum(kmax_ref[...], jnp.max(kss, axis=0, keepdims=True))
    kmax_ref[...] = kmax
    bound = jnp.sqrt(_dot((qf * qf).astype(BF16), ss) * kmax)
    bmax_ref[...] = jnp.maximum(bmax_ref[...], jnp.max(bound, axis=0, keepdims=True))

    lane = _lane_iota((rows, LANES))
    terms = jnp.concatenate(_split3(jnp.where(lane < SMALL_BOUND, cum, -bound)), axis=1)
    extras = (_dot(terms, sel_ref[...]) + pat_ref[...]).astype(BF16)
    is_even = lane < HEAD_DIM
    for p in range(N_PAIRS):
        ls = slice(p * LANES, (p + 1) * LANES)
        for hh in range(2):
            h = 2 * p + hh
            keep = is_even if hh == 0 else jnp.logical_not(is_even)
            qa_ref[p, hh, :, 0:LANES] = jnp.where(keep, qb[:, ls], jnp.zeros_like(qb[:, ls]))
            qa_ref[p, hh, :, LANES:2 * LANES] = extras[:, h * LANES:(h + 1) * LANES]
        ka_ref[:, 2 * p * LANES:(2 * p + 1) * LANES] = kb[:, ls]
        ka_ref[:, (2 * p + 1) * LANES:(2 * p + 2) * LANES] = extras[
            :, (N_HEADS + p) * LANES:(N_HEADS + p + 1) * LANES]


def _foxprep(proj, qw_tiled, kw_tiled, fb_pad, bd, tri, *, rows):
    s = proj.shape[0]
    const = lambda shape: pl.BlockSpec(shape, lambda i: (0,) * len(shape))
    ss_sel = _sumsq_selector()
    ex_sel, ex_pat = _extras_selector()
    return pl.pallas_call(
        _foxprep_kernel,
        grid=(s // rows,),
        in_specs=[
            pl.BlockSpec((rows, QK_W), lambda i: (i, COL_FOX_Q // QK_W)),
            pl.BlockSpec((rows, QK_W), lambda i: (i, COL_FOX_K // QK_W)),
            pl.BlockSpec((rows, QK_W), lambda i: (i, COL_FOX_V // QK_W)),
            pl.BlockSpec((rows, LANES), lambda i: (i, COL_SMALL // LANES)),
            const((1, QK_W)),
            const((1, QK_W)),
            const((1, LANES)),
            const((QK_W, QK_W)),
            const((rows, rows)),
            const(ss_sel.shape),
            const(ex_sel.shape),
            const(ex_pat.shape),
        ],
        out_specs=[
            pl.BlockSpec((N_PAIRS, 2, rows, 2 * LANES), lambda i: (0, 0, i, 0)),
            pl.BlockSpec((rows, 2 * QK_W), lambda i: (i, 0)),
            pl.BlockSpec((rows, QK_W), lambda i: (i, 0)),
            pl.BlockSpec((1, N_HEADS, rows), lambda i: (i, 0, 0)),
            pl.BlockSpec((1, LANES), lambda i: (0, 0)),
            pl.BlockSpec((1, 2, LANES), lambda i: (i, 0, 0)),
        ],
        out_shape=[
            jax.ShapeDtypeStruct((N_PAIRS, 2, s, 2 * LANES), BF16),
            jax.ShapeDtypeStruct((s, 2 * QK_W), BF16),
            jax.ShapeDtypeStruct((s, QK_W), BF16),
            jax.ShapeDtypeStruct((s // rows, N_HEADS, rows), F32),
            jax.ShapeDtypeStruct((1, LANES), F32),
            jax.ShapeDtypeStruct((s // rows, 2, LANES), F32),
        ],
        scratch_shapes=[pltpu.VMEM((1, LANES), F32), pltpu.VMEM((1, LANES), F32)],
        compiler_params=pltpu.CompilerParams(
            dimension_semantics=("arbitrary",), vmem_limit_bytes=VMEM_LIMIT),
        name="foxprep",
    )(proj, proj, proj, proj, qw_tiled, kw_tiled, fb_pad, bd, tri, ss_sel, ex_sel, ex_pat)


FAST_BOUND_MAX = 40.0
FAST_TAIL_TILES = 3


def _fox_kernel(fast_ref, first_ref, q_ref, k_ref, v_ref, c_ref, z_ref, nw_ref, o_ref, m_ref,
                acc_ref, *, tq, tk):
    p = pl.program_id(0)
    i = pl.program_id(1)
    first = first_ref[p * pl.num_programs(1) + i]
    n_sub = tq // tk

    acc_ref[...] = jnp.zeros(acc_ref.shape, F32)
    ones = jnp.ones((tk, LANES), BF16)

    def kv_tile(j):
        start = pl.multiple_of(j * tk, tk)
        vj = jnp.concatenate([v_ref[pl.ds(start, tk), :], ones], axis=1)
        return k_ref[pl.ds(start, tk), :], vj

    def causal_keep(j):
        q_pos = i * tq + lax.broadcasted_iota(jnp.int32, (tq, tk), 0)
        k_pos = j * tk + lax.broadcasted_iota(jnp.int32, (tq, tk), 1)
        return k_pos <= q_pos

    def fast_tile(j, masked):
        kj, vj = kv_tile(j)
        for hh in range(2):
            s = _dot_nt(q_ref[0, hh], kj)
            if masked:
                s = jnp.where(causal_keep(j), s, -jnp.inf)
            acc_ref[hh] += _dot(jnp.exp(s).astype(BF16), vj)

    def slow_tile(j, masked):
        kj, vj = kv_tile(j)
        for hh in range(2):
            ck = c_ref[j, pl.ds(2 * p + hh, 1), :]
            s = _dot_nt(q_ref[0, hh, :, 0:LANES], kj[:, 0:LANES]) - ck
            if masked:
                s = jnp.where(causal_keep(j), s, -jnp.inf)
            m_prev = m_ref[hh]
            m_new = jnp.maximum(m_prev, jnp.max(s, axis=-1, keepdims=True))
            alpha = jnp.exp(m_prev - m_new)
            pr = jnp.exp(s - m_new)
            acc_ref[hh] = alpha * acc_ref[hh] + _dot(pr.astype(BF16), vj)
            m_ref[hh] = m_new

    def run(tile):
        def body(j, carry):
            tile(j, False)
            return carry
        lax.fori_loop(first, i * n_sub, body, 0)
        for d in range(n_sub):
            tile(i * n_sub + d, True)

    def fast_tail(n_tiles):
        span = n_tiles * tk
        start = pl.multiple_of((i - (n_tiles - 1)) * tk, tk)
        ks = k_ref[pl.ds(start, span), :]
        vs = jnp.concatenate([v_ref[pl.ds(start, span), :], jnp.ones((span, LANES), BF16)], axis=1)
        q_rel = (n_tiles - 1) * tk + lax.broadcasted_iota(jnp.int32, (tq, span), 0)
        keep = lax.broadcasted_iota(jnp.int32, (tq, span), 1) <= q_rel
        for hh in range(2):
            s = jnp.where(keep, _dot_nt(q_ref[0, hh], ks), -jnp.inf)
            acc_ref[hh] += _dot(jnp.exp(s).astype(BF16), vs)

    def run_fast():
        assert n_sub == 1
        def body(j, carry):
            fast_tile(j, False)
            return carry
        lax.fori_loop(first, jnp.maximum(first, i - (FAST_TAIL_TILES - 1)), body, 0)
        for n_tiles in range(1, FAST_TAIL_TILES):
            pl.when(i == n_tiles - 1)(functools.partial(fast_tail, n_tiles))
        pl.when(i >= FAST_TAIL_TILES - 1)(functools.partial(fast_tail, FAST_TAIL_TILES))

    @pl.when(fast_ref[0] == 1)
    def _():
        run_fast()

    @pl.when(fast_ref[0] != 1)
    def _():
        m_ref[...] = jnp.full(m_ref.shape, -jnp.inf, F32)
        run(slow_tile)

    acc_e = acc_ref[0]
    acc_o = acc_ref[1]
    is_even = _lane_iota((tq, LANES)) < HEAD_DIM
    o = jnp.where(is_even, acc_e[:, 0:LANES] / acc_e[:, LANES:2 * LANES],
                  acc_o[:, 0:LANES] / acc_o[:, LANES:2 * LANES])
    o2 = o * o
    ss_e = jnp.sum(jnp.where(is_even, o2, 0.0), axis=-1, keepdims=True)
    ss_o = jnp.sum(jnp.where(is_even, 0.0, o2), axis=-1, keepdims=True)
    ms = jnp.where(is_even, ss_e, ss_o) * (1.0 / HEAD_DIM)
    z = z_ref[...]
    o_ref[...] = (o * lax.rsqrt(ms + EPS) * nw_ref[...] * (z * _sigmoid(z))).astype(o_ref.dtype)


SKIP_LOG = 110.0


def _first_key_tile(crange, bmax, *, tq, tk):
    n_k = crange.shape[0]
    n_sub = tq // tk
    c_max = crange[:, 0, SMALL_F:SMALL_F + N_HEADS].T
    c_min = crange[:, 1, SMALL_F:SMALL_F + N_HEADS].T
    cq_max = c_max.reshape(N_HEADS, n_k // n_sub, n_sub).max(axis=-1)
    bound = bmax[0, SMALL_BOUND:SMALL_BOUND + N_HEADS]
    gap = cq_max[:, :, None] - c_min[:, None, :] + 2.0 * bound[:, None, None]
    dead = gap <= -SKIP_LOG
    tile_idx = lax.broadcasted_iota(jnp.int32, dead.shape, 2)
    first = jnp.min(jnp.where(dead, n_k, tile_idx), axis=-1)
    first = jnp.min(first.reshape(N_PAIRS, 2, -1), axis=1)
    first = jnp.minimum(first, jnp.arange(first.shape[1], dtype=jnp.int32) * n_sub)
    return first.reshape(-1).astype(jnp.int32)


def _fox(fast, first, qa, ka, vb, ct, proj, nw_pair, *, tq, tk):
    s = vb.shape[0]
    grid_spec = pltpu.PrefetchScalarGridSpec(
        num_scalar_prefetch=2,
        grid=(N_PAIRS, s // tq),
        in_specs=[
            pl.BlockSpec((1, 2, tq, 2 * LANES), lambda p, i, *_: (p, 0, i, 0)),
            pl.BlockSpec((s, 2 * LANES), lambda p, i, *_: (0, p)),
            pl.BlockSpec((s, LANES), lambda p, i, *_: (0, p)),
            pl.BlockSpec(ct.shape, lambda p, i, *_: (0, 0, 0)),
            pl.BlockSpec((tq, LANES), lambda p, i, *_: (i, COL_FOX_Z // LANES + p)),
            pl.BlockSpec((1, LANES), lambda p, i, *_: (0, 0)),
        ],
        out_specs=pl.BlockSpec((tq, LANES), lambda p, i, *_: (i, p)),
        scratch_shapes=[
            pltpu.VMEM((2, tq, 1), F32),
            pltpu.VMEM((2, tq, 2 * LANES), F32),
        ],
    )
    return pl.pallas_call(
        functools.partial(_fox_kernel, tq=tq, tk=tk),
        grid_spec=grid_spec,
        out_shape=jax.ShapeDtypeStruct((s, QK_W), BF16),
        compiler_params=pltpu.CompilerParams(
            dimension_semantics=("arbitrary", "arbitrary"), vmem_limit_bytes=VMEM_LIMIT),
        name="fox",
    )(fast, first, qa, ka, vb, ct, proj, nw_pair)


def _merge_kernel(x_ref, ya_ref, yb_ref, ga_ref, gb_ref, gbias_ref, pa_ref, pb_ref, wo_ref,
                  fw_ref, o_ref):
    d = x_ref.shape[1]
    ua = _dot(ya_ref[...].astype(BF16), pa_ref[...])
    ub = _dot(yb_ref[...].astype(BF16), pb_ref[...])
    ga = _sigmoid(ga_ref[...] + gbias_ref[:, 0:d])
    gb = _sigmoid(gb_ref[...] + gbias_ref[:, d:2 * d])
    merged = ga * ua + gb * ub
    xn = x_ref[...] + _dot(merged.astype(BF16), wo_ref[...])
    ms = jnp.mean(xn * xn, axis=-1, keepdims=True)
    o_ref[...] = xn * lax.rsqrt(ms + EPS) * fw_ref[...]


def _merge(x, ya, yb, proj, gate_b, w_up_a, w_up_b, w_out, out_w, *, tm=512):
    s, d = x.shape
    const = lambda shape: pl.BlockSpec(shape, lambda i: (0,) * len(shape))
    return pl.pallas_call(
        _merge_kernel,
        grid=(s // tm,),
        in_specs=[
            pl.BlockSpec((tm, d), lambda i: (i, 0)),
            pl.BlockSpec((tm, QK_W), lambda i: (i, 0)),
            pl.BlockSpec((tm, QK_W), lambda i: (i, 0)),
            pl.BlockSpec((tm, d), lambda i: (i, COL_GATE // d)),
            pl.BlockSpec((tm, d), lambda i: (i, COL_GATE // d + 1)),
            const((1, 2 * d)),
            const((QK_W, d)),
            const((QK_W, d)),
            const((d, d)),
            const((1, d)),
        ],
        out_specs=pl.BlockSpec((tm, d), lambda i: (i, 0)),
        out_shape=jax.ShapeDtypeStruct((s, d), F32),
        compiler_params=pltpu.CompilerParams(
            dimension_semantics=("arbitrary",), vmem_limit_bytes=VMEM_LIMIT),
        name="merge",
    )(x, ya, yb, proj, proj, gate_b, w_up_a, w_up_b, w_out, out_w)


def _block_diag_ones(n, blk):
    idx = np.arange(n) // blk
    return jnp.asarray(idx[:, None] == idx[None, :], dtype=BF16)


def _lower_tri_ones(n):
    idx = np.arange(n)
    return jnp.asarray(idx[None, :] <= idx[:, None], dtype=BF16)


def _expand_matrix():
    e = np.zeros((LANES, 2 * QK_W + N_HEADS * LANES), np.float32)
    for h in range(N_HEADS):
        e[SMALL_B + h, h * HEAD_DIM:(h + 1) * HEAD_DIM] = 1.0
        e[SMALL_A + h, QK_W + h * HEAD_DIM:QK_W + (h + 1) * HEAD_DIM] = 1.0
        e[SMALL_A + h, 2 * QK_W + h * LANES:2 * QK_W + (h + 1) * LANES] = 1.0
    return jnp.asarray(e, dtype=BF16)


def _pad_lanes(vec, offset):
    return jnp.zeros((1, LANES), F32).at[0, offset:offset + vec.shape[0]].set(vec.astype(F32))


def _regroup_w_in(w):
    d = w.shape[0]
    o_z = 3 * QK_W
    o_b = o_z + QK_W
    o_a = o_b + N_HEADS
    o_fq = o_a + N_HEADS
    o_ff = o_fq + 4 * QK_W
    o_gate = o_ff + N_HEADS
    small = jnp.concatenate(
        [w[:, o_b:o_a], w[:, o_a:o_fq], w[:, o_ff:o_gate],
         jnp.zeros((d, LANES - 3 * N_HEADS), w.dtype)], axis=1)
    return jnp.concatenate([w[:, 0:o_b], w[:, o_fq:o_ff], w[:, o_gate:], small], axis=1).astype(BF16)


FOX_TQ = 512
FOX_TK = 512


def kernel(x, norm_w, w_in, gate_b, conv_w, a_log, dt_bias, gdn_norm_w, f_bias,
           fox_qn_w, fox_kn_w, fox_on_w, w_up_gdn, w_up_fox, w_out, final_norm_w):
    batch, s, d = x.shape
    assert norm_w.shape[0] == 1, "the merge call fuses the final RMSNorm: single-layer trunk only"
    assert s % FOX_TQ == 0 and s % GDN_CHUNK == 0
    assert w_in.shape[2] == P_WIDTH - LANES + 3 * N_HEADS
    l = 0

    bd = _block_diag_ones(QK_W, HEAD_DIM)
    tri_gdn = _lower_tri_ones(GDN_CHUNK)
    tri_fox = _lower_tri_ones(FOX_TK)
    expand = _expand_matrix()
    tile_heads = lambda w: jnp.tile(w.astype(F32), N_HEADS)[None, :]

    outs = []
    x_rows = x.reshape(batch * s, d)
    for b in range(batch):
        xb = x_rows if batch == 1 else x_rows[b * s:(b + 1) * s]
        proj = _proj(xb, norm_w[l][None, :], _regroup_w_in(w_in.reshape(w_in.shape[1:])))
        ya = _gdn(proj, conv_w[l], _pad_lanes(a_log[l], SMALL_A), _pad_lanes(dt_bias[l], SMALL_A),
                  tile_heads(gdn_norm_w[l]), bd, tri_gdn, expand)
        qa, ka, vb, ct, bmax, crange = _foxprep(
            proj, tile_heads(fox_qn_w[l]), tile_heads(fox_kn_w[l]),
            _pad_lanes(f_bias[l], SMALL_F), bd, tri_fox, rows=FOX_TK)
        bound_max = jnp.max(bmax[0, SMALL_BOUND:SMALL_BOUND + N_HEADS])
        fast = (bound_max <= FAST_BOUND_MAX).astype(jnp.int32).reshape(1)
        first = _first_key_tile(crange, bmax, tq=FOX_TQ, tk=FOX_TK)
        yb = _fox(fast, first, qa, ka, vb, ct, proj,
                  jnp.tile(fox_on_w[l].astype(F32), 2)[None, :], tq=FOX_TQ, tk=FOX_TK)
        outs.append(_merge(xb, ya, yb, proj, gate_b[l][None, :].astype(F32),
                           w_up_gdn[l].astype(BF16), w_up_fox[l].astype(BF16),
                           w_out[l].astype(BF16), final_norm_w[None, :].astype(F32)))
    out = outs[0] if batch == 1 else jnp.concatenate(outs, axis=0)
    return out.reshape(batch, s, d)
```

```python
import functools

import numpy as np
import jax
import jax.numpy as jnp
from jax import lax
from jax.experimental import pallas as pl
from jax.experimental.pallas import tpu as pltpu

F32 = jnp.float32
BF16 = jnp.bfloat16

EPS = 1e-6
LANES = 128
HEAD_DIM = 64
N_HEADS = 8
N_PAIRS = N_HEADS // 2
QK_W = N_HEADS * HEAD_DIM
CONV_K = 4
GDN_CHUNK = 128
GDN_ROWS = 256
VMEM_LIMIT = 56 * 1024 * 1024

COL_GDN_QKV = 0
COL_GDN_Z = 1536
COL_FOX_Q = 2048
COL_FOX_K = 2560
COL_FOX_V = 3072
COL_FOX_Z = 3584
COL_GATE = 4096
COL_SMALL = 6144
P_WIDTH = 6272
SMALL_B, SMALL_A, SMALL_F = 0, 8, 16


def _dot(a, b):
    return jnp.dot(a, b, preferred_element_type=F32)


def _dot_nt(a, b):
    return lax.dot_general(a, b, (((1,), (1,)), ((), ())), preferred_element_type=F32)


def _split3(x):
    hi = x.astype(BF16)
    r = x - hi.astype(F32)
    mid = r.astype(BF16)
    lo = (r - mid.astype(F32)).astype(BF16)
    return hi, mid, lo


def _dot_sel_rhs(x, sel):
    hi, mid, lo = _split3(x)
    return _dot(hi, sel) + _dot(mid, sel) + _dot(lo, sel)


def _dot_sel_lhs(sel, x):
    hi, mid, lo = _split3(x)
    return _dot(sel, hi) + _dot(sel, mid) + _dot(sel, lo)


def _head_sumsq(x, bd):
    return _dot((x * x).astype(BF16), bd)


def _sigmoid(x):
    return 1.0 / (1.0 + jnp.exp(-x))


def _softplus(x):
    return jnp.maximum(x, 0.0) + jnp.log1p(jnp.exp(-jnp.abs(x)))


def _lane_iota(shape):
    return lax.broadcasted_iota(jnp.int32, shape, len(shape) - 1)


def _proj_kernel(x_ref, nw_ref, w_ref, o_ref, *, n_chunk):
    x = x_ref[...]
    ms = jnp.mean(x * x, axis=-1, keepdims=True)
    h = (x * lax.rsqrt(ms + EPS) * nw_ref[...]).astype(BF16)
    width = o_ref.shape[1]
    for c0 in range(0, width, n_chunk):
        c1 = min(c0 + n_chunk, width)
        o_ref[:, c0:c1] = _dot(h, w_ref[:, c0:c1])


def _proj(x, norm_w, w_all, *, tm=256):
    s, d = x.shape
    width = w_all.shape[1]
    return pl.pallas_call(
        functools.partial(_proj_kernel, n_chunk=1024),
        grid=(s // tm,),
        in_specs=[
            pl.BlockSpec((tm, d), lambda i: (i, 0)),
            pl.BlockSpec((1, d), lambda i: (0, 0)),
            pl.BlockSpec((d, width), lambda i: (0, 0)),
        ],
        out_specs=pl.BlockSpec((tm, width), lambda i: (i, 0)),
        out_shape=jax.ShapeDtypeStruct((s, width), F32),
        compiler_params=pltpu.CompilerParams(
            dimension_semantics=("arbitrary",), vmem_limit_bytes=VMEM_LIMIT),
        name="proj",
    )(x, norm_w, w_all)


INV_BASE_LOG2 = 5


def _shr(x, n):
    return lax.shift_right_logical(x, n)


def _inverse_masks(row, col):
    c = row.shape[0]
    lb = INV_BASE_LOG2
    diag = _shr(row, lb) == _shr(col, lb)
    levels = []
    b = lb
    while (1 << b) < c:
        in_pair = _shr(row, b + 1) == _shr(col, b + 1)
        levels.append(in_pair & ((_shr(row, b) & 1) == 1) & ((_shr(col, b) & 1) == 0))
        b += 1
    return diag, levels


def _unit_lower_inverses(ms, eye, masks):
    c = ms[0].shape[0]
    diag, levels = masks
    mds = [jnp.where(diag, m, 0.0) for m in ms]
    ts = [eye - md for md in mds]
    mps = [_dot(md.astype(BF16), md.astype(BF16)) for md in mds]
    n_sq = INV_BASE_LOG2 - 1
    for it in range(n_sq):
        if it < n_sq - 1:
            boths = [_dot(jnp.concatenate([t, mp], axis=0).astype(BF16), mp.astype(BF16))
                     for t, mp in zip(ts, mps)]
            ts = [t + both[0:c, :] for t, both in zip(ts, boths)]
            mps = [both[c:2 * c, :] for both in boths]
        else:
            ts = [t + _dot(t.astype(BF16), mp.astype(BF16)) for t, mp in zip(ts, mps)]
    for level in levels:
        tls = [_dot(t.astype(BF16), jnp.where(level, m, 0.0).astype(BF16)) for t, m in zip(ts, ms)]
        ts = [t - _dot(tl.astype(BF16), t.astype(BF16)) for t, tl in zip(ts, tls)]
    return ts


def _gdn_kernel(qkv_ref, z_ref, sm_ref, cw_ref, alog_ref, dtb_ref, nw_ref,
                bd_ref, tri_ref, ex_ref, o_ref, xext_ref, state_ref):
    c = GDN_CHUNK
    r = qkv_ref.shape[0]
    chunks = [slice(ci * c, (ci + 1) * c) for ci in range(r // c)]
    step = pl.program_id(0)

    @pl.when(step == 0)
    def _():
        xext_ref[0:8, :] = jnp.zeros((8, xext_ref.shape[1]), F32)
        state_ref[...] = jnp.zeros(state_ref.shape, F32)

    xext_ref[8:8 + r, :] = qkv_ref[...]
    y = cw_ref[3:4, :] * xext_ref[8:8 + r, :]
    for j in range(CONV_K - 1):
        y = y + cw_ref[j:j + 1, :] * xext_ref[5 + j:5 + j + r, :]
    xext_ref[0:8, :] = qkv_ref[r - 8:r, :]
    y = y * _sigmoid(y)

    bd = bd_ref[...]
    q = y[:, 0:QK_W]
    k = y[:, QK_W:2 * QK_W]
    v = y[:, 2 * QK_W:3 * QK_W]
    q = q * lax.rsqrt(_head_sumsq(q, bd) + EPS) * (HEAD_DIM ** -0.5)
    k = k * lax.rsqrt(_head_sumsq(k, bd) + EPS)

    sm = sm_ref[...]
    beta_s = _sigmoid(sm)
    g_s = -jnp.exp(alog_ref[...]) * _softplus(sm + dtb_ref[...])
    g_cum = _dot_sel_lhs(tri_ref[...], g_s)
    beta = _dot(beta_s.astype(BF16), ex_ref[:, 0:QK_W])
    g = _dot_sel_rhs(g_cum, ex_ref[:, QK_W:2 * QK_W])
    g_bc = _dot_sel_rhs(g_cum, ex_ref[:, 2 * QK_W:2 * QK_W + N_HEADS * LANES])
    g_t = g_cum.T

    eg = jnp.exp(g)
    kb = k * beta
    vb = v * beta
    kbg = kb * eg
    qg = q * eg
    g_last = [g[rows.stop - 1:rows.stop, :] for rows in chunks]
    e_last = [jnp.exp(gl) for gl in g_last]
    kt = [k[rows] * jnp.exp(gl - g[rows]) for rows, gl in zip(chunks, g_last)]

    row = lax.broadcasted_iota(jnp.int32, (c, c), 0)
    col = lax.broadcasted_iota(jnp.int32, (c, c), 1)
    causal = col <= row
    strict = col < row
    eye = (col == row).astype(F32)
    inv_masks = _inverse_masks(row, col)
    lane = _lane_iota((c, LANES))
    is_even = lane < HEAD_DIM
    bd_mask = (lane < HEAD_DIM) == (lax.broadcasted_iota(jnp.int32, (c, LANES), 0) < HEAD_DIM)

    pair_lanes = [slice(p * LANES, (p + 1) * LANES) for p in range(N_PAIRS)]
    odd = jnp.logical_not(is_even)

    kq = []
    for rows in chunks:
        for ls in pair_lanes:
            kb_p, q_p = kb[rows, ls], q[rows, ls]
            lhs = jnp.concatenate([jnp.where(is_even, kb_p, 0.0), jnp.where(odd, kb_p, 0.0),
                                   jnp.where(is_even, q_p, 0.0), jnp.where(odd, q_p, 0.0)], axis=0)
            kq.append(_dot_nt(lhs.astype(BF16), k[rows, ls].astype(BF16)))
    ms, a_h = [], []
    for ci, rows in enumerate(chunks):
        for h in range(N_HEADS):
            p, hh = divmod(h, 2)
            kq_p = kq[ci * N_PAIRS + p]
            diff = g_bc[rows, h * LANES:(h + 1) * LANES] - g_t[SMALL_A + h:SMALL_A + h + 1, rows]
            dec = jnp.where(causal, jnp.exp(jnp.where(causal, diff, 0.0)), 0.0)
            ms.append(jnp.where(strict, kq_p[hh * c:(hh + 1) * c, :] * dec, 0.0))
            a_h.append((kq_p[(2 + hh) * c:(3 + hh) * c, :] * dec).astype(BF16))
    ts = _unit_lower_inverses(ms, eye, inv_masks)
    wk = []
    for ci, rows in enumerate(chunks):
        rhs = [jnp.concatenate([vb[rows, ls], kbg[rows, ls]], axis=1).astype(BF16)
               for ls in pair_lanes]
        wk += [_dot(ts[ci * N_HEADS + h].astype(BF16), rhs[h // 2]) for h in range(N_HEADS)]

    sts = [state_ref[p] for p in range(N_PAIRS)]
    for ci, rows in enumerate(chunks):
        stb = [st.astype(BF16) for st in sts]
        ub = []
        for p in range(N_PAIRS):
            wk_e, wk_o = wk[ci * N_HEADS + 2 * p], wk[ci * N_HEADS + 2 * p + 1]
            w_pair = jnp.where(is_even, wk_e[:, 0:LANES], wk_o[:, 0:LANES])
            kc_pair = jnp.where(is_even, wk_e[:, LANES:2 * LANES], wk_o[:, LANES:2 * LANES])
            ub.append((w_pair - _dot(kc_pair.astype(BF16), stb[p])).astype(BF16))
        for p, ls in enumerate(pair_lanes):
            a_e, a_o = a_h[ci * N_HEADS + 2 * p], a_h[ci * N_HEADS + 2 * p + 1]
            o = _dot(qg[rows, ls].astype(BF16), stb[p]) + jnp.where(
                is_even, _dot(a_e, ub[p]), _dot(a_o, ub[p]))
            upd = _dot(kt[ci][:, ls].T.astype(BF16), ub[p])
            sts[p] = sts[p] * e_last[ci][:, ls] + jnp.where(bd_mask, upd, 0.0)
            o_ref[rows, ls] = o.astype(o_ref.dtype)
    for p in range(N_PAIRS):
        state_ref[p] = sts[p]

    o_all = o_ref[...].astype(F32)
    z = z_ref[...]
    ms = _head_sumsq(o_all, bd) * (1.0 / HEAD_DIM)
    yn = o_all * lax.rsqrt(ms + EPS) * nw_ref[...]
    o_ref[...] = (yn * (z * _sigmoid(z))).astype(o_ref.dtype)


def _gdn(proj, conv_w, a_log_pad, dtb_pad, nw_tiled, bd, tri, expand):
    s = proj.shape[0]
    c = tri.shape[0]
    conv_width = 3 * QK_W
    const = lambda shape: pl.BlockSpec(shape, lambda i: (0,) * len(shape))
    return pl.pallas_call(
        _gdn_kernel,
        grid=(s // c,),
        in_specs=[
            pl.BlockSpec((c, conv_width), lambda i: (i, COL_GDN_QKV // conv_width)),
            pl.BlockSpec((c, QK_W), lambda i: (i, COL_GDN_Z // QK_W)),
            pl.BlockSpec((c, LANES), lambda i: (i, COL_SMALL // LANES)),
            const((CONV_K, conv_width)),
            const((1, LANES)),
            const((1, LANES)),
            const((1, QK_W)),
            const((QK_W, QK_W)),
            const((c, c)),
            const(expand.shape),
        ],
        out_specs=pl.BlockSpec((c, QK_W), lambda i: (i, 0)),
        out_shape=jax.ShapeDtypeStruct((s, QK_W), F32),
        scratch_shapes=[
            pltpu.VMEM((c + 8, conv_width), F32),
            pltpu.VMEM((N_PAIRS, LANES, LANES), F32),
        ],
        compiler_params=pltpu.CompilerParams(
            dimension_semantics=("arbitrary",), vmem_limit_bytes=VMEM_LIMIT),
        name="gdn",
    )(proj, proj, proj, conv_w, a_log_pad, dtb_pad, nw_tiled, bd, tri, expand)


X_C, X_EVEN, X_ODD, X_BOUND = 0, 3, 6, 9
SMALL_BOUND = 24
N_EXTRA_COLS = (N_HEADS + N_PAIRS) * LANES


def _extras_selector():
    sel = np.zeros((3 * LANES, N_EXTRA_COLS), np.float32)
    pat = np.zeros((1, N_EXTRA_COLS), np.float32)
    k0 = N_HEADS * LANES
    for t in range(3):
        r0 = t * LANES
        for h in range(N_HEADS):
            sel[r0 + SMALL_F + h, h * LANES + X_C + t] = 1.0
            sel[r0 + SMALL_BOUND + h, h * LANES + X_BOUND + t] = 1.0
            pat[0, h * LANES + (X_EVEN if h % 2 == 0 else X_ODD) + t] = 1.0
        for p in range(N_PAIRS):
            sel[r0 + SMALL_F + 2 * p, k0 + p * LANES + X_EVEN + t] = -1.0
            sel[r0 + SMALL_F + 2 * p + 1, k0 + p * LANES + X_ODD + t] = -1.0
            pat[0, k0 + p * LANES + X_C + t] = 1.0
            pat[0, k0 + p * LANES + X_BOUND + t] = 1.0
    return jnp.asarray(sel, dtype=BF16), jnp.asarray(pat, dtype=F32)


def _sumsq_selector():
    sel = np.zeros((QK_W, LANES), np.float32)
    for h in range(N_HEADS):
        sel[h * HEAD_DIM:(h + 1) * HEAD_DIM, SMALL_BOUND + h] = 1.0
    return jnp.asarray(sel, dtype=BF16)


def _foxprep_kernel(q_ref, k_ref, v_ref, sm_ref, qw_ref, kw_ref, fb_ref, bd_ref, tri_ref, ss_ref,
                    sel_ref, pat_ref,
                    qa_ref, ka_ref, vo_ref, ct_ref, bmax_ref, crange_ref, carry_ref, kmax_ref):
    @pl.when(pl.program_id(0) == 0)
    def _():
        carry_ref[...] = jnp.zeros(carry_ref.shape, F32)
        kmax_ref[...] = jnp.zeros(kmax_ref.shape, F32)
        bmax_ref[...] = jnp.zeros(bmax_ref.shape, F32)

    bd = bd_ref[...]

    def headnorm(x, w):
        ms = _head_sumsq(x, bd) * (1.0 / HEAD_DIM)
        return x * lax.rsqrt(ms + EPS) * w

    qb = (headnorm(q_ref[...], qw_ref[...]) * (HEAD_DIM ** -0.5)).astype(BF16)
    kb = headnorm(k_ref[...], kw_ref[...]).astype(BF16)
    vo_ref[...] = v_ref[...].astype(vo_ref.dtype)
    rows = qb.shape[0]

    xf = sm_ref[...] + fb_ref[...]
    log_f = jnp.minimum(xf, 0.0) - jnp.log1p(jnp.exp(-jnp.abs(xf)))
    cum = carry_ref[...] + _dot_sel_lhs(tri_ref[...], log_f)
    carry_ref[...] = cum[rows - 1:rows, :]
    ct_ref[0] = cum.T[SMALL_F:SMALL_F + N_HEADS, :]
    crange_ref[0, 0:1, :] = jnp.max(cum, axis=0, keepdims=True)
    crange_ref[0, 1:2, :] = jnp.min(cum, axis=0, keepdims=True)

    qf = qb.astype(F32)
    kf = kb.astype(F32)
    ss = ss_ref[...]
    kss = _dot((kf * kf).astype(BF16), ss)
    kmax = jnp.maximum(kmax_ref[...], jnp.max(kss, axis=0, keepdims=True))
    kmax_ref[...] = kmax
    bound = jnp.sqrt(_dot((qf * qf).astype(BF16), ss) * kmax)
    bmax_ref[...] = jnp.maximum(bmax_ref[...], jnp.max(bound, axis=0, keepdims=True))

    lane = _lane_iota((rows, LANES))
    terms = jnp.concatenate(_split3(jnp.where(lane < SMALL_BOUND, cum, -bound)), axis=1)
    extras = (_dot(terms, sel_ref[...]) + pat_ref[...]).astype(BF16)
    is_even = lane < HEAD_DIM
    for p in range(N_PAIRS):
        ls = slice(p * LANES, (p + 1) * LANES)
        for hh in range(2):
            h = 2 * p + hh
            keep = is_even if hh == 0 else jnp.logical_not(is_even)
            qa_ref[p, hh, :, 0:LANES] = jnp.where(keep, qb[:, ls], jnp.zeros_like(qb[:, ls]))
            qa_ref[p, hh, :, LANES:2 * LANES] = extras[:, h * LANES:(h + 1) * LANES]
        ka_ref[:, 2 * p * LANES:(2 * p + 1) * LANES] = kb[:, ls]
        ka_ref[:, (2 * p + 1) * LANES:(2 * p + 2) * LANES] = extras[
            :, (N_HEADS + p) * LANES:(N_HEADS + p + 1) * LANES]


def _foxprep(proj, qw_tiled, kw_tiled, fb_pad, bd, tri, *, rows):
    s = proj.shape[0]
    const = lambda shape: pl.BlockSpec(shape, lambda i: (0,) * len(shape))
    ss_sel = _sumsq_selector()
    ex_sel, ex_pat = _extras_selector()
    return pl.pallas_call(
        _foxprep_kernel,
        grid=(s // rows,),
        in_specs=[
            pl.BlockSpec((rows, QK_W), lambda i: (i, COL_FOX_Q // QK_W)),
            pl.BlockSpec((rows, QK_W), lambda i: (i, COL_FOX_K // QK_W)),
            pl.BlockSpec((rows, QK_W), lambda i: (i, COL_FOX_V // QK_W)),
            pl.BlockSpec((rows, LANES), lambda i: (i, COL_SMALL // LANES)),
            const((1, QK_W)),
            const((1, QK_W)),
            const((1, LANES)),
            const((QK_W, QK_W)),
            const((rows, rows)),
            const(ss_sel.shape),
            const(ex_sel.shape),
            const(ex_pat.shape),
        ],
        out_specs=[
            pl.BlockSpec((N_PAIRS, 2, rows, 2 * LANES), lambda i: (0, 0, i, 0)),
            pl.BlockSpec((rows, 2 * QK_W), lambda i: (i, 0)),
            pl.BlockSpec((rows, QK_W), lambda i: (i, 0)),
            pl.BlockSpec((1, N_HEADS, rows), lambda i: (i, 0, 0)),
            pl.BlockSpec((1, LANES), lambda i: (0, 0)),
            pl.BlockSpec((1, 2, LANES), lambda i: (i, 0, 0)),
        ],
        out_shape=[
            jax.ShapeDtypeStruct((N_PAIRS, 2, s, 2 * LANES), BF16),
            jax.ShapeDtypeStruct((s, 2 * QK_W), BF16),
            jax.ShapeDtypeStruct((s, QK_W), BF16),
            jax.ShapeDtypeStruct((s // rows, N_HEADS, rows), F32),
            jax.ShapeDtypeStruct((1, LANES), F32),
            jax.ShapeDtypeStruct((s // rows, 2, LANES), F32),
        ],
        scratch_shapes=[pltpu.VMEM((1, LANES), F32), pltpu.VMEM((1, LANES), F32)],
        compiler_params=pltpu.CompilerParams(
            dimension_semantics=("arbitrary",), vmem_limit_bytes=VMEM_LIMIT),
        name="foxprep",
    )(proj, proj, proj, proj, qw_tiled, kw_tiled, fb_pad, bd, tri, ss_sel, ex_sel, ex_pat)


FAST_BOUND_MAX = 40.0
FAST_TAIL_TILES = 3


def _fox_kernel(fast_ref, first_ref, q_ref, k_ref, v_ref, c_ref, z_ref, nw_ref, o_ref, m_ref,
                acc_ref, *, tq, tk):
    p = pl.program_id(0)
    i = pl.program_id(1)
    first = first_ref[p * pl.num_programs(1) + i]
    n_sub = tq // tk

    acc_ref[...] = jnp.zeros(acc_ref.shape, F32)
    ones = jnp.ones((tk, LANES), BF16)

    def kv_tile(j):
        start = pl.multiple_of(j * tk, tk)
        vj = jnp.concatenate([v_ref[pl.ds(start, tk), :], ones], axis=1)
        return k_ref[pl.ds(start, tk), :], vj

    def causal_keep(j):
        q_pos = i * tq + lax.broadcasted_iota(jnp.int32, (tq, tk), 0)
        k_pos = j * tk + lax.broadcasted_iota(jnp.int32, (tq, tk), 1)
        return k_pos <= q_pos

    def fast_tile(j, masked):
        kj, vj = kv_tile(j)
        for hh in range(2):
            s = _dot_nt(q_ref[0, hh], kj)
            if masked:
                s = jnp.where(causal_keep(j), s, -jnp.inf)
            acc_ref[hh] += _dot(jnp.exp(s).astype(BF16), vj)

    def slow_tile(j, masked):
        kj, vj = kv_tile(j)
        for hh in range(2):
            ck = c_ref[j, pl.ds(2 * p + hh, 1), :]
            s = _dot_nt(q_ref[0, hh, :, 0:LANES], kj[:, 0:LANES]) - ck
            if masked:
                s = jnp.where(causal_keep(j), s, -jnp.inf)
            m_prev = m_ref[hh]
            m_new = jnp.maximum(m_prev, jnp.max(s, axis=-1, keepdims=True))
            alpha = jnp.exp(m_prev - m_new)
            pr = jnp.exp(s - m_new)
            acc_ref[hh] = alpha * acc_ref[hh] + _dot(pr.astype(BF16), vj)
            m_ref[hh] = m_new

    def run(tile):
        def body(j, carry):
            tile(j, False)
            return carry
        lax.fori_loop(first, i * n_sub, body, 0)
        for d in range(n_sub):
            tile(i * n_sub + d, True)

    def fast_tail(n_tiles):
        span = n_tiles * tk
        start = pl.multiple_of((i - (n_tiles - 1)) * tk, tk)
        ks = k_ref[pl.ds(start, span), :]
        vs = jnp.concatenate([v_ref[pl.ds(start, span), :], jnp.ones((span, LANES), BF16)], axis=1)
        q_rel = (n_tiles - 1) * tk + lax.broadcasted_iota(jnp.int32, (tq, span), 0)
        keep = lax.broadcasted_iota(jnp.int32, (tq, span), 1) <= q_rel
        for hh in range(2):
            s = jnp.where(keep, _dot_nt(q_ref[0, hh], ks), -jnp.inf)
            acc_ref[hh] += _dot(jnp.exp(s).astype(BF16), vs)

    def run_fast():
        assert n_sub == 1
        def body(j, carry):
            fast_tile(j, False)
            return carry
        lax.fori_loop(first, jnp.maximum(first, i - (FAST_TAIL_TILES - 1)), body, 0)
        for n_tiles in range(1, FAST_TAIL_TILES):
            pl.when(i == n_tiles - 1)(functools.partial(fast_tail, n_tiles))
        pl.when(i >= FAST_TAIL_TILES - 1)(functools.partial(fast_tail, FAST_TAIL_TILES))

    @pl.when(fast_ref[0] == 1)
    def _():
        run_fast()

    @pl.when(fast_ref[0] != 1)
    def _():
        m_ref[...] = jnp.full(m_ref.shape, -jnp.inf, F32)
        run(slow_tile)

    acc_e = acc_ref[0]
    acc_o = acc_ref[1]
    is_even = _lane_iota((tq, LANES)) < HEAD_DIM
    o = jnp.where(is_even, acc_e[:, 0:LANES] / acc_e[:, LANES:2 * LANES],
                  acc_o[:, 0:LANES] / acc_o[:, LANES:2 * LANES])
    o2 = o * o
    ss_e = jnp.sum(jnp.where(is_even, o2, 0.0), axis=-1, keepdims=True)
    ss_o = jnp.sum(jnp.where(is_even, 0.0, o2), axis=-1, keepdims=True)
    ms = jnp.where(is_even, ss_e, ss_o) * (1.0 / HEAD_DIM)
    z = z_ref[...]
    o_ref[...] = (o * lax.rsqrt(ms + EPS) * nw_ref[...] * (z * _sigmoid(z))).astype(o_ref.dtype)


SKIP_LOG = 110.0


def _first_key_tile(crange, bmax, *, tq, tk):
    n_k = crange.shape[0]
    n_sub = tq // tk
    c_max = crange[:, 0, SMALL_F:SMALL_F + N_HEADS].T
    c_min = crange[:, 1, SMALL_F:SMALL_F + N_HEADS].T
    cq_max = c_max.reshape(N_HEADS, n_k // n_sub, n_sub).max(axis=-1)
    bound = bmax[0, SMALL_BOUND:SMALL_BOUND + N_HEADS]
    gap = cq_max[:, :, None] - c_min[:, None, :] + 2.0 * bound[:, None, None]
    dead = gap <= -SKIP_LOG
    tile_idx = lax.broadcasted_iota(jnp.int32, dead.shape, 2)
    first = jnp.min(jnp.where(dead, n_k, tile_idx), axis=-1)
    first = jnp.min(first.reshape(N_PAIRS, 2, -1), axis=1)
    first = jnp.minimum(first, jnp.arange(first.shape[1], dtype=jnp.int32) * n_sub)
    return first.reshape(-1).astype(jnp.int32)


def _fox(fast, first, qa, ka, vb, ct, proj, nw_pair, *, tq, tk):
    s = vb.shape[0]
    grid_spec = pltpu.PrefetchScalarGridSpec(
        num_scalar_prefetch=2,
        grid=(N_PAIRS, s // tq),
        in_specs=[
            pl.BlockSpec((1, 2, tq, 2 * LANES), lambda p, i, *_: (p, 0, i, 0)),
            pl.BlockSpec((s, 2 * LANES), lambda p, i, *_: (0, p)),
            pl.BlockSpec((s, LANES), lambda p, i, *_: (0, p)),
            pl.BlockSpec(ct.shape, lambda p, i, *_: (0, 0, 0)),
            pl.BlockSpec((tq, LANES), lambda p, i, *_: (i, COL_FOX_Z // LANES + p)),
            pl.BlockSpec((1, LANES), lambda p, i, *_: (0, 0)),
        ],
        out_specs=pl.BlockSpec((tq, LANES), lambda p, i, *_: (i, p)),
        scratch_shapes=[
            pltpu.VMEM((2, tq, 1), F32),
            pltpu.VMEM((2, tq, 2 * LANES), F32),
        ],
    )
    return pl.pallas_call(
        functools.partial(_fox_kernel, tq=tq, tk=tk),
        grid_spec=grid_spec,
        out_shape=jax.ShapeDtypeStruct((s, QK_W), BF16),
        compiler_params=pltpu.CompilerParams(
            dimension_semantics=("arbitrary", "arbitrary"), vmem_limit_bytes=VMEM_LIMIT),
        name="fox",
    )(fast, first, qa, ka, vb, ct, proj, nw_pair)


def _merge_kernel(x_ref, ya_ref, yb_ref, ga_ref, gb_ref, gbias_ref, pa_ref, pb_ref, wo_ref,
                  fw_ref, o_ref):
    d = x_ref.shape[1]
    ua = _dot(ya_ref[...].astype(BF16), pa_ref[...])
    ub = _dot(yb_ref[...].astype(BF16), pb_ref[...])
    ga = _sigmoid(ga_ref[...] + gbias_ref[:, 0:d])
    gb = _sigmoid(gb_ref[...] + gbias_ref[:, d:2 * d])
    merged = ga * ua + gb * ub
    xn = x_ref[...] + _dot(merged.astype(BF16), wo_ref[...])
    ms = jnp.mean(xn * xn, axis=-1, keepdims=True)
    o_ref[...] = xn * lax.rsqrt(ms + EPS) * fw_ref[...]


def _merge(x, ya, yb, proj, gate_b, w_up_a, w_up_b, w_out, out_w, *, tm=512):
    s, d = x.shape
    const = lambda shape: pl.BlockSpec(shape, lambda i: (0,) * len(shape))
    return pl.pallas_call(
        _merge_kernel,
        grid=(s // tm,),
        in_specs=[
            pl.BlockSpec((tm, d), lambda i: (i, 0)),
            pl.BlockSpec((tm, QK_W), lambda i: (i, 0)),
            pl.BlockSpec((tm, QK_W), lambda i: (i, 0)),
            pl.BlockSpec((tm, d), lambda i: (i, COL_GATE // d)),
            pl.BlockSpec((tm, d), lambda i: (i, COL_GATE // d + 1)),
            const((1, 2 * d)),
            const((QK_W, d)),
            const((QK_W, d)),
            const((d, d)),
            const((1, d)),
        ],
        out_specs=pl.BlockSpec((tm, d), lambda i: (i, 0)),
        out_shape=jax.ShapeDtypeStruct((s, d), F32),
        compiler_params=pltpu.CompilerParams(
            dimension_semantics=("arbitrary",), vmem_limit_bytes=VMEM_LIMIT),
        name="merge",
    )(x, ya, yb, proj, proj, gate_b, w_up_a, w_up_b, w_out, out_w)


def _block_diag_ones(n, blk):
    idx = np.arange(n) // blk
    return jnp.asarray(idx[:, None] == idx[None, :], dtype=BF16)


def _lower_tri_ones(n, blk=None):
    idx = np.arange(n)
    tri = idx[None, :] <= idx[:, None]
    if blk is not None:
        tri = tri & (idx[None, :] // blk == idx[:, None] // blk)
    return jnp.asarray(tri, dtype=BF16)


def _expand_matrix():
    e = np.zeros((LANES, 2 * QK_W + N_HEADS * LANES), np.float32)
    for h in range(N_HEADS):
        e[SMALL_B + h, h * HEAD_DIM:(h + 1) * HEAD_DIM] = 1.0
        e[SMALL_A + h, QK_W + h * HEAD_DIM:QK_W + (h + 1) * HEAD_DIM] = 1.0
        e[SMALL_A + h, 2 * QK_W + h * LANES:2 * QK_W + (h + 1) * LANES] = 1.0
    return jnp.asarray(e, dtype=BF16)


def _pad_lanes(vec, offset):
    return jnp.zeros((1, LANES), F32).at[0, offset:offset + vec.shape[0]].set(vec.astype(F32))


def _regroup_w_in(w):
    d = w.shape[0]
    o_z = 3 * QK_W
    o_b = o_z + QK_W
    o_a = o_b + N_HEADS
    o_fq = o_a + N_HEADS
    o_ff = o_fq + 4 * QK_W
    o_gate = o_ff + N_HEADS
    small = jnp.concatenate(
        [w[:, o_b:o_a], w[:, o_a:o_fq], w[:, o_ff:o_gate],
         jnp.zeros((d, LANES - 3 * N_HEADS), w.dtype)], axis=1)
    return jnp.concatenate([w[:, 0:o_b], w[:, o_fq:o_ff], w[:, o_gate:], small], axis=1).astype(BF16)


FOX_TQ = 512
FOX_TK = 512


def kernel(x, norm_w, w_in, gate_b, conv_w, a_log, dt_bias, gdn_norm_w, f_bias,
           fox_qn_w, fox_kn_w, fox_on_w, w_up_gdn, w_up_fox, w_out, final_norm_w):
    batch, s, d = x.shape
    assert norm_w.shape[0] == 1, "the merge call fuses the final RMSNorm: single-layer trunk only"
    assert s % FOX_TQ == 0 and s % GDN_ROWS == 0 and GDN_ROWS % GDN_CHUNK == 0
    assert w_in.shape[2] == P_WIDTH - LANES + 3 * N_HEADS
    l = 0

    bd = _block_diag_ones(QK_W, HEAD_DIM)
    tri_gdn = _lower_tri_ones(GDN_ROWS, GDN_CHUNK)
    tri_fox = _lower_tri_ones(FOX_TK)
    expand = _expand_matrix()
    tile_heads = lambda w: jnp.tile(w.astype(F32), N_HEADS)[None, :]

    outs = []
    x_rows = x.reshape(batch * s, d)
    for b in range(batch):
        xb = x_rows if batch == 1 else x_rows[b * s:(b + 1) * s]
        proj = _proj(xb, norm_w[l][None, :], _regroup_w_in(w_in.reshape(w_in.shape[1:])))
        ya = _gdn(proj, conv_w[l], _pad_lanes(a_log[l], SMALL_A), _pad_lanes(dt_bias[l], SMALL_A),
                  tile_heads(gdn_norm_w[l]), bd, tri_gdn, expand)
        qa, ka, vb, ct, bmax, crange = _foxprep(
            proj, tile_heads(fox_qn_w[l]), tile_heads(fox_kn_w[l]),
            _pad_lanes(f_bias[l], SMALL_F), bd, tri_fox, rows=FOX_TK)
        bound_max = jnp.max(bmax[0, SMALL_BOUND:SMALL_BOUND + N_HEADS])
        fast = (bound_max <= FAST_BOUND_MAX).astype(jnp.int32).reshape(1)
        first = _first_key_tile(crange, bmax, tq=FOX_TQ, tk=FOX_TK)
        yb = _fox(fast, first, qa, ka, vb, ct, proj,
                  jnp.tile(fox_on_w[l].astype(F32), 2)[None, :], tq=FOX_TQ, tk=FOX_TK)
        outs.append(_merge(xb, ya, yb, proj, gate_b[l][None, :].astype(F32),
                           w_up_gdn[l].astype(BF16), w_up_fox[l].astype(BF16),
                           w_out[l].astype(BF16), final_norm_w[None, :].astype(F32)))
    out = outs[0] if batch == 1 else jnp.concatenate(outs, axis=0)
    return out.reshape(batch, s, d)
```

```python
import functools

import numpy as np
import jax
import jax.numpy as jnp
from jax import lax
from jax.experimental import pallas as pl
from jax.experimental.pallas import tpu as pltpu

F32 = jnp.float32
BF16 = jnp.bfloat16

EPS = 1e-6
LANES = 128
HEAD_DIM = 64
N_HEADS = 8
N_PAIRS = N_HEADS // 2
QK_W = N_HEADS * HEAD_DIM
CONV_K = 4
GDN_CHUNK = 128
GDN_ROWS = 256
FOX_TQ = 512
FOX_TK = 512
FOX_TF = 256
VMEM_LIMIT = 56 * 1024 * 1024

COL_GDN_QKV = 0
COL_GDN_Z = 1536
COL_FOX_Q = 2048
COL_FOX_K = 2560
COL_FOX_V = 3072
COL_FOX_Z = 3584
COL_GATE = 4096
COL_SMALL = 6144
P_WIDTH = 6272
SMALL_B, SMALL_A, SMALL_F = 0, 8, 16


def _dot(a, b):
    return jnp.dot(a, b, preferred_element_type=F32)


def _dot_nt(a, b):
    return lax.dot_general(a, b, (((1,), (1,)), ((), ())), preferred_element_type=F32)


def _split3(x):
    hi = x.astype(BF16)
    r = x - hi.astype(F32)
    mid = r.astype(BF16)
    lo = (r - mid.astype(F32)).astype(BF16)
    return hi, mid, lo


def _dot_sel_rhs(x, sel):
    hi, mid, lo = _split3(x)
    return _dot(hi, sel) + _dot(mid, sel) + _dot(lo, sel)


def _dot_sel_lhs(sel, x):
    hi, mid, lo = _split3(x)
    return _dot(sel, hi) + _dot(sel, mid) + _dot(sel, lo)


def _head_sumsq(x, bd):
    return _dot((x * x).astype(BF16), bd)


def _sigmoid(x):
    return 1.0 / (1.0 + jnp.exp(-x))


def _softplus(x):
    return jnp.maximum(x, 0.0) + jnp.log1p(jnp.exp(-jnp.abs(x)))


def _lane_iota(shape):
    return lax.broadcasted_iota(jnp.int32, shape, len(shape) - 1)


def _proj_kernel(x_ref, nw_ref, w_ref, o_ref, *, n_chunk):
    x = x_ref[...]
    ms = jnp.mean(x * x, axis=-1, keepdims=True)
    h = (x * lax.rsqrt(ms + EPS) * nw_ref[...]).astype(BF16)
    width = o_ref.shape[1]
    for c0 in range(0, width, n_chunk):
        c1 = min(c0 + n_chunk, width)
        o_ref[:, c0:c1] = _dot(h, w_ref[:, c0:c1])


def _proj(x, norm_w, w_all, *, tm=256):
    s, d = x.shape
    width = w_all.shape[1]
    return pl.pallas_call(
        functools.partial(_proj_kernel, n_chunk=1024),
        grid=(s // tm,),
        in_specs=[
            pl.BlockSpec((tm, d), lambda i: (i, 0)),
            pl.BlockSpec((1, d), lambda i: (0, 0)),
            pl.BlockSpec((d, width), lambda i: (0, 0)),
        ],
        out_specs=pl.BlockSpec((tm, width), lambda i: (i, 0)),
        out_shape=jax.ShapeDtypeStruct((s, width), F32),
        compiler_params=pltpu.CompilerParams(
            dimension_semantics=("arbitrary",), vmem_limit_bytes=VMEM_LIMIT),
        name="proj",
    )(x, norm_w, w_all)


INV_BASE_LOG2 = 5


def _shr(x, n):
    return lax.shift_right_logical(x, n)


def _inverse_masks(row, col):
    c = row.shape[0]
    lb = INV_BASE_LOG2
    diag = _shr(row, lb) == _shr(col, lb)
    levels = []
    b = lb
    while (1 << b) < c:
        in_pair = _shr(row, b + 1) == _shr(col, b + 1)
        levels.append(in_pair & ((_shr(row, b) & 1) == 1) & ((_shr(col, b) & 1) == 0))
        b += 1
    return diag, levels


def _unit_lower_inverses(ms, eye, masks):
    c = ms[0].shape[0]
    diag, levels = masks
    mds = [jnp.where(diag, m, 0.0) for m in ms]
    ts = [eye - md for md in mds]
    mps = [_dot(md.astype(BF16), md.astype(BF16)) for md in mds]
    n_sq = INV_BASE_LOG2 - 1
    for it in range(n_sq):
        if it < n_sq - 1:
            boths = [_dot(jnp.concatenate([t, mp], axis=0).astype(BF16), mp.astype(BF16))
                     for t, mp in zip(ts, mps)]
            ts = [t + both[0:c, :] for t, both in zip(ts, boths)]
            mps = [both[c:2 * c, :] for both in boths]
        else:
            ts = [t + _dot(t.astype(BF16), mp.astype(BF16)) for t, mp in zip(ts, mps)]
    b = 1 << INV_BASE_LOG2
    for level in levels:
        odd_rows = lambda x: jnp.concatenate([x[s:s + b, :] for s in range(b, c, 2 * b)], axis=0)
        t_odd = [odd_rows(t) for t in ts]
        tls = [_dot(to.astype(BF16), jnp.where(level, m, 0.0).astype(BF16))
               for to, m in zip(t_odd, ms)]
        new_odd = [to - _dot(tl.astype(BF16), t.astype(BF16)) for to, tl, t in zip(t_odd, tls, ts)]
        ts = [jnp.concatenate(
            [no[(blk // 2) * b:(blk // 2 + 1) * b, :] if blk % 2 else t[blk * b:(blk + 1) * b, :]
             for blk in range(c // b)], axis=0) for t, no in zip(ts, new_odd)]
        b *= 2
    return ts


def _gdn_kernel(qkv_ref, z_ref, sm_ref, cw_ref, alog_ref, dtb_ref, nw_ref,
                bd_ref, tri_ref, ex_ref, o_ref, xext_ref, state_ref):
    c = GDN_CHUNK
    r = qkv_ref.shape[0]
    chunks = [slice(ci * c, (ci + 1) * c) for ci in range(r // c)]
    step = pl.program_id(0)

    @pl.when(step == 0)
    def _():
        xext_ref[0:8, :] = jnp.zeros((8, xext_ref.shape[1]), F32)
        state_ref[...] = jnp.zeros(state_ref.shape, F32)

    xext_ref[8:8 + r, :] = qkv_ref[...]
    y = cw_ref[3:4, :] * xext_ref[8:8 + r, :]
    for j in range(CONV_K - 1):
        y = y + cw_ref[j:j + 1, :] * xext_ref[5 + j:5 + j + r, :]
    xext_ref[0:8, :] = qkv_ref[r - 8:r, :]
    y = y * _sigmoid(y)

    bd = bd_ref[...]
    q = y[:, 0:QK_W]
    k = y[:, QK_W:2 * QK_W]
    v = y[:, 2 * QK_W:3 * QK_W]
    q = q * lax.rsqrt(_head_sumsq(q, bd) + EPS) * (HEAD_DIM ** -0.5)
    k = k * lax.rsqrt(_head_sumsq(k, bd) + EPS)

    sm = sm_ref[...]
    beta_s = _sigmoid(sm)
    g_s = -jnp.exp(alog_ref[...]) * _softplus(sm + dtb_ref[...])
    g_cum = _dot_sel_lhs(tri_ref[...], g_s)
    beta = _dot(beta_s.astype(BF16), ex_ref[:, 0:QK_W])
    g_bc = _dot_sel_rhs(g_cum, ex_ref[:, QK_W:QK_W + N_HEADS * LANES])
    even_r = _lane_iota((r, LANES)) < HEAD_DIM
    g = jnp.concatenate(
        [jnp.where(even_r, g_bc[:, 2 * p * LANES:(2 * p + 1) * LANES],
                   g_bc[:, (2 * p + 1) * LANES:(2 * p + 2) * LANES]) for p in range(N_PAIRS)], axis=1)
    g_t = g_cum.T

    eg = jnp.exp(g)
    kb = k * beta
    vb = v * beta
    kbg = kb * eg
    qg = q * eg
    g_last = [g[rows.stop - 1:rows.stop, :] for rows in chunks]
    e_last = [jnp.exp(gl) for gl in g_last]
    kt = [k[rows] * jnp.exp(gl - g[rows]) for rows, gl in zip(chunks, g_last)]

    row = lax.broadcasted_iota(jnp.int32, (c, c), 0)
    col = lax.broadcasted_iota(jnp.int32, (c, c), 1)
    causal = col <= row
    strict = col < row
    eye = (col == row).astype(F32)
    inv_masks = _inverse_masks(row, col)
    lane = _lane_iota((c, LANES))
    is_even = lane < HEAD_DIM
    bd_mask = (lane < HEAD_DIM) == (lax.broadcasted_iota(jnp.int32, (c, LANES), 0) < HEAD_DIM)

    pair_lanes = [slice(p * LANES, (p + 1) * LANES) for p in range(N_PAIRS)]
    odd = jnp.logical_not(is_even)

    kq = []
    for rows in chunks:
        for ls in pair_lanes:
            kb_p, q_p = kb[rows, ls], q[rows, ls]
            lhs = jnp.concatenate([jnp.where(is_even, kb_p, 0.0), jnp.where(odd, kb_p, 0.0),
                                   jnp.where(is_even, q_p, 0.0), jnp.where(odd, q_p, 0.0)], axis=0)
            kq.append(_dot_nt(lhs.astype(BF16), k[rows, ls].astype(BF16)))
    ms, a_h = [], []
    for ci, rows in enumerate(chunks):
        for h in range(N_HEADS):
            p, hh = divmod(h, 2)
            kq_p = kq[ci * N_PAIRS + p]
            diff = g_bc[rows, h * LANES:(h + 1) * LANES] - g_t[SMALL_A + h:SMALL_A + h + 1, rows]
            dec = jnp.where(causal, jnp.exp(jnp.where(causal, diff, 0.0)), 0.0)
            ms.append(jnp.where(strict, kq_p[hh * c:(hh + 1) * c, :] * dec, 0.0))
            a_h.append((kq_p[(2 + hh) * c:(3 + hh) * c, :] * dec).astype(BF16))
    ts = _unit_lower_inverses(ms, eye, inv_masks)
    wk = []
    for ci, rows in enumerate(chunks):
        rhs = [jnp.concatenate([vb[rows, ls], kbg[rows, ls]], axis=1).astype(BF16)
               for ls in pair_lanes]
        wk += [_dot(ts[ci * N_HEADS + h].astype(BF16), rhs[h // 2]) for h in range(N_HEADS)]

    sts = [state_ref[p] for p in range(N_PAIRS)]
    for ci, rows in enumerate(chunks):
        stb = [st.astype(BF16) for st in sts]
        ub = []
        for p in range(N_PAIRS):
            wk_e, wk_o = wk[ci * N_HEADS + 2 * p], wk[ci * N_HEADS + 2 * p + 1]
            w_pair = jnp.where(is_even, wk_e[:, 0:LANES], wk_o[:, 0:LANES])
            kc_pair = jnp.where(is_even, wk_e[:, LANES:2 * LANES], wk_o[:, LANES:2 * LANES])
            ub.append((w_pair - _dot(kc_pair.astype(BF16), stb[p])).astype(BF16))
        for p, ls in enumerate(pair_lanes):
            a_e, a_o = a_h[ci * N_HEADS + 2 * p], a_h[ci * N_HEADS + 2 * p + 1]
            o = _dot(qg[rows, ls].astype(BF16), stb[p]) + jnp.where(
                is_even, _dot(a_e, ub[p]), _dot(a_o, ub[p]))
            upd = _dot(kt[ci][:, ls].T.astype(BF16), ub[p])
            sts[p] = sts[p] * e_last[ci][:, ls] + jnp.where(bd_mask, upd, 0.0)
            o_ref[rows, ls] = o.astype(o_ref.dtype)
    for p in range(N_PAIRS):
        state_ref[p] = sts[p]

    o_all = o_ref[...].astype(F32)
    z = z_ref[...]
    ms = _head_sumsq(o_all, bd) * (1.0 / HEAD_DIM)
    yn = o_all * lax.rsqrt(ms + EPS) * nw_ref[...]
    o_ref[...] = (yn * (z * _sigmoid(z))).astype(o_ref.dtype)


def _gdn(proj, conv_w, a_log_pad, dtb_pad, nw_tiled, bd, tri, expand):
    s = proj.shape[0]
    c = tri.shape[0]
    conv_width = 3 * QK_W
    const = lambda shape: pl.BlockSpec(shape, lambda i: (0,) * len(shape))
    return pl.pallas_call(
        _gdn_kernel,
        grid=(s // c,),
        in_specs=[
            pl.BlockSpec((c, conv_width), lambda i: (i, COL_GDN_QKV // conv_width)),
            pl.BlockSpec((c, QK_W), lambda i: (i, COL_GDN_Z // QK_W)),
            pl.BlockSpec((c, LANES), lambda i: (i, COL_SMALL // LANES)),
            const((CONV_K, conv_width)),
            const((1, LANES)),
            const((1, LANES)),
            const((1, QK_W)),
            const((QK_W, QK_W)),
            const((c, c)),
            const(expand.shape),
        ],
        out_specs=pl.BlockSpec((c, QK_W), lambda i: (i, 0)),
        out_shape=jax.ShapeDtypeStruct((s, QK_W), F32),
        scratch_shapes=[
            pltpu.VMEM((c + 8, conv_width), F32),
            pltpu.VMEM((N_PAIRS, LANES, LANES), F32),
        ],
        compiler_params=pltpu.CompilerParams(
            dimension_semantics=("arbitrary",), vmem_limit_bytes=VMEM_LIMIT),
        name="gdn",
    )(proj, proj, proj, conv_w, a_log_pad, dtb_pad, nw_tiled, bd, tri, expand)


X_C, X_EVEN, X_ODD, X_BOUND = 0, 3, 6, 9
SMALL_BOUND = 24
N_EXTRA_COLS = (N_HEADS + N_PAIRS) * LANES


def _extras_selector():
    sel = np.zeros((3 * LANES, N_EXTRA_COLS), np.float32)
    pat = np.zeros((1, N_EXTRA_COLS), np.float32)
    k0 = N_HEADS * LANES
    for t in range(3):
        r0 = t * LANES
        for h in range(N_HEADS):
            sel[r0 + SMALL_F + h, h * LANES + X_C + t] = 1.0
            sel[r0 + SMALL_BOUND + h, h * LANES + X_BOUND + t] = 1.0
            pat[0, h * LANES + (X_EVEN if h % 2 == 0 else X_ODD) + t] = 1.0
        for p in range(N_PAIRS):
            sel[r0 + SMALL_F + 2 * p, k0 + p * LANES + X_EVEN + t] = -1.0
            sel[r0 + SMALL_F + 2 * p + 1, k0 + p * LANES + X_ODD + t] = -1.0
            pat[0, k0 + p * LANES + X_C + t] = 1.0
            pat[0, k0 + p * LANES + X_BOUND + t] = 1.0
    return jnp.asarray(sel, dtype=BF16), jnp.asarray(pat, dtype=F32)


def _sumsq_selector():
    sel = np.zeros((QK_W, LANES), np.float32)
    for h in range(N_HEADS):
        sel[h * HEAD_DIM:(h + 1) * HEAD_DIM, SMALL_BOUND + h] = 1.0
    return jnp.asarray(sel, dtype=BF16)


def _foxprep_kernel(q_ref, k_ref, v_ref, sm_ref, qw_ref, kw_ref, fb_ref, bd_ref, tri_ref, ss_ref,
                    sel_ref, pat_ref,
                    qa_ref, ka_ref, vo_ref, ct_ref, bmax_ref, crange_ref, carry_ref, kmax_ref):
    @pl.when(pl.program_id(0) == 0)
    def _():
        carry_ref[...] = jnp.zeros(carry_ref.shape, F32)
        kmax_ref[...] = jnp.zeros(kmax_ref.shape, F32)
        bmax_ref[...] = jnp.zeros(bmax_ref.shape, F32)

    bd = bd_ref[...]

    def headnorm(x, w):
        ms = _head_sumsq(x, bd) * (1.0 / HEAD_DIM)
        return x * lax.rsqrt(ms + EPS) * w

    qb = (headnorm(q_ref[...], qw_ref[...]) * (HEAD_DIM ** -0.5)).astype(BF16)
    kb = headnorm(k_ref[...], kw_ref[...]).astype(BF16)
    vo_ref[...] = v_ref[...].astype(vo_ref.dtype)
    rows = qb.shape[0]

    xf = sm_ref[...] + fb_ref[...]
    log_f = jnp.minimum(xf, 0.0) - jnp.log1p(jnp.exp(-jnp.abs(xf)))
    cum = carry_ref[...] + _dot_sel_lhs(tri_ref[...], log_f)
    carry_ref[...] = cum[rows - 1:rows, :]
    ct_ref[0] = cum.T[SMALL_F:SMALL_F + N_HEADS, :]
    for f in range(rows // FOX_TF):
        seg = cum[f * FOX_TF:(f + 1) * FOX_TF, :]
        crange_ref[0, 2 * f:2 * f + 1, :] = jnp.max(seg, axis=0, keepdims=True)
        crange_ref[0, 2 * f + 1:2 * f + 2, :] = jnp.min(seg, axis=0, keepdims=True)

    qf = qb.astype(F32)
    kf = kb.astype(F32)
    ss = ss_ref[...]
    kss = _dot((kf * kf).astype(BF16), ss)
    kmax = jnp.maximum(kmax_ref[...], jnp.max(kss, axis=0, keepdims=True))
    kmax_ref[...] = kmax
    bound = jnp.sqrt(_dot((qf * qf).astype(BF16), ss) * kmax)
    bmax_ref[...] = jnp.maximum(bmax_ref[...], jnp.max(bound, axis=0, keepdims=True))

    lane = _lane_iota((rows, LANES))
    terms = jnp.concatenate(_split3(jnp.where(lane < SMALL_BOUND, cum, -bound)), axis=1)
    extras = (_dot(terms, sel_ref[...]) + pat_ref[...]).astype(BF16)
    is_even = lane < HEAD_DIM
    for p in range(N_PAIRS):
        ls = slice(p * LANES, (p + 1) * LANES)
        for hh in range(2):
            h = 2 * p + hh
            keep = is_even if hh == 0 else jnp.logical_not(is_even)
            qa_ref[p, hh, :, 0:LANES] = jnp.where(keep, qb[:, ls], jnp.zeros_like(qb[:, ls]))
            qa_ref[p, hh, :, LANES:2 * LANES] = extras[:, h * LANES:(h + 1) * LANES]
        ka_ref[:, 2 * p * LANES:(2 * p + 1) * LANES] = kb[:, ls]
        ka_ref[:, (2 * p + 1) * LANES:(2 * p + 2) * LANES] = extras[
            :, (N_HEADS + p) * LANES:(N_HEADS + p + 1) * LANES]


def _foxprep(proj, qw_tiled, kw_tiled, fb_pad, bd, tri, *, rows):
    s = proj.shape[0]
    const = lambda shape: pl.BlockSpec(shape, lambda i: (0,) * len(shape))
    ss_sel = _sumsq_selector()
    ex_sel, ex_pat = _extras_selector()
    return pl.pallas_call(
        _foxprep_kernel,
        grid=(s // rows,),
        in_specs=[
            pl.BlockSpec((rows, QK_W), lambda i: (i, COL_FOX_Q // QK_W)),
            pl.BlockSpec((rows, QK_W), lambda i: (i, COL_FOX_K // QK_W)),
            pl.BlockSpec((rows, QK_W), lambda i: (i, COL_FOX_V // QK_W)),
            pl.BlockSpec((rows, LANES), lambda i: (i, COL_SMALL // LANES)),
            const((1, QK_W)),
            const((1, QK_W)),
            const((1, LANES)),
            const((QK_W, QK_W)),
            const((rows, rows)),
            const(ss_sel.shape),
            const(ex_sel.shape),
            const(ex_pat.shape),
        ],
        out_specs=[
            pl.BlockSpec((N_PAIRS, 2, rows, 2 * LANES), lambda i: (0, 0, i, 0)),
            pl.BlockSpec((rows, 2 * QK_W), lambda i: (i, 0)),
            pl.BlockSpec((rows, QK_W), lambda i: (i, 0)),
            pl.BlockSpec((1, N_HEADS, rows), lambda i: (i, 0, 0)),
            pl.BlockSpec((1, LANES), lambda i: (0, 0)),
            pl.BlockSpec((1, 2 * (rows // FOX_TF), LANES), lambda i: (i, 0, 0)),
        ],
        out_shape=[
            jax.ShapeDtypeStruct((N_PAIRS, 2, s, 2 * LANES), BF16),
            jax.ShapeDtypeStruct((s, 2 * QK_W), BF16),
            jax.ShapeDtypeStruct((s, QK_W), BF16),
            jax.ShapeDtypeStruct((s // rows, N_HEADS, rows), F32),
            jax.ShapeDtypeStruct((1, LANES), F32),
            jax.ShapeDtypeStruct((s // rows, 2 * (rows // FOX_TF), LANES), F32),
        ],
        scratch_shapes=[pltpu.VMEM((1, LANES), F32), pltpu.VMEM((1, LANES), F32)],
        compiler_params=pltpu.CompilerParams(
            dimension_semantics=("arbitrary",), vmem_limit_bytes=VMEM_LIMIT),
        name="foxprep",
    )(proj, proj, proj, proj, qw_tiled, kw_tiled, fb_pad, bd, tri, ss_sel, ex_sel, ex_pat)


FAST_BOUND_MAX = 40.0
FAST_TAIL_BACK = 3


def _fox_kernel(fast_ref, first_ref, q_ref, k_ref, v_ref, c_ref, z_ref, nw_ref, o_ref, m_ref,
                acc_ref, *, tq, tk, tf):
    p = pl.program_id(0)
    i = pl.program_id(1)
    first = first_ref[p * pl.num_programs(1) + i]
    n_sub = tq // tk
    n_diag = tq // tf

    acc_ref[...] = jnp.zeros(acc_ref.shape, F32)

    def kv_rows(start, size):
        start = pl.multiple_of(start, tf)
        vj = jnp.concatenate([v_ref[pl.ds(start, size), :], jnp.ones((size, LANES), BF16)], axis=1)
        return k_ref[pl.ds(start, size), :], vj

    def causal_keep(j):
        q_pos = i * tq + lax.broadcasted_iota(jnp.int32, (tq, tk), 0)
        k_pos = j * tk + lax.broadcasted_iota(jnp.int32, (tq, tk), 1)
        return k_pos <= q_pos

    def fast_tile(j):
        kj, vj = kv_rows(j * tf, tf)
        for hh in range(2):
            s = _dot_nt(q_ref[0, hh], kj)
            acc_ref[hh] += _dot(jnp.exp(s).astype(BF16), vj)

    def slow_tile(j, masked):
        kj, vj = kv_rows(j * tk, tk)
        for hh in range(2):
            ck = c_ref[j, pl.ds(2 * p + hh, 1), :]
            s = _dot_nt(q_ref[0, hh, :, 0:LANES], kj[:, 0:LANES]) - ck
            if masked:
                s = jnp.where(causal_keep(j), s, -jnp.inf)
            m_prev = m_ref[hh]
            m_new = jnp.maximum(m_prev, jnp.max(s, axis=-1, keepdims=True))
            alpha = jnp.exp(m_prev - m_new)
            pr = jnp.exp(s - m_new)
            acc_ref[hh] = alpha * acc_ref[hh] + _dot(pr.astype(BF16), vj)
            m_ref[hh] = m_new

    def run(tile):
        def body(j, carry):
            tile(j, False)
            return carry
        lax.fori_loop(first // (tk // tf), i * n_sub, body, 0)
        for d in range(n_sub):
            tile(i * n_sub + d, True)

    def fast_tail(n_back):
        span = n_back * tf + tq
        ks, vs = kv_rows(i * tq - n_back * tf, span)
        q_rel = n_back * tf + lax.broadcasted_iota(jnp.int32, (tq, span), 0)
        keep = lax.broadcasted_iota(jnp.int32, (tq, span), 1) <= q_rel
        for hh in range(2):
            s = jnp.where(keep, _dot_nt(q_ref[0, hh], ks), -jnp.inf)
            acc_ref[hh] += _dot(jnp.exp(s).astype(BF16), vs)

    def run_fast():
        def body(j, carry):
            fast_tile(j)
            return carry
        lax.fori_loop(first, jnp.maximum(first, i * n_diag - FAST_TAIL_BACK), body, 0)
        n_short = -(-FAST_TAIL_BACK // n_diag)
        for i_short in range(n_short):
            pl.when(i == i_short)(functools.partial(fast_tail, i_short * n_diag))
        pl.when(i >= n_short)(functools.partial(fast_tail, FAST_TAIL_BACK))

    @pl.when(fast_ref[0] == 1)
    def _():
        run_fast()

    @pl.when(fast_ref[0] != 1)
    def _():
        m_ref[...] = jnp.full(m_ref.shape, -jnp.inf, F32)
        run(slow_tile)

    acc_e = acc_ref[0]
    acc_o = acc_ref[1]
    is_even = _lane_iota((tq, LANES)) < HEAD_DIM
    o = jnp.where(is_even, acc_e[:, 0:LANES] / acc_e[:, LANES:2 * LANES],
                  acc_o[:, 0:LANES] / acc_o[:, LANES:2 * LANES])
    o2 = o * o
    ss_e = jnp.sum(jnp.where(is_even, o2, 0.0), axis=-1, keepdims=True)
    ss_o = jnp.sum(jnp.where(is_even, 0.0, o2), axis=-1, keepdims=True)
    ms = jnp.where(is_even, ss_e, ss_o) * (1.0 / HEAD_DIM)
    z = z_ref[...]
    o_ref[...] = (o * lax.rsqrt(ms + EPS) * nw_ref[...] * (z * _sigmoid(z))).astype(o_ref.dtype)


SKIP_LOG = 110.0


def _first_key_tile(crange, bmax, *, tq, tk):
    n_k = crange.shape[0]
    n_sub = tq // tk
    c_max = crange[:, 0, SMALL_F:SMALL_F + N_HEADS].T
    c_min = crange[:, 1, SMALL_F:SMALL_F + N_HEADS].T
    cq_max = c_max.reshape(N_HEADS, n_k // n_sub, n_sub).max(axis=-1)
    bound = bmax[0, SMALL_BOUND:SMALL_BOUND + N_HEADS]
    gap = cq_max[:, :, None] - c_min[:, None, :] + 2.0 * bound[:, None, None]
    dead = gap <= -SKIP_LOG
    tile_idx = lax.broadcasted_iota(jnp.int32, dead.shape, 2)
    first = jnp.min(jnp.where(dead, n_k, tile_idx), axis=-1)
    first = jnp.min(first.reshape(N_PAIRS, 2, -1), axis=1)
    first = jnp.minimum(first, jnp.arange(first.shape[1], dtype=jnp.int32) * n_sub)
    return first.reshape(-1).astype(jnp.int32)


def _fox(fast, first, qa, ka, vb, ct, proj, nw_pair, *, tq, tk):
    s = vb.shape[0]
    grid_spec = pltpu.PrefetchScalarGridSpec(
        num_scalar_prefetch=2,
        grid=(N_PAIRS, s // tq),
        in_specs=[
            pl.BlockSpec((1, 2, tq, 2 * LANES), lambda p, i, *_: (p, 0, i, 0)),
            pl.BlockSpec((s, 2 * LANES), lambda p, i, *_: (0, p)),
            pl.BlockSpec((s, LANES), lambda p, i, *_: (0, p)),
            pl.BlockSpec(ct.shape, lambda p, i, *_: (0, 0, 0)),
            pl.BlockSpec((tq, LANES), lambda p, i, *_: (i, COL_FOX_Z // LANES + p)),
            pl.BlockSpec((1, LANES), lambda p, i, *_: (0, 0)),
        ],
        out_specs=pl.BlockSpec((tq, LANES), lambda p, i, *_: (i, p)),
        scratch_shapes=[
            pltpu.VMEM((2, tq, 1), F32),
            pltpu.VMEM((2, tq, 2 * LANES), F32),
        ],
    )
    return pl.pallas_call(
        functools.partial(_fox_kernel, tq=tq, tk=tk, tf=FOX_TF),
        grid_spec=grid_spec,
        out_shape=jax.ShapeDtypeStruct((s, QK_W), BF16),
        compiler_params=pltpu.CompilerParams(
            dimension_semantics=("arbitrary", "arbitrary"), vmem_limit_bytes=VMEM_LIMIT),
        name="fox",
    )(fast, first, qa, ka, vb, ct, proj, nw_pair)


def _merge_kernel(x_ref, ya_ref, yb_ref, ga_ref, gb_ref, gbias_ref, pa_ref, pb_ref, wo_ref,
                  fw_ref, o_ref):
    d = x_ref.shape[1]
    ua = _dot(ya_ref[...].astype(BF16), pa_ref[...])
    ub = _dot(yb_ref[...].astype(BF16), pb_ref[...])
    ga = _sigmoid(ga_ref[...] + gbias_ref[:, 0:d])
    gb = _sigmoid(gb_ref[...] + gbias_ref[:, d:2 * d])
    merged = ga * ua + gb * ub
    xn = x_ref[...] + _dot(merged.astype(BF16), wo_ref[...])
    ms = jnp.mean(xn * xn, axis=-1, keepdims=True)
    o_ref[...] = xn * lax.rsqrt(ms + EPS) * fw_ref[...]


def _merge(x, ya, yb, proj, gate_b, w_up_a, w_up_b, w_out, out_w, *, tm=512):
    s, d = x.shape
    const = lambda shape: pl.BlockSpec(shape, lambda i: (0,) * len(shape))
    return pl.pallas_call(
        _merge_kernel,
        grid=(s // tm,),
        in_specs=[
            pl.BlockSpec((tm, d), lambda i: (i, 0)),
            pl.BlockSpec((tm, QK_W), lambda i: (i, 0)),
            pl.BlockSpec((tm, QK_W), lambda i: (i, 0)),
            pl.BlockSpec((tm, d), lambda i: (i, COL_GATE // d)),
            pl.BlockSpec((tm, d), lambda i: (i, COL_GATE // d + 1)),
            const((1, 2 * d)),
            const((QK_W, d)),
            const((QK_W, d)),
            const((d, d)),
            const((1, d)),
        ],
        out_specs=pl.BlockSpec((tm, d), lambda i: (i, 0)),
        out_shape=jax.ShapeDtypeStruct((s, d), F32),
        compiler_params=pltpu.CompilerParams(
            dimension_semantics=("arbitrary",), vmem_limit_bytes=VMEM_LIMIT),
        name="merge",
    )(x, ya, yb, proj, proj, gate_b, w_up_a, w_up_b, w_out, out_w)


def _block_diag_ones(n, blk):
    idx = np.arange(n) // blk
    return jnp.asarray(idx[:, None] == idx[None, :], dtype=BF16)


def _lower_tri_ones(n, blk=None):
    idx = np.arange(n)
    tri = idx[None, :] <= idx[:, None]
    if blk is not None:
        tri = tri & (idx[None, :] // blk == idx[:, None] // blk)
    return jnp.asarray(tri, dtype=BF16)


def _expand_matrix():
    e = np.zeros((LANES, QK_W + N_HEADS * LANES), np.float32)
    for h in range(N_HEADS):
        e[SMALL_B + h, h * HEAD_DIM:(h + 1) * HEAD_DIM] = 1.0
        e[SMALL_A + h, QK_W + h * LANES:QK_W + (h + 1) * LANES] = 1.0
    return jnp.asarray(e, dtype=BF16)


def _pad_lanes(vec, offset):
    return jnp.zeros((1, LANES), F32).at[0, offset:offset + vec.shape[0]].set(vec.astype(F32))


def _regroup_w_in(w):
    d = w.shape[0]
    o_z = 3 * QK_W
    o_b = o_z + QK_W
    o_a = o_b + N_HEADS
    o_fq = o_a + N_HEADS
    o_ff = o_fq + 4 * QK_W
    o_gate = o_ff + N_HEADS
    small = jnp.concatenate(
        [w[:, o_b:o_a], w[:, o_a:o_fq], w[:, o_ff:o_gate],
         jnp.zeros((d, LANES - 3 * N_HEADS), w.dtype)], axis=1)
    return jnp.concatenate([w[:, 0:o_b], w[:, o_fq:o_ff], w[:, o_gate:], small], axis=1).astype(BF16)


def kernel(x, norm_w, w_in, gate_b, conv_w, a_log, dt_bias, gdn_norm_w, f_bias,
           fox_qn_w, fox_kn_w, fox_on_w, w_up_gdn, w_up_fox, w_out, final_norm_w):
    batch, s, d = x.shape
    assert norm_w.shape[0] == 1, "the merge call fuses the final RMSNorm: single-layer trunk only"
    assert s % FOX_TQ == 0 and s % GDN_ROWS == 0 and GDN_ROWS % GDN_CHUNK == 0
    assert w_in.shape[2] == P_WIDTH - LANES + 3 * N_HEADS
    l = 0

    bd = _block_diag_ones(QK_W, HEAD_DIM)
    tri_gdn = _lower_tri_ones(GDN_ROWS, GDN_CHUNK)
    tri_fox = _lower_tri_ones(FOX_TK)
    expand = _expand_matrix()
    tile_heads = lambda w: jnp.tile(w.astype(F32), N_HEADS)[None, :]

    outs = []
    x_rows = x.reshape(batch * s, d)
    for b in range(batch):
        xb = x_rows if batch == 1 else x_rows[b * s:(b + 1) * s]
        proj = _proj(xb, norm_w[l][None, :], _regroup_w_in(w_in.reshape(w_in.shape[1:])))
        ya = _gdn(proj, conv_w[l], _pad_lanes(a_log[l], SMALL_A), _pad_lanes(dt_bias[l], SMALL_A),
                  tile_heads(gdn_norm_w[l]), bd, tri_gdn, expand)
        qa, ka, vb, ct, bmax, crange = _foxprep(
            proj, tile_heads(fox_qn_w[l]), tile_heads(fox_kn_w[l]),
            _pad_lanes(f_bias[l], SMALL_F), bd, tri_fox, rows=FOX_TK)
        bound_max = jnp.max(bmax[0, SMALL_BOUND:SMALL_BOUND + N_HEADS])
        fast = (bound_max <= FAST_BOUND_MAX).astype(jnp.int32).reshape(1)
        first = _first_key_tile(crange.reshape(s // FOX_TF, 2, LANES), bmax, tq=FOX_TQ, tk=FOX_TF)
        yb = _fox(fast, first, qa, ka, vb, ct, proj,
                  jnp.tile(fox_on_w[l].astype(F32), 2)[None, :], tq=FOX_TQ, tk=FOX_TK)
        outs.append(_merge(xb, ya, yb, proj, gate_b[l][None, :].astype(F32),
                           w_up_gdn[l].astype(BF16), w_up_fox[l].astype(BF16),
                           w_out[l].astype(BF16), final_norm_w[None, :].astype(F32)))
    out = outs[0] if batch == 1 else jnp.concatenate(outs, axis=0)
    return out.reshape(batch, s, d)
```

```python
import functools

import numpy as np
import jax
import jax.numpy as jnp
from jax import lax
from jax.experimental import pallas as pl
from jax.experimental.pallas import tpu as pltpu

F32 = jnp.float32
BF16 = jnp.bfloat16

EPS = 1e-6
LANES = 128
HEAD_DIM = 64
N_HEADS = 8
N_PAIRS = N_HEADS // 2
QK_W = N_HEADS * HEAD_DIM
CONV_K = 4
GDN_CHUNK = 128
GDN_ROWS = 256
FOX_TQ = 512
FOX_TK = 512
FOX_TF = 256
VMEM_LIMIT = 56 * 1024 * 1024

COL_GDN_QKV = 0
COL_GDN_Z = 1536
COL_FOX_Q = 2048
COL_FOX_K = 2560
COL_FOX_V = 3072
COL_FOX_Z = 3584
COL_SMALL = 4096
P_WIDTH = 4224
SMALL_B, SMALL_A, SMALL_F = 0, 8, 16


def _dot(a, b):
    return jnp.dot(a, b, preferred_element_type=F32)


def _dot_nt(a, b):
    return lax.dot_general(a, b, (((1,), (1,)), ((), ())), preferred_element_type=F32)


def _split3(x):
    hi = x.astype(BF16)
    r = x - hi.astype(F32)
    mid = r.astype(BF16)
    lo = (r - mid.astype(F32)).astype(BF16)
    return hi, mid, lo


def _dot_sel_rhs(x, sel):
    hi, mid, lo = _split3(x)
    return _dot(hi, sel) + _dot(mid, sel) + _dot(lo, sel)


def _dot_sel_lhs(sel, x):
    hi, mid, lo = _split3(x)
    return _dot(sel, hi) + _dot(sel, mid) + _dot(sel, lo)


def _head_sumsq(x, bd):
    return _dot((x * x).astype(BF16), bd)


def _sigmoid(x):
    return 1.0 / (1.0 + jnp.exp(-x))


def _softplus(x):
    return jnp.maximum(x, 0.0) + jnp.log1p(jnp.exp(-jnp.abs(x)))


def _lane_iota(shape):
    return lax.broadcasted_iota(jnp.int32, shape, len(shape) - 1)


def _proj_kernel(x_ref, nw_ref, w_ref, gb_ref, o_ref, g_ref, *, n_chunk):
    x = x_ref[...]
    ms = jnp.mean(x * x, axis=-1, keepdims=True)
    h = (x * lax.rsqrt(ms + EPS) * nw_ref[...]).astype(BF16)
    width = o_ref.shape[1]
    for c0 in range(0, width, n_chunk):
        c1 = min(c0 + n_chunk, width)
        o_ref[:, c0:c1] = _dot(h, w_ref[:, c0:c1])
    for c0 in range(0, g_ref.shape[1], n_chunk):
        c1 = c0 + n_chunk
        logit = _dot(h, w_ref[:, width + c0:width + c1]) + gb_ref[:, c0:c1]
        g_ref[:, c0:c1] = _sigmoid(logit).astype(g_ref.dtype)


def _proj(x, norm_w, w_all, gate_b, *, tm=256):
    s, d = x.shape
    gate_w = gate_b.shape[1]
    width = w_all.shape[1] - gate_w
    return pl.pallas_call(
        functools.partial(_proj_kernel, n_chunk=1024),
        grid=(s // tm,),
        in_specs=[
            pl.BlockSpec((tm, d), lambda i: (i, 0)),
            pl.BlockSpec((1, d), lambda i: (0, 0)),
            pl.BlockSpec((d, width + gate_w), lambda i: (0, 0)),
            pl.BlockSpec((1, gate_w), lambda i: (0, 0)),
        ],
        out_specs=[pl.BlockSpec((tm, width), lambda i: (i, 0)),
                   pl.BlockSpec((tm, gate_w), lambda i: (i, 0))],
        out_shape=[jax.ShapeDtypeStruct((s, width), F32),
                   jax.ShapeDtypeStruct((s, gate_w), BF16)],
        compiler_params=pltpu.CompilerParams(
            dimension_semantics=("arbitrary",), vmem_limit_bytes=VMEM_LIMIT),
        name="proj",
    )(x, norm_w, w_all, gate_b)


INV_BASE_LOG2 = 5


def _shr(x, n):
    return lax.shift_right_logical(x, n)


def _inverse_masks(row, col):
    c = row.shape[0]
    lb = INV_BASE_LOG2
    diag = _shr(row, lb) == _shr(col, lb)
    levels = []
    b = lb
    while (1 << b) < c:
        in_pair = _shr(row, b + 1) == _shr(col, b + 1)
        levels.append(in_pair & ((_shr(row, b) & 1) == 1) & ((_shr(col, b) & 1) == 0))
        b += 1
    return diag, levels


def _unit_lower_inverses(ms, eye, masks):
    c = ms[0].shape[0]
    diag, levels = masks
    mds = [jnp.where(diag, m, 0.0) for m in ms]
    ts = [eye - md for md in mds]
    mps = [_dot(md.astype(BF16), md.astype(BF16)) for md in mds]
    n_sq = INV_BASE_LOG2 - 1
    for it in range(n_sq):
        if it < n_sq - 1:
            boths = [_dot(jnp.concatenate([t, mp], axis=0).astype(BF16), mp.astype(BF16))
                     for t, mp in zip(ts, mps)]
            ts = [t + both[0:c, :] for t, both in zip(ts, boths)]
            mps = [both[c:2 * c, :] for both in boths]
        else:
            ts = [t + _dot(t.astype(BF16), mp.astype(BF16)) for t, mp in zip(ts, mps)]
    b = 1 << INV_BASE_LOG2
    for level in levels:
        odd_rows = lambda x: jnp.concatenate([x[s:s + b, :] for s in range(b, c, 2 * b)], axis=0)
        t_odd = [odd_rows(t) for t in ts]
        tls = [_dot(to.astype(BF16), jnp.where(level, m, 0.0).astype(BF16))
               for to, m in zip(t_odd, ms)]
        new_odd = [to - _dot(tl.astype(BF16), t.astype(BF16)) for to, tl, t in zip(t_odd, tls, ts)]
        ts = [jnp.concatenate(
            [no[(blk // 2) * b:(blk // 2 + 1) * b, :] if blk % 2 else t[blk * b:(blk + 1) * b, :]
             for blk in range(c // b)], axis=0) for t, no in zip(ts, new_odd)]
        b *= 2
    return ts


def _gdn_kernel(qkv_ref, z_ref, sm_ref, cw_ref, alog_ref, dtb_ref, nw_ref,
                bd_ref, tri_ref, ex_ref, o_ref, xext_ref, state_ref):
    c = GDN_CHUNK
    r = qkv_ref.shape[0]
    chunks = [slice(ci * c, (ci + 1) * c) for ci in range(r // c)]
    step = pl.program_id(0)

    @pl.when(step == 0)
    def _():
        xext_ref[0:8, :] = jnp.zeros((8, xext_ref.shape[1]), F32)
        state_ref[...] = jnp.zeros(state_ref.shape, F32)

    xext_ref[8:8 + r, :] = qkv_ref[...]
    y = cw_ref[3:4, :] * xext_ref[8:8 + r, :]
    for j in range(CONV_K - 1):
        y = y + cw_ref[j:j + 1, :] * xext_ref[5 + j:5 + j + r, :]
    xext_ref[0:8, :] = qkv_ref[r - 8:r, :]
    y = y * _sigmoid(y)

    bd = bd_ref[...]
    q = y[:, 0:QK_W]
    k = y[:, QK_W:2 * QK_W]
    v = y[:, 2 * QK_W:3 * QK_W]
    q = q * lax.rsqrt(_head_sumsq(q, bd) + EPS) * (HEAD_DIM ** -0.5)
    k = k * lax.rsqrt(_head_sumsq(k, bd) + EPS)

    sm = sm_ref[...]
    beta_s = _sigmoid(sm)
    g_s = -jnp.exp(alog_ref[...]) * _softplus(sm + dtb_ref[...])
    g_cum = _dot_sel_lhs(tri_ref[...], g_s)
    beta = _dot(beta_s.astype(BF16), ex_ref[:, 0:QK_W])
    g_bc = _dot_sel_rhs(g_cum, ex_ref[:, QK_W:QK_W + N_HEADS * LANES])
    even_r = _lane_iota((r, LANES)) < HEAD_DIM
    g = jnp.concatenate(
        [jnp.where(even_r, g_bc[:, 2 * p * LANES:(2 * p + 1) * LANES],
                   g_bc[:, (2 * p + 1) * LANES:(2 * p + 2) * LANES]) for p in range(N_PAIRS)], axis=1)
    g_t = g_cum.T

    eg = jnp.exp(g)
    kb = k * beta
    vb = v * beta
    kbg = kb * eg
    qg = q * eg
    g_last = [g[rows.stop - 1:rows.stop, :] for rows in chunks]
    e_last = [jnp.exp(gl) for gl in g_last]
    kt = [k[rows] * jnp.exp(gl - g[rows]) for rows, gl in zip(chunks, g_last)]

    row = lax.broadcasted_iota(jnp.int32, (c, c), 0)
    col = lax.broadcasted_iota(jnp.int32, (c, c), 1)
    causal = col <= row
    strict = col < row
    eye = (col == row).astype(F32)
    inv_masks = _inverse_masks(row, col)
    lane = _lane_iota((c, LANES))
    is_even = lane < HEAD_DIM
    bd_mask = (lane < HEAD_DIM) == (lax.broadcasted_iota(jnp.int32, (c, LANES), 0) < HEAD_DIM)

    pair_lanes = [slice(p * LANES, (p + 1) * LANES) for p in range(N_PAIRS)]
    odd = jnp.logical_not(is_even)

    kq = []
    for rows in chunks:
        for ls in pair_lanes:
            kb_p, q_p = kb[rows, ls], q[rows, ls]
            lhs = jnp.concatenate([jnp.where(is_even, kb_p, 0.0), jnp.where(odd, kb_p, 0.0),
                                   jnp.where(is_even, q_p, 0.0), jnp.where(odd, q_p, 0.0)], axis=0)
            kq.append(_dot_nt(lhs.astype(BF16), k[rows, ls].astype(BF16)))
    ms, a_h = [], []
    for ci, rows in enumerate(chunks):
        for h in range(N_HEADS):
            p, hh = divmod(h, 2)
            kq_p = kq[ci * N_PAIRS + p]
            diff = g_bc[rows, h * LANES:(h + 1) * LANES] - g_t[SMALL_A + h:SMALL_A + h + 1, rows]
            dec = jnp.where(causal, jnp.exp(jnp.where(causal, diff, 0.0)), 0.0)
            ms.append(jnp.where(strict, kq_p[hh * c:(hh + 1) * c, :] * dec, 0.0))
            a_h.append((kq_p[(2 + hh) * c:(3 + hh) * c, :] * dec).astype(BF16))
    ts = _unit_lower_inverses(ms, eye, inv_masks)
    wk = []
    for ci, rows in enumerate(chunks):
        rhs = [jnp.concatenate([vb[rows, ls], kbg[rows, ls]], axis=1).astype(BF16)
               for ls in pair_lanes]
        wk += [_dot(ts[ci * N_HEADS + h].astype(BF16), rhs[h // 2]) for h in range(N_HEADS)]

    sts = [state_ref[p] for p in range(N_PAIRS)]
    for ci, rows in enumerate(chunks):
        stb = [st.astype(BF16) for st in sts]
        ub = []
        for p in range(N_PAIRS):
            wk_e, wk_o = wk[ci * N_HEADS + 2 * p], wk[ci * N_HEADS + 2 * p + 1]
            w_pair = jnp.where(is_even, wk_e[:, 0:LANES], wk_o[:, 0:LANES])
            kc_pair = jnp.where(is_even, wk_e[:, LANES:2 * LANES], wk_o[:, LANES:2 * LANES])
            ub.append((w_pair - _dot(kc_pair.astype(BF16), stb[p])).astype(BF16))
        for p, ls in enumerate(pair_lanes):
            a_e, a_o = a_h[ci * N_HEADS + 2 * p], a_h[ci * N_HEADS + 2 * p + 1]
            o = _dot(qg[rows, ls].astype(BF16), stb[p]) + jnp.where(
                is_even, _dot(a_e, ub[p]), _dot(a_o, ub[p]))
            upd = _dot(kt[ci][:, ls].T.astype(BF16), ub[p])
            sts[p] = sts[p] * e_last[ci][:, ls] + jnp.where(bd_mask, upd, 0.0)
            o_ref[rows, ls] = o.astype(o_ref.dtype)
    for p in range(N_PAIRS):
        state_ref[p] = sts[p]

    o_all = o_ref[...].astype(F32)
    z = z_ref[...]
    ms = _head_sumsq(o_all, bd) * (1.0 / HEAD_DIM)
    yn = o_all * lax.rsqrt(ms + EPS) * nw_ref[...]
    o_ref[...] = (yn * (z * _sigmoid(z))).astype(o_ref.dtype)


def _gdn(proj, conv_w, a_log_pad, dtb_pad, nw_tiled, bd, tri, expand):
    s = proj.shape[0]
    c = tri.shape[0]
    conv_width = 3 * QK_W
    const = lambda shape: pl.BlockSpec(shape, lambda i: (0,) * len(shape))
    return pl.pallas_call(
        _gdn_kernel,
        grid=(s // c,),
        in_specs=[
            pl.BlockSpec((c, conv_width), lambda i: (i, COL_GDN_QKV // conv_width)),
            pl.BlockSpec((c, QK_W), lambda i: (i, COL_GDN_Z // QK_W)),
            pl.BlockSpec((c, LANES), lambda i: (i, COL_SMALL // LANES)),
            const((CONV_K, conv_width)),
            const((1, LANES)),
            const((1, LANES)),
            const((1, QK_W)),
            const((QK_W, QK_W)),
            const((c, c)),
            const(expand.shape),
        ],
        out_specs=pl.BlockSpec((c, QK_W), lambda i: (i, 0)),
        out_shape=jax.ShapeDtypeStruct((s, QK_W), F32),
        scratch_shapes=[
            pltpu.VMEM((c + 8, conv_width), F32),
            pltpu.VMEM((N_PAIRS, LANES, LANES), F32),
        ],
        compiler_params=pltpu.CompilerParams(
            dimension_semantics=("arbitrary",), vmem_limit_bytes=VMEM_LIMIT),
        name="gdn",
    )(proj, proj, proj, conv_w, a_log_pad, dtb_pad, nw_tiled, bd, tri, expand)


X_C, X_EVEN, X_ODD, X_BOUND = 0, 3, 6, 9
SMALL_BOUND = 24
N_EXTRA_COLS = (N_HEADS + N_PAIRS) * LANES


def _extras_selector():
    sel = np.zeros((3 * LANES, N_EXTRA_COLS), np.float32)
    pat = np.zeros((1, N_EXTRA_COLS), np.float32)
    k0 = N_HEADS * LANES
    for t in range(3):
        r0 = t * LANES
        for h in range(N_HEADS):
            sel[r0 + SMALL_F + h, h * LANES + X_C + t] = 1.0
            sel[r0 + SMALL_BOUND + h, h * LANES + X_BOUND + t] = 1.0
            pat[0, h * LANES + (X_EVEN if h % 2 == 0 else X_ODD) + t] = 1.0
        for p in range(N_PAIRS):
            sel[r0 + SMALL_F + 2 * p, k0 + p * LANES + X_EVEN + t] = -1.0
            sel[r0 + SMALL_F + 2 * p + 1, k0 + p * LANES + X_ODD + t] = -1.0
            pat[0, k0 + p * LANES + X_C + t] = 1.0
            pat[0, k0 + p * LANES + X_BOUND + t] = 1.0
    return jnp.asarray(sel, dtype=BF16), jnp.asarray(pat, dtype=F32)


def _sumsq_selector():
    sel = np.zeros((QK_W, LANES), np.float32)
    for h in range(N_HEADS):
        sel[h * HEAD_DIM:(h + 1) * HEAD_DIM, SMALL_BOUND + h] = 1.0
    return jnp.asarray(sel, dtype=BF16)


def _foxprep_kernel(q_ref, k_ref, v_ref, sm_ref, qw_ref, kw_ref, fb_ref, bd_ref, tri_ref, ss_ref,
                    sel_ref, pat_ref,
                    qa_ref, ka_ref, vo_ref, ct_ref, bmax_ref, crange_ref, carry_ref, kmax_ref):
    @pl.when(pl.program_id(0) == 0)
    def _():
        carry_ref[...] = jnp.zeros(carry_ref.shape, F32)
        kmax_ref[...] = jnp.zeros(kmax_ref.shape, F32)
        bmax_ref[...] = jnp.zeros(bmax_ref.shape, F32)

    bd = bd_ref[...]

    def headnorm(x, w):
        ms = _head_sumsq(x, bd) * (1.0 / HEAD_DIM)
        return x * lax.rsqrt(ms + EPS) * w

    qb = (headnorm(q_ref[...], qw_ref[...]) * (HEAD_DIM ** -0.5)).astype(BF16)
    kb = headnorm(k_ref[...], kw_ref[...]).astype(BF16)
    vo_ref[...] = v_ref[...].astype(vo_ref.dtype)
    rows = qb.shape[0]

    xf = sm_ref[...] + fb_ref[...]
    log_f = jnp.minimum(xf, 0.0) - jnp.log1p(jnp.exp(-jnp.abs(xf)))
    cum = carry_ref[...] + _dot_sel_lhs(tri_ref[...], log_f)
    carry_ref[...] = cum[rows - 1:rows, :]
    ct_ref[0] = cum.T[SMALL_F:SMALL_F + N_HEADS, :]
    for f in range(rows // FOX_TF):
        seg = cum[f * FOX_TF:(f + 1) * FOX_TF, :]
        crange_ref[0, 2 * f:2 * f + 1, :] = jnp.max(seg, axis=0, keepdims=True)
        crange_ref[0, 2 * f + 1:2 * f + 2, :] = jnp.min(seg, axis=0, keepdims=True)

    qf = qb.astype(F32)
    kf = kb.astype(F32)
    ss = ss_ref[...]
    kss = _dot((kf * kf).astype(BF16), ss)
    kmax = jnp.maximum(kmax_ref[...], jnp.max(kss, axis=0, keepdims=True))
    kmax_ref[...] = kmax
    bound = jnp.sqrt(_dot((qf * qf).astype(BF16), ss) * kmax)
    bmax_ref[...] = jnp.maximum(bmax_ref[...], jnp.max(bound, axis=0, keepdims=True))

    lane = _lane_iota((rows, LANES))
    terms = jnp.concatenate(_split3(jnp.where(lane < SMALL_BOUND, cum, -bound)), axis=1)
    extras = (_dot(terms, sel_ref[...]) + pat_ref[...]).astype(BF16)
    is_even = lane < HEAD_DIM
    for p in range(N_PAIRS):
        ls = slice(p * LANES, (p + 1) * LANES)
        for hh in range(2):
            h = 2 * p + hh
            keep = is_even if hh == 0 else jnp.logical_not(is_even)
            qa_ref[p, hh, :, 0:LANES] = jnp.where(keep, qb[:, ls], jnp.zeros_like(qb[:, ls]))
            qa_ref[p, hh, :, LANES:2 * LANES] = extras[:, h * LANES:(h + 1) * LANES]
        ka_ref[:, 2 * p * LANES:(2 * p + 1) * LANES] = kb[:, ls]
        ka_ref[:, (2 * p + 1) * LANES:(2 * p + 2) * LANES] = extras[
            :, (N_HEADS + p) * LANES:(N_HEADS + p + 1) * LANES]


def _foxprep(proj, qw_tiled, kw_tiled, fb_pad, bd, tri, *, rows):
    s = proj.shape[0]
    const = lambda shape: pl.BlockSpec(shape, lambda i: (0,) * len(shape))
    ss_sel = _sumsq_selector()
    ex_sel, ex_pat = _extras_selector()
    return pl.pallas_call(
        _foxprep_kernel,
        grid=(s // rows,),
        in_specs=[
            pl.BlockSpec((rows, QK_W), lambda i: (i, COL_FOX_Q // QK_W)),
            pl.BlockSpec((rows, QK_W), lambda i: (i, COL_FOX_K // QK_W)),
            pl.BlockSpec((rows, QK_W), lambda i: (i, COL_FOX_V // QK_W)),
            pl.BlockSpec((rows, LANES), lambda i: (i, COL_SMALL // LANES)),
            const((1, QK_W)),
            const((1, QK_W)),
            const((1, LANES)),
            const((QK_W, QK_W)),
            const((rows, rows)),
            const(ss_sel.shape),
            const(ex_sel.shape),
            const(ex_pat.shape),
        ],
        out_specs=[
            pl.BlockSpec((N_PAIRS, 2, rows, 2 * LANES), lambda i: (0, 0, i, 0)),
            pl.BlockSpec((rows, 2 * QK_W), lambda i: (i, 0)),
            pl.BlockSpec((rows, QK_W), lambda i: (i, 0)),
            pl.BlockSpec((1, N_HEADS, rows), lambda i: (i, 0, 0)),
            pl.BlockSpec((1, LANES), lambda i: (0, 0)),
            pl.BlockSpec((1, 2 * (rows // FOX_TF), LANES), lambda i: (i, 0, 0)),
        ],
        out_shape=[
            jax.ShapeDtypeStruct((N_PAIRS, 2, s, 2 * LANES), BF16),
            jax.ShapeDtypeStruct((s, 2 * QK_W), BF16),
            jax.ShapeDtypeStruct((s, QK_W), BF16),
            jax.ShapeDtypeStruct((s // rows, N_HEADS, rows), F32),
            jax.ShapeDtypeStruct((1, LANES), F32),
            jax.ShapeDtypeStruct((s // rows, 2 * (rows // FOX_TF), LANES), F32),
        ],
        scratch_shapes=[pltpu.VMEM((1, LANES), F32), pltpu.VMEM((1, LANES), F32)],
        compiler_params=pltpu.CompilerParams(
            dimension_semantics=("arbitrary",), vmem_limit_bytes=VMEM_LIMIT),
        name="foxprep",
    )(proj, proj, proj, proj, qw_tiled, kw_tiled, fb_pad, bd, tri, ss_sel, ex_sel, ex_pat)


FAST_BOUND_MAX = 40.0
FAST_TAIL_BACK = 3


def _fox_kernel(fast_ref, first_ref, q_ref, k_ref, v_ref, c_ref, z_ref, nw_ref, o_ref, m_ref,
                acc_ref, *, tq, tk, tf):
    p = pl.program_id(0)
    i = pl.program_id(1)
    first = first_ref[p * pl.num_programs(1) + i]
    n_sub = tq // tk
    n_diag = tq // tf

    acc_ref[...] = jnp.zeros(acc_ref.shape, F32)

    def kv_rows(start, size):
        start = pl.multiple_of(start, tf)
        vj = jnp.concatenate([v_ref[pl.ds(start, size), :], jnp.ones((size, LANES), BF16)], axis=1)
        return k_ref[pl.ds(start, size), :], vj

    def causal_keep(j):
        q_pos = i * tq + lax.broadcasted_iota(jnp.int32, (tq, tk), 0)
        k_pos = j * tk + lax.broadcasted_iota(jnp.int32, (tq, tk), 1)
        return k_pos <= q_pos

    def fast_tile(j):
        kj, vj = kv_rows(j * tf, tf)
        for hh in range(2):
            s = _dot_nt(q_ref[0, hh], kj)
            acc_ref[hh] += _dot(jnp.exp(s).astype(BF16), vj)

    def slow_tile(j, masked):
        kj, vj = kv_rows(j * tk, tk)
        for hh in range(2):
            ck = c_ref[j, pl.ds(2 * p + hh, 1), :]
            s = _dot_nt(q_ref[0, hh, :, 0:LANES], kj[:, 0:LANES]) - ck
            if masked:
                s = jnp.where(causal_keep(j), s, -jnp.inf)
            m_prev = m_ref[hh]
            m_new = jnp.maximum(m_prev, jnp.max(s, axis=-1, keepdims=True))
            alpha = jnp.exp(m_prev - m_new)
            pr = jnp.exp(s - m_new)
            acc_ref[hh] = alpha * acc_ref[hh] + _dot(pr.astype(BF16), vj)
            m_ref[hh] = m_new

    def run(tile):
        def body(j, carry):
            tile(j, False)
            return carry
        lax.fori_loop(first // (tk // tf), i * n_sub, body, 0)
        for d in range(n_sub):
            tile(i * n_sub + d, True)

    def fast_tail(n_back):
        span = n_back * tf + tq
        ks, vs = kv_rows(i * tq - n_back * tf, span)
        q_rel = n_back * tf + lax.broadcasted_iota(jnp.int32, (tq, span), 0)
        keep = lax.broadcasted_iota(jnp.int32, (tq, span), 1) <= q_rel
        for hh in range(2):
            s = jnp.where(keep, _dot_nt(q_ref[0, hh], ks), -jnp.inf)
            acc_ref[hh] += _dot(jnp.exp(s).astype(BF16), vs)

    def run_fast():
        def body(j, carry):
            fast_tile(j)
            return carry
        lax.fori_loop(first, jnp.maximum(first, i * n_diag - FAST_TAIL_BACK), body, 0)
        n_short = -(-FAST_TAIL_BACK // n_diag)
        for i_short in range(n_short):
            pl.when(i == i_short)(functools.partial(fast_tail, i_short * n_diag))
        pl.when(i >= n_short)(functools.partial(fast_tail, FAST_TAIL_BACK))

    @pl.when(fast_ref[0] == 1)
    def _():
        run_fast()

    @pl.when(fast_ref[0] != 1)
    def _():
        m_ref[...] = jnp.full(m_ref.shape, -jnp.inf, F32)
        run(slow_tile)

    acc_e = acc_ref[0]
    acc_o = acc_ref[1]
    is_even = _lane_iota((tq, LANES)) < HEAD_DIM
    o = jnp.where(is_even, acc_e[:, 0:LANES] / acc_e[:, LANES:2 * LANES],
                  acc_o[:, 0:LANES] / acc_o[:, LANES:2 * LANES])
    o2 = o * o
    ss_e = jnp.sum(jnp.where(is_even, o2, 0.0), axis=-1, keepdims=True)
    ss_o = jnp.sum(jnp.where(is_even, 0.0, o2), axis=-1, keepdims=True)
    ms = jnp.where(is_even, ss_e, ss_o) * (1.0 / HEAD_DIM)
    z = z_ref[...]
    o_ref[...] = (o * lax.rsqrt(ms + EPS) * nw_ref[...] * (z * _sigmoid(z))).astype(o_ref.dtype)


SKIP_LOG = 110.0


def _first_key_tile(crange, bmax, *, tq, tk):
    n_k = crange.shape[0]
    n_sub = tq // tk
    c_max = crange[:, 0, SMALL_F:SMALL_F + N_HEADS].T
    c_min = crange[:, 1, SMALL_F:SMALL_F + N_HEADS].T
    cq_max = c_max.reshape(N_HEADS, n_k // n_sub, n_sub).max(axis=-1)
    bound = bmax[0, SMALL_BOUND:SMALL_BOUND + N_HEADS]
    gap = cq_max[:, :, None] - c_min[:, None, :] + 2.0 * bound[:, None, None]
    dead = gap <= -SKIP_LOG
    tile_idx = lax.broadcasted_iota(jnp.int32, dead.shape, 2)
    first = jnp.min(jnp.where(dead, n_k, tile_idx), axis=-1)
    first = jnp.min(first.reshape(N_PAIRS, 2, -1), axis=1)
    first = jnp.minimum(first, jnp.arange(first.shape[1], dtype=jnp.int32) * n_sub)
    return first.reshape(-1).astype(jnp.int32)


def _fox(fast, first, qa, ka, vb, ct, proj, nw_pair, *, tq, tk):
    s = vb.shape[0]
    grid_spec = pltpu.PrefetchScalarGridSpec(
        num_scalar_prefetch=2,
        grid=(N_PAIRS, s // tq),
        in_specs=[
            pl.BlockSpec((1, 2, tq, 2 * LANES), lambda p, i, *_: (p, 0, i, 0)),
            pl.BlockSpec((s, 2 * LANES), lambda p, i, *_: (0, p)),
            pl.BlockSpec((s, LANES), lambda p, i, *_: (0, p)),
            pl.BlockSpec(ct.shape, lambda p, i, *_: (0, 0, 0)),
            pl.BlockSpec((tq, LANES), lambda p, i, *_: (i, COL_FOX_Z // LANES + p)),
            pl.BlockSpec((1, LANES), lambda p, i, *_: (0, 0)),
        ],
        out_specs=pl.BlockSpec((tq, LANES), lambda p, i, *_: (i, p)),
        scratch_shapes=[
            pltpu.VMEM((2, tq, 1), F32),
            pltpu.VMEM((2, tq, 2 * LANES), F32),
        ],
    )
    return pl.pallas_call(
        functools.partial(_fox_kernel, tq=tq, tk=tk, tf=FOX_TF),
        grid_spec=grid_spec,
        out_shape=jax.ShapeDtypeStruct((s, QK_W), BF16),
        compiler_params=pltpu.CompilerParams(
            dimension_semantics=("arbitrary", "arbitrary"), vmem_limit_bytes=VMEM_LIMIT),
        name="fox",
    )(fast, first, qa, ka, vb, ct, proj, nw_pair)


def _merge_kernel(x_ref, ya_ref, yb_ref, ga_ref, gb_ref, pa_ref, pb_ref, wo_ref, fw_ref, o_ref):
    ua = _dot(ya_ref[...].astype(BF16), pa_ref[...])
    ub = _dot(yb_ref[...].astype(BF16), pb_ref[...])
    merged = ga_ref[...].astype(F32) * ua + gb_ref[...].astype(F32) * ub
    xn = x_ref[...] + _dot(merged.astype(BF16), wo_ref[...])
    ms = jnp.mean(xn * xn, axis=-1, keepdims=True)
    o_ref[...] = xn * lax.rsqrt(ms + EPS) * fw_ref[...]


def _merge(x, ya, yb, gates, w_up_a, w_up_b, w_out, out_w, *, tm=512):
    s, d = x.shape
    const = lambda shape: pl.BlockSpec(shape, lambda i: (0,) * len(shape))
    return pl.pallas_call(
        _merge_kernel,
        grid=(s // tm,),
        in_specs=[
            pl.BlockSpec((tm, d), lambda i: (i, 0)),
            pl.BlockSpec((tm, QK_W), lambda i: (i, 0)),
            pl.BlockSpec((tm, QK_W), lambda i: (i, 0)),
            pl.BlockSpec((tm, d), lambda i: (i, 0)),
            pl.BlockSpec((tm, d), lambda i: (i, 1)),
            const((QK_W, d)),
            const((QK_W, d)),
            const((d, d)),
            const((1, d)),
        ],
        out_specs=pl.BlockSpec((tm, d), lambda i: (i, 0)),
        out_shape=jax.ShapeDtypeStruct((s, d), F32),
        compiler_params=pltpu.CompilerParams(
            dimension_semantics=("arbitrary",), vmem_limit_bytes=VMEM_LIMIT),
        name="merge",
    )(x, ya, yb, gates, gates, w_up_a, w_up_b, w_out, out_w)


def _block_diag_ones(n, blk):
    idx = np.arange(n) // blk
    return jnp.asarray(idx[:, None] == idx[None, :], dtype=BF16)


def _lower_tri_ones(n, blk=None):
    idx = np.arange(n)
    tri = idx[None, :] <= idx[:, None]
    if blk is not None:
        tri = tri & (idx[None, :] // blk == idx[:, None] // blk)
    return jnp.asarray(tri, dtype=BF16)


def _expand_matrix():
    e = np.zeros((LANES, QK_W + N_HEADS * LANES), np.float32)
    for h in range(N_HEADS):
        e[SMALL_B + h, h * HEAD_DIM:(h + 1) * HEAD_DIM] = 1.0
        e[SMALL_A + h, QK_W + h * LANES:QK_W + (h + 1) * LANES] = 1.0
    return jnp.asarray(e, dtype=BF16)


def _pad_lanes(vec, offset):
    return jnp.zeros((1, LANES), F32).at[0, offset:offset + vec.shape[0]].set(vec.astype(F32))


def _regroup_w_in(w):
    d = w.shape[0]
    o_z = 3 * QK_W
    o_b = o_z + QK_W
    o_a = o_b + N_HEADS
    o_fq = o_a + N_HEADS
    o_ff = o_fq + 4 * QK_W
    o_gate = o_ff + N_HEADS
    small = jnp.concatenate(
        [w[:, o_b:o_a], w[:, o_a:o_fq], w[:, o_ff:o_gate],
         jnp.zeros((d, LANES - 3 * N_HEADS), w.dtype)], axis=1)
    return jnp.concatenate([w[:, 0:o_b], w[:, o_fq:o_ff], small, w[:, o_gate:]], axis=1).astype(BF16)


def kernel(x, norm_w, w_in, gate_b, conv_w, a_log, dt_bias, gdn_norm_w, f_bias,
           fox_qn_w, fox_kn_w, fox_on_w, w_up_gdn, w_up_fox, w_out, final_norm_w):
    batch, s, d = x.shape
    assert norm_w.shape[0] == 1, "the merge call fuses the final RMSNorm: single-layer trunk only"
    assert s % FOX_TQ == 0 and s % GDN_ROWS == 0 and GDN_ROWS % GDN_CHUNK == 0
    assert w_in.shape[2] == P_WIDTH - LANES + 3 * N_HEADS + gate_b.shape[1]
    l = 0

    bd = _block_diag_ones(QK_W, HEAD_DIM)
    tri_gdn = _lower_tri_ones(GDN_ROWS, GDN_CHUNK)
    tri_fox = _lower_tri_ones(FOX_TK)
    expand = _expand_matrix()
    tile_heads = lambda w: jnp.tile(w.astype(F32), N_HEADS)[None, :]

    outs = []
    x_rows = x.reshape(batch * s, d)
    for b in range(batch):
        xb = x_rows if batch == 1 else x_rows[b * s:(b + 1) * s]
        proj, gates = _proj(xb, norm_w[l][None, :], _regroup_w_in(w_in.reshape(w_in.shape[1:])),
                            gate_b[l][None, :].astype(F32))
        ya = _gdn(proj, conv_w[l], _pad_lanes(a_log[l], SMALL_A), _pad_lanes(dt_bias[l], SMALL_A),
                  tile_heads(gdn_norm_w[l]), bd, tri_gdn, expand)
        qa, ka, vb, ct, bmax, crange = _foxprep(
            proj, tile_heads(fox_qn_w[l]), tile_heads(fox_kn_w[l]),
            _pad_lanes(f_bias[l], SMALL_F), bd, tri_fox, rows=FOX_TK)
        bound_max = jnp.max(bmax[0, SMALL_BOUND:SMALL_BOUND + N_HEADS])
        fast = (bound_max <= FAST_BOUND_MAX).astype(jnp.int32).reshape(1)
        first = _first_key_tile(crange.reshape(s // FOX_TF, 2, LANES), bmax, tq=FOX_TQ, tk=FOX_TF)
        yb = _fox(fast, first, qa, ka, vb, ct, proj,
                  jnp.tile(fox_on_w[l].astype(F32), 2)[None, :], tq=FOX_TQ, tk=FOX_TK)
        outs.append(_merge(xb, ya, yb, gates,
                           w_up_gdn[l].astype(BF16), w_up_fox[l].astype(BF16),
                           w_out[l].astype(BF16), final_norm_w[None, :].astype(F32)))
    out = outs[0] if batch == 1 else jnp.concatenate(outs, axis=0)
    return out.reshape(batch, s, d)
```

```python
import functools

import numpy as np
import jax
import jax.numpy as jnp
from jax import lax
from jax.experimental import pallas as pl
from jax.experimental.pallas import tpu as pltpu

F32 = jnp.float32
BF16 = jnp.bfloat16

EPS = 1e-6
LANES = 128
HEAD_DIM = 64
N_HEADS = 8
N_PAIRS = N_HEADS // 2
QK_W = N_HEADS * HEAD_DIM
CONV_K = 4
GDN_CHUNK = 128
GDN_ROWS = 256
FOX_TQ = 512
FOX_TK = 512
FOX_TF = 256
VMEM_LIMIT = 56 * 1024 * 1024

COL_GDN_QKV = 0
COL_GDN_Z = 1536
COL_FOX_Q = 2048
COL_FOX_K = 2560
COL_FOX_V = 3072
COL_FOX_Z = 3584
COL_SMALL = 4096
P_WIDTH = 4224
SMALL_B, SMALL_A, SMALL_F = 0, 8, 16


def _dot(a, b):
    return jnp.dot(a, b, preferred_element_type=F32)


def _dot_nt(a, b):
    return lax.dot_general(a, b, (((1,), (1,)), ((), ())), preferred_element_type=F32)


def _split3(x):
    hi = x.astype(BF16)
    r = x - hi.astype(F32)
    mid = r.astype(BF16)
    lo = (r - mid.astype(F32)).astype(BF16)
    return hi, mid, lo


def _dot_sel_rhs(x, sel):
    hi, mid, lo = _split3(x)
    return _dot(hi, sel) + _dot(mid, sel) + _dot(lo, sel)


def _dot_sel_lhs(sel, x):
    hi, mid, lo = _split3(x)
    return _dot(sel, hi) + _dot(sel, mid) + _dot(sel, lo)


def _head_sumsq(x, bd):
    return _dot((x * x).astype(BF16), bd)


def _sigmoid(x):
    return 1.0 / (1.0 + jnp.exp(-x))


def _softplus(x):
    return jnp.maximum(x, 0.0) + jnp.log1p(jnp.exp(-jnp.abs(x)))


def _lane_iota(shape):
    return lax.broadcasted_iota(jnp.int32, shape, len(shape) - 1)


def _proj_kernel(x_ref, nw_ref, w_ref, gb_ref, o_ref, g_ref, *, n_chunk):
    x = x_ref[...]
    ms = jnp.mean(x * x, axis=-1, keepdims=True)
    h = (x * lax.rsqrt(ms + EPS) * nw_ref[...]).astype(BF16)
    width = o_ref.shape[1]
    for c0 in range(0, width, n_chunk):
        c1 = min(c0 + n_chunk, width)
        o_ref[:, c0:c1] = _dot(h, w_ref[:, c0:c1])
    for c0 in range(0, g_ref.shape[1], n_chunk):
        c1 = c0 + n_chunk
        logit = _dot(h, w_ref[:, width + c0:width + c1]) + gb_ref[:, c0:c1]
        g_ref[:, c0:c1] = _sigmoid(logit).astype(g_ref.dtype)


def _proj(x, norm_w, w_all, gate_b, *, tm=512):
    s, d = x.shape
    gate_w = gate_b.shape[1]
    width = w_all.shape[1] - gate_w
    return pl.pallas_call(
        functools.partial(_proj_kernel, n_chunk=1024),
        grid=(s // tm,),
        in_specs=[
            pl.BlockSpec((tm, d), lambda i: (i, 0)),
            pl.BlockSpec((1, d), lambda i: (0, 0)),
            pl.BlockSpec((d, width + gate_w), lambda i: (0, 0)),
            pl.BlockSpec((1, gate_w), lambda i: (0, 0)),
        ],
        out_specs=[pl.BlockSpec((tm, width), lambda i: (i, 0)),
                   pl.BlockSpec((tm, gate_w), lambda i: (i, 0))],
        out_shape=[jax.ShapeDtypeStruct((s, width), F32),
                   jax.ShapeDtypeStruct((s, gate_w), BF16)],
        compiler_params=pltpu.CompilerParams(
            dimension_semantics=("arbitrary",), vmem_limit_bytes=VMEM_LIMIT),
        name="proj",
    )(x, norm_w, w_all, gate_b)


INV_BASE_LOG2 = 5


def _shr(x, n):
    return lax.shift_right_logical(x, n)


def _inverse_masks(row, col):
    c = row.shape[0]
    lb = INV_BASE_LOG2
    diag = _shr(row, lb) == _shr(col, lb)
    levels = []
    b = lb
    while (1 << b) < c:
        in_pair = _shr(row, b + 1) == _shr(col, b + 1)
        levels.append(in_pair & ((_shr(row, b) & 1) == 1) & ((_shr(col, b) & 1) == 0))
        b += 1
    return diag, levels


def _unit_lower_inverses(ms, eye, masks):
    c = ms[0].shape[0]
    diag, levels = masks
    mds = [jnp.where(diag, m, 0.0) for m in ms]
    ts = [eye - md for md in mds]
    mps = [_dot(md.astype(BF16), md.astype(BF16)) for md in mds]
    n_sq = INV_BASE_LOG2 - 1
    for it in range(n_sq):
        if it < n_sq - 1:
            boths = [_dot(jnp.concatenate([t, mp], axis=0).astype(BF16), mp.astype(BF16))
                     for t, mp in zip(ts, mps)]
            ts = [t + both[0:c, :] for t, both in zip(ts, boths)]
            mps = [both[c:2 * c, :] for both in boths]
        else:
            ts = [t + _dot(t.astype(BF16), mp.astype(BF16)) for t, mp in zip(ts, mps)]
    b = 1 << INV_BASE_LOG2
    for level in levels:
        odd_rows = lambda x: jnp.concatenate([x[s:s + b, :] for s in range(b, c, 2 * b)], axis=0)
        t_odd = [odd_rows(t) for t in ts]
        tls = [_dot(to.astype(BF16), jnp.where(level, m, 0.0).astype(BF16))
               for to, m in zip(t_odd, ms)]
        new_odd = [to - _dot(tl.astype(BF16), t.astype(BF16)) for to, tl, t in zip(t_odd, tls, ts)]
        ts = [jnp.concatenate(
            [no[(blk // 2) * b:(blk // 2 + 1) * b, :] if blk % 2 else t[blk * b:(blk + 1) * b, :]
             for blk in range(c // b)], axis=0) for t, no in zip(ts, new_odd)]
        b *= 2
    return ts


def _gdn_kernel(qkv_ref, z_ref, sm_ref, cw_ref, alog_ref, dtb_ref, nw_ref,
                bd_ref, tri_ref, ex_ref, o_ref, xext_ref, state_ref):
    c = GDN_CHUNK
    r = qkv_ref.shape[0]
    chunks = [slice(ci * c, (ci + 1) * c) for ci in range(r // c)]
    step = pl.program_id(0)

    @pl.when(step == 0)
    def _():
        xext_ref[0:8, :] = jnp.zeros((8, xext_ref.shape[1]), F32)
        state_ref[...] = jnp.zeros(state_ref.shape, F32)

    xext_ref[8:8 + r, :] = qkv_ref[...]
    y = cw_ref[3:4, :] * xext_ref[8:8 + r, :]
    for j in range(CONV_K - 1):
        y = y + cw_ref[j:j + 1, :] * xext_ref[5 + j:5 + j + r, :]
    xext_ref[0:8, :] = qkv_ref[r - 8:r, :]
    y = y * _sigmoid(y)

    bd = bd_ref[...]
    q = y[:, 0:QK_W]
    k = y[:, QK_W:2 * QK_W]
    v = y[:, 2 * QK_W:3 * QK_W]
    q = q * lax.rsqrt(_head_sumsq(q, bd) + EPS) * (HEAD_DIM ** -0.5)
    k = k * lax.rsqrt(_head_sumsq(k, bd) + EPS)

    sm = sm_ref[...]
    beta_s = _sigmoid(sm)
    g_s = -jnp.exp(alog_ref[...]) * _softplus(sm + dtb_ref[...])
    g_cum = _dot_sel_lhs(tri_ref[...], g_s)
    beta = _dot(beta_s.astype(BF16), ex_ref[:, 0:QK_W])
    g_bc = _dot_sel_rhs(g_cum, ex_ref[:, QK_W:QK_W + N_HEADS * LANES])
    even_r = _lane_iota((r, LANES)) < HEAD_DIM
    g = jnp.concatenate(
        [jnp.where(even_r, g_bc[:, 2 * p * LANES:(2 * p + 1) * LANES],
                   g_bc[:, (2 * p + 1) * LANES:(2 * p + 2) * LANES]) for p in range(N_PAIRS)], axis=1)
    g_t = g_cum.T

    eg = jnp.exp(g)
    kb = k * beta
    vb = v * beta
    kbg = kb * eg
    qg = q * eg
    g_last = [g[rows.stop - 1:rows.stop, :] for rows in chunks]
    e_last = [jnp.exp(gl) for gl in g_last]
    kt = [k[rows] * jnp.exp(gl - g[rows]) for rows, gl in zip(chunks, g_last)]

    row = lax.broadcasted_iota(jnp.int32, (c, c), 0)
    col = lax.broadcasted_iota(jnp.int32, (c, c), 1)
    causal = col <= row
    strict = col < row
    eye = (col == row).astype(F32)
    inv_masks = _inverse_masks(row, col)
    lane = _lane_iota((c, LANES))
    is_even = lane < HEAD_DIM
    bd_mask = (lane < HEAD_DIM) == (lax.broadcasted_iota(jnp.int32, (c, LANES), 0) < HEAD_DIM)

    pair_lanes = [slice(p * LANES, (p + 1) * LANES) for p in range(N_PAIRS)]
    odd = jnp.logical_not(is_even)

    kq = []
    for rows in chunks:
        for ls in pair_lanes:
            kb_p, q_p = kb[rows, ls], q[rows, ls]
            lhs = jnp.concatenate([jnp.where(is_even, kb_p, 0.0), jnp.where(odd, kb_p, 0.0),
                                   jnp.where(is_even, q_p, 0.0), jnp.where(odd, q_p, 0.0)], axis=0)
            kq.append(_dot_nt(lhs.astype(BF16), k[rows, ls].astype(BF16)))
    ms, a_h = [], []
    for ci, rows in enumerate(chunks):
        for h in range(N_HEADS):
            p, hh = divmod(h, 2)
            kq_p = kq[ci * N_PAIRS + p]
            diff = g_bc[rows, h * LANES:(h + 1) * LANES] - g_t[SMALL_A + h:SMALL_A + h + 1, rows]
            dec = jnp.where(causal, jnp.exp(jnp.where(causal, diff, 0.0)), 0.0)
            ms.append(jnp.where(strict, kq_p[hh * c:(hh + 1) * c, :] * dec, 0.0))
            a_h.append((kq_p[(2 + hh) * c:(3 + hh) * c, :] * dec).astype(BF16))
    ts = _unit_lower_inverses(ms, eye, inv_masks)
    wk = []
    for ci, rows in enumerate(chunks):
        rhs = [jnp.concatenate([vb[rows, ls], kbg[rows, ls]], axis=1).astype(BF16)
               for ls in pair_lanes]
        wk += [_dot(ts[ci * N_HEADS + h].astype(BF16), rhs[h // 2]) for h in range(N_HEADS)]

    sts = [state_ref[p] for p in range(N_PAIRS)]
    for ci, rows in enumerate(chunks):
        stb = [st.astype(BF16) for st in sts]
        ub = []
        for p in range(N_PAIRS):
            wk_e, wk_o = wk[ci * N_HEADS + 2 * p], wk[ci * N_HEADS + 2 * p + 1]
            w_pair = jnp.where(is_even, wk_e[:, 0:LANES], wk_o[:, 0:LANES])
            kc_pair = jnp.where(is_even, wk_e[:, LANES:2 * LANES], wk_o[:, LANES:2 * LANES])
            ub.append((w_pair - _dot(kc_pair.astype(BF16), stb[p])).astype(BF16))
        for p, ls in enumerate(pair_lanes):
            a_e, a_o = a_h[ci * N_HEADS + 2 * p], a_h[ci * N_HEADS + 2 * p + 1]
            o = _dot(qg[rows, ls].astype(BF16), stb[p]) + jnp.where(
                is_even, _dot(a_e, ub[p]), _dot(a_o, ub[p]))
            upd = _dot(kt[ci][:, ls].T.astype(BF16), ub[p])
            sts[p] = sts[p] * e_last[ci][:, ls] + jnp.where(bd_mask, upd, 0.0)
            o_ref[rows, ls] = o.astype(o_ref.dtype)
    for p in range(N_PAIRS):
        state_ref[p] = sts[p]

    o_all = o_ref[...].astype(F32)
    z = z_ref[...]
    ms = _head_sumsq(o_all, bd) * (1.0 / HEAD_DIM)
    yn = o_all * lax.rsqrt(ms + EPS) * nw_ref[...]
    o_ref[...] = (yn * (z * _sigmoid(z))).astype(o_ref.dtype)


def _gdn(proj, conv_w, a_log_pad, dtb_pad, nw_tiled, bd, tri, expand):
    s = proj.shape[0]
    c = tri.shape[0]
    conv_width = 3 * QK_W
    const = lambda shape: pl.BlockSpec(shape, lambda i: (0,) * len(shape))
    return pl.pallas_call(
        _gdn_kernel,
        grid=(s // c,),
        in_specs=[
            pl.BlockSpec((c, conv_width), lambda i: (i, COL_GDN_QKV // conv_width)),
            pl.BlockSpec((c, QK_W), lambda i: (i, COL_GDN_Z // QK_W)),
            pl.BlockSpec((c, LANES), lambda i: (i, COL_SMALL // LANES)),
            const((CONV_K, conv_width)),
            const((1, LANES)),
            const((1, LANES)),
            const((1, QK_W)),
            const((QK_W, QK_W)),
            const((c, c)),
            const(expand.shape),
        ],
        out_specs=pl.BlockSpec((c, QK_W), lambda i: (i, 0)),
        out_shape=jax.ShapeDtypeStruct((s, QK_W), F32),
        scratch_shapes=[
            pltpu.VMEM((c + 8, conv_width), F32),
            pltpu.VMEM((N_PAIRS, LANES, LANES), F32),
        ],
        compiler_params=pltpu.CompilerParams(
            dimension_semantics=("arbitrary",), vmem_limit_bytes=VMEM_LIMIT),
        name="gdn",
    )(proj, proj, proj, conv_w, a_log_pad, dtb_pad, nw_tiled, bd, tri, expand)


X_C, X_EVEN, X_ODD, X_BOUND = 0, 3, 6, 9
SMALL_BOUND = 24
TERM_LANE_STEP = 16
N_EXTRA_COLS = (N_HEADS + N_PAIRS) * LANES


def _extras_selector():
    sel = np.zeros((LANES, N_EXTRA_COLS), np.float32)
    pat = np.zeros((1, N_EXTRA_COLS), np.float32)
    k0 = N_HEADS * LANES
    for t in range(3):
        r0 = t * TERM_LANE_STEP
        for h in range(N_HEADS):
            sel[r0 + SMALL_F + h, h * LANES + X_C + t] = 1.0
            sel[r0 + SMALL_BOUND + h, h * LANES + X_BOUND + t] = 1.0
            pat[0, h * LANES + (X_EVEN if h % 2 == 0 else X_ODD) + t] = 1.0
        for p in range(N_PAIRS):
            sel[r0 + SMALL_F + 2 * p, k0 + p * LANES + X_EVEN + t] = -1.0
            sel[r0 + SMALL_F + 2 * p + 1, k0 + p * LANES + X_ODD + t] = -1.0
            pat[0, k0 + p * LANES + X_C + t] = 1.0
            pat[0, k0 + p * LANES + X_BOUND + t] = 1.0
    return jnp.asarray(sel, dtype=BF16), jnp.asarray(pat, dtype=F32)


def _sumsq_selector():
    sel = np.zeros((QK_W, LANES), np.float32)
    for h in range(N_HEADS):
        sel[h * HEAD_DIM:(h + 1) * HEAD_DIM, SMALL_BOUND + h] = 1.0
    return jnp.asarray(sel, dtype=BF16)


def _foxprep_kernel(q_ref, k_ref, v_ref, sm_ref, qw_ref, kw_ref, fb_ref, bd_ref, tri_ref, ss_ref,
                    sel_ref, pat_ref,
                    qa_ref, ka_ref, vo_ref, ct_ref, bmax_ref, crange_ref, carry_ref, kmax_ref):
    @pl.when(pl.program_id(0) == 0)
    def _():
        carry_ref[...] = jnp.zeros(carry_ref.shape, F32)
        kmax_ref[...] = jnp.zeros(kmax_ref.shape, F32)
        bmax_ref[...] = jnp.zeros(bmax_ref.shape, F32)

    bd = bd_ref[...]

    def headnorm(x, w):
        ms = _head_sumsq(x, bd) * (1.0 / HEAD_DIM)
        return x * lax.rsqrt(ms + EPS) * w

    qb = (headnorm(q_ref[...], qw_ref[...]) * (HEAD_DIM ** -0.5)).astype(BF16)
    kb = headnorm(k_ref[...], kw_ref[...]).astype(BF16)
    vo_ref[...] = v_ref[...].astype(vo_ref.dtype)
    rows = qb.shape[0]

    xf = sm_ref[...] + fb_ref[...]
    log_f = jnp.minimum(xf, 0.0) - jnp.log1p(jnp.exp(-jnp.abs(xf)))
    cum = carry_ref[...] + _dot_sel_lhs(tri_ref[...], log_f)
    carry_ref[...] = cum[rows - 1:rows, :]
    ct_ref[0] = cum.T[SMALL_F:SMALL_F + N_HEADS, :]
    for f in range(rows // FOX_TF):
        seg = cum[f * FOX_TF:(f + 1) * FOX_TF, :]
        crange_ref[0, 2 * f:2 * f + 1, :] = jnp.max(seg, axis=0, keepdims=True)
        crange_ref[0, 2 * f + 1:2 * f + 2, :] = jnp.min(seg, axis=0, keepdims=True)

    qf = qb.astype(F32)
    kf = kb.astype(F32)
    ss = ss_ref[...]
    kss = _dot((kf * kf).astype(BF16), ss)
    kmax = jnp.maximum(kmax_ref[...], jnp.max(kss, axis=0, keepdims=True))
    kmax_ref[...] = kmax
    bound = jnp.sqrt(_dot((qf * qf).astype(BF16), ss) * kmax)
    bmax_ref[...] = jnp.maximum(bmax_ref[...], jnp.max(bound, axis=0, keepdims=True))

    lane = _lane_iota((rows, LANES))
    hi, mid, lo = _split3(jnp.where(lane < SMALL_BOUND, cum, -bound))
    first_mid = SMALL_F + TERM_LANE_STEP
    terms = jnp.where(lane < first_mid, hi.astype(F32), jnp.where(
        lane < first_mid + TERM_LANE_STEP, pltpu.roll(mid.astype(F32), TERM_LANE_STEP, 1),
        pltpu.roll(lo.astype(F32), 2 * TERM_LANE_STEP, 1)))
    extras = (_dot(terms.astype(BF16), sel_ref[...]) + pat_ref[...]).astype(BF16)
    is_even = lane < HEAD_DIM
    for p in range(N_PAIRS):
        ls = slice(p * LANES, (p + 1) * LANES)
        for hh in range(2):
            h = 2 * p + hh
            keep = is_even if hh == 0 else jnp.logical_not(is_even)
            qa_ref[p, hh, :, 0:LANES] = jnp.where(keep, qb[:, ls], jnp.zeros_like(qb[:, ls]))
            qa_ref[p, hh, :, LANES:2 * LANES] = extras[:, h * LANES:(h + 1) * LANES]
        ka_ref[:, 2 * p * LANES:(2 * p + 1) * LANES] = kb[:, ls]
        ka_ref[:, (2 * p + 1) * LANES:(2 * p + 2) * LANES] = extras[
            :, (N_HEADS + p) * LANES:(N_HEADS + p + 1) * LANES]


def _foxprep(proj, qw_tiled, kw_tiled, fb_pad, bd, tri, *, rows):
    s = proj.shape[0]
    const = lambda shape: pl.BlockSpec(shape, lambda i: (0,) * len(shape))
    ss_sel = _sumsq_selector()
    ex_sel, ex_pat = _extras_selector()
    return pl.pallas_call(
        _foxprep_kernel,
        grid=(s // rows,),
        in_specs=[
            pl.BlockSpec((rows, QK_W), lambda i: (i, COL_FOX_Q // QK_W)),
            pl.BlockSpec((rows, QK_W), lambda i: (i, COL_FOX_K // QK_W)),
            pl.BlockSpec((rows, QK_W), lambda i: (i, COL_FOX_V // QK_W)),
            pl.BlockSpec((rows, LANES), lambda i: (i, COL_SMALL // LANES)),
            const((1, QK_W)),
            const((1, QK_W)),
            const((1, LANES)),
            const((QK_W, QK_W)),
            const((rows, rows)),
            const(ss_sel.shape),
            const(ex_sel.shape),
            const(ex_pat.shape),
        ],
        out_specs=[
            pl.BlockSpec((N_PAIRS, 2, rows, 2 * LANES), lambda i: (0, 0, i, 0)),
            pl.BlockSpec((rows, 2 * QK_W), lambda i: (i, 0)),
            pl.BlockSpec((rows, QK_W), lambda i: (i, 0)),
            pl.BlockSpec((1, N_HEADS, rows), lambda i: (i, 0, 0)),
            pl.BlockSpec((1, LANES), lambda i: (0, 0)),
            pl.BlockSpec((1, 2 * (rows // FOX_TF), LANES), lambda i: (i, 0, 0)),
        ],
        out_shape=[
            jax.ShapeDtypeStruct((N_PAIRS, 2, s, 2 * LANES), BF16),
            jax.ShapeDtypeStruct((s, 2 * QK_W), BF16),
            jax.ShapeDtypeStruct((s, QK_W), BF16),
            jax.ShapeDtypeStruct((s // rows, N_HEADS, rows), F32),
            jax.ShapeDtypeStruct((1, LANES), F32),
            jax.ShapeDtypeStruct((s // rows, 2 * (rows // FOX_TF), LANES), F32),
        ],
        scratch_shapes=[pltpu.VMEM((1, LANES), F32), pltpu.VMEM((1, LANES), F32)],
        compiler_params=pltpu.CompilerParams(
            dimension_semantics=("arbitrary",), vmem_limit_bytes=VMEM_LIMIT),
        name="foxprep",
    )(proj, proj, proj, proj, qw_tiled, kw_tiled, fb_pad, bd, tri, ss_sel, ex_sel, ex_pat)


FAST_BOUND_MAX = 40.0
FAST_TAIL_BACK = 3


def _fox_kernel(fast_ref, first_ref, q_ref, k_ref, v_ref, c_ref, z_ref, nw_ref, o_ref, m_ref,
                acc_ref, *, tq, tk, tf):
    p = pl.program_id(0)
    i = pl.program_id(1)
    first = first_ref[p * pl.num_programs(1) + i]
    n_sub = tq // tk
    n_diag = tq // tf

    acc_ref[...] = jnp.zeros(acc_ref.shape, F32)

    def kv_rows(start, size):
        start = pl.multiple_of(start, tf)
        vj = jnp.concatenate([v_ref[pl.ds(start, size), :], jnp.ones((size, LANES), BF16)], axis=1)
        return k_ref[pl.ds(start, size), :], vj

    def causal_keep(j):
        q_pos = i * tq + lax.broadcasted_iota(jnp.int32, (tq, tk), 0)
        k_pos = j * tk + lax.broadcasted_iota(jnp.int32, (tq, tk), 1)
        return k_pos <= q_pos

    def fast_tile(j):
        kj, vj = kv_rows(j * tf, tf)
        for hh in range(2):
            s = _dot_nt(q_ref[0, hh], kj)
            acc_ref[hh] += _dot(jnp.exp(s).astype(BF16), vj)

    def slow_tile(j, masked):
        kj, vj = kv_rows(j * tk, tk)
        for hh in range(2):
            ck = c_ref[j, pl.ds(2 * p + hh, 1), :]
            s = _dot_nt(q_ref[0, hh, :, 0:LANES], kj[:, 0:LANES]) - ck
            if masked:
                s = jnp.where(causal_keep(j), s, -jnp.inf)
            m_prev = m_ref[hh]
            m_new = jnp.maximum(m_prev, jnp.max(s, axis=-1, keepdims=True))
            alpha = jnp.exp(m_prev - m_new)
            pr = jnp.exp(s - m_new)
            acc_ref[hh] = alpha * acc_ref[hh] + _dot(pr.astype(BF16), vj)
            m_ref[hh] = m_new

    def run(tile):
        def body(j, carry):
            tile(j, False)
            return carry
        lax.fori_loop(first // (tk // tf), i * n_sub, body, 0)
        for d in range(n_sub):
            tile(i * n_sub + d, True)

    def fast_tail(n_back):
        span = n_back * tf + tq
        ks, vs = kv_rows(i * tq - n_back * tf, span)
        q_rel = n_back * tf + lax.broadcasted_iota(jnp.int32, (tq, span), 0)
        keep = lax.broadcasted_iota(jnp.int32, (tq, span), 1) <= q_rel
        for hh in range(2):
            s = jnp.where(keep, _dot_nt(q_ref[0, hh], ks), -jnp.inf)
            acc_ref[hh] += _dot(jnp.exp(s).astype(BF16), vs)

    def run_fast():
        def body(j, carry):
            fast_tile(j)
            return carry
        lax.fori_loop(first, jnp.maximum(first, i * n_diag - FAST_TAIL_BACK), body, 0)
        n_short = -(-FAST_TAIL_BACK // n_diag)
        for i_short in range(n_short):
            pl.when(i == i_short)(functools.partial(fast_tail, i_short * n_diag))
        pl.when(i >= n_short)(functools.partial(fast_tail, FAST_TAIL_BACK))

    @pl.when(fast_ref[0] == 1)
    def _():
        run_fast()

    @pl.when(fast_ref[0] != 1)
    def _():
        m_ref[...] = jnp.full(m_ref.shape, -jnp.inf, F32)
        run(slow_tile)

    acc_e = acc_ref[0]
    acc_o = acc_ref[1]
    is_even = _lane_iota((tq, LANES)) < HEAD_DIM
    o = jnp.where(is_even, acc_e[:, 0:LANES] / acc_e[:, LANES:2 * LANES],
                  acc_o[:, 0:LANES] / acc_o[:, LANES:2 * LANES])
    o2 = o * o
    ss_e = jnp.sum(jnp.where(is_even, o2, 0.0), axis=-1, keepdims=True)
    ss_o = jnp.sum(jnp.where(is_even, 0.0, o2), axis=-1, keepdims=True)
    ms = jnp.where(is_even, ss_e, ss_o) * (1.0 / HEAD_DIM)
    z = z_ref[...]
    o_ref[...] = (o * lax.rsqrt(ms + EPS) * nw_ref[...] * (z * _sigmoid(z))).astype(o_ref.dtype)


SKIP_LOG = 110.0


def _first_key_tile(crange, bmax, *, tq, tk):
    n_k = crange.shape[0]
    n_sub = tq // tk
    c_max = crange[:, 0, SMALL_F:SMALL_F + N_HEADS].T
    c_min = crange[:, 1, SMALL_F:SMALL_F + N_HEADS].T
    cq_max = c_max.reshape(N_HEADS, n_k // n_sub, n_sub).max(axis=-1)
    bound = bmax[0, SMALL_BOUND:SMALL_BOUND + N_HEADS]
    gap = cq_max[:, :, None] - c_min[:, None, :] + 2.0 * bound[:, None, None]
    dead = gap <= -SKIP_LOG
    tile_idx = lax.broadcasted_iota(jnp.int32, dead.shape, 2)
    first = jnp.min(jnp.where(dead, n_k, tile_idx), axis=-1)
    first = jnp.min(first.reshape(N_PAIRS, 2, -1), axis=1)
    first = jnp.minimum(first, jnp.arange(first.shape[1], dtype=jnp.int32) * n_sub)
    return first.reshape(-1).astype(jnp.int32)


def _fox(fast, first, qa, ka, vb, ct, proj, nw_pair, *, tq, tk):
    s = vb.shape[0]
    grid_spec = pltpu.PrefetchScalarGridSpec(
        num_scalar_prefetch=2,
        grid=(N_PAIRS, s // tq),
        in_specs=[
            pl.BlockSpec((1, 2, tq, 2 * LANES), lambda p, i, *_: (p, 0, i, 0)),
            pl.BlockSpec((s, 2 * LANES), lambda p, i, *_: (0, p)),
            pl.BlockSpec((s, LANES), lambda p, i, *_: (0, p)),
            pl.BlockSpec(ct.shape, lambda p, i, *_: (0, 0, 0)),
            pl.BlockSpec((tq, LANES), lambda p, i, *_: (i, COL_FOX_Z // LANES + p)),
            pl.BlockSpec((1, LANES), lambda p, i, *_: (0, 0)),
        ],
        out_specs=pl.BlockSpec((tq, LANES), lambda p, i, *_: (i, p)),
        scratch_shapes=[
            pltpu.VMEM((2, tq, 1), F32),
            pltpu.VMEM((2, tq, 2 * LANES), F32),
        ],
    )
    return pl.pallas_call(
        functools.partial(_fox_kernel, tq=tq, tk=tk, tf=FOX_TF),
        grid_spec=grid_spec,
        out_shape=jax.ShapeDtypeStruct((s, QK_W), BF16),
        compiler_params=pltpu.CompilerParams(
            dimension_semantics=("arbitrary", "arbitrary"), vmem_limit_bytes=VMEM_LIMIT),
        name="fox",
    )(fast, first, qa, ka, vb, ct, proj, nw_pair)


def _merge_kernel(x_ref, ya_ref, yb_ref, ga_ref, gb_ref, pa_ref, pb_ref, wo_ref, fw_ref, o_ref):
    ua = _dot(ya_ref[...].astype(BF16), pa_ref[...])
    ub = _dot(yb_ref[...].astype(BF16), pb_ref[...])
    merged = ga_ref[...].astype(F32) * ua + gb_ref[...].astype(F32) * ub
    xn = x_ref[...] + _dot(merged.astype(BF16), wo_ref[...])
    ms = jnp.mean(xn * xn, axis=-1, keepdims=True)
    o_ref[...] = xn * lax.rsqrt(ms + EPS) * fw_ref[...]


def _merge(x, ya, yb, gates, w_up_a, w_up_b, w_out, out_w, *, tm=512):
    s, d = x.shape
    const = lambda shape: pl.BlockSpec(shape, lambda i: (0,) * len(shape))
    return pl.pallas_call(
        _merge_kernel,
        grid=(s // tm,),
        in_specs=[
            pl.BlockSpec((tm, d), lambda i: (i, 0)),
            pl.BlockSpec((tm, QK_W), lambda i: (i, 0)),
            pl.BlockSpec((tm, QK_W), lambda i: (i, 0)),
            pl.BlockSpec((tm, d), lambda i: (i, 0)),
            pl.BlockSpec((tm, d), lambda i: (i, 1)),
            const((QK_W, d)),
            const((QK_W, d)),
            const((d, d)),
            const((1, d)),
        ],
        out_specs=pl.BlockSpec((tm, d), lambda i: (i, 0)),
        out_shape=jax.ShapeDtypeStruct((s, d), F32),
        compiler_params=pltpu.CompilerParams(
            dimension_semantics=("arbitrary",), vmem_limit_bytes=VMEM_LIMIT),
        name="merge",
    )(x, ya, yb, gates, gates, w_up_a, w_up_b, w_out, out_w)


def _block_diag_ones(n, blk):
    idx = np.arange(n) // blk
    return jnp.asarray(idx[:, None] == idx[None, :], dtype=BF16)


def _lower_tri_ones(n, blk=None):
    idx = np.arange(n)
    tri = idx[None, :] <= idx[:, None]
    if blk is not None:
        tri = tri & (idx[None, :] // blk == idx[:, None] // blk)
    return jnp.asarray(tri, dtype=BF16)


def _expand_matrix():
    e = np.zeros((LANES, QK_W + N_HEADS * LANES), np.float32)
    for h in range(N_HEADS):
        e[SMALL_B + h, h * HEAD_DIM:(h + 1) * HEAD_DIM] = 1.0
        e[SMALL_A + h, QK_W + h * LANES:QK_W + (h + 1) * LANES] = 1.0
    return jnp.asarray(e, dtype=BF16)


def _pad_lanes(vec, offset):
    return jnp.zeros((1, LANES), F32).at[0, offset:offset + vec.shape[0]].set(vec.astype(F32))


def _regroup_w_in(w):
    d = w.shape[0]
    o_z = 3 * QK_W
    o_b = o_z + QK_W
    o_a = o_b + N_HEADS
    o_fq = o_a + N_HEADS
    o_ff = o_fq + 4 * QK_W
    o_gate = o_ff + N_HEADS
    small = jnp.concatenate(
        [w[:, o_b:o_a], w[:, o_a:o_fq], w[:, o_ff:o_gate],
         jnp.zeros((d, LANES - 3 * N_HEADS), w.dtype)], axis=1)
    return jnp.concatenate([w[:, 0:o_b], w[:, o_fq:o_ff], small, w[:, o_gate:]], axis=1).astype(BF16)


def kernel(x, norm_w, w_in, gate_b, conv_w, a_log, dt_bias, gdn_norm_w, f_bias,
           fox_qn_w, fox_kn_w, fox_on_w, w_up_gdn, w_up_fox, w_out, final_norm_w):
    batch, s, d = x.shape
    assert norm_w.shape[0] == 1, "the merge call fuses the final RMSNorm: single-layer trunk only"
    assert s % FOX_TQ == 0 and s % GDN_ROWS == 0 and GDN_ROWS % GDN_CHUNK == 0
    assert w_in.shape[2] == P_WIDTH - LANES + 3 * N_HEADS + gate_b.shape[1]
    l = 0

    bd = _block_diag_ones(QK_W, HEAD_DIM)
    tri_gdn = _lower_tri_ones(GDN_ROWS, GDN_CHUNK)
    tri_fox = _lower_tri_ones(FOX_TK)
    expand = _expand_matrix()
    tile_heads = lambda w: jnp.tile(w.astype(F32), N_HEADS)[None, :]

    outs = []
    x_rows = x.reshape(batch * s, d)
    for b in range(batch):
        xb = x_rows if batch == 1 else x_rows[b * s:(b + 1) * s]
        proj, gates = _proj(xb, norm_w[l][None, :], _regroup_w_in(w_in.reshape(w_in.shape[1:])),
                            gate_b[l][None, :].astype(F32))
        ya = _gdn(proj, conv_w[l], _pad_lanes(a_log[l], SMALL_A), _pad_lanes(dt_bias[l], SMALL_A),
                  tile_heads(gdn_norm_w[l]), bd, tri_gdn, expand)
        qa, ka, vb, ct, bmax, crange = _foxprep(
            proj, tile_heads(fox_qn_w[l]), tile_heads(fox_kn_w[l]),
            _pad_lanes(f_bias[l], SMALL_F), bd, tri_fox, rows=FOX_TK)
        bound_max = jnp.max(bmax[0, SMALL_BOUND:SMALL_BOUND + N_HEADS])
        fast = (bound_max <= FAST_BOUND_MAX).astype(jnp.int32).reshape(1)
        first = _first_key_tile(crange.reshape(s // FOX_TF, 2, LANES), bmax, tq=FOX_TQ, tk=FOX_TF)
        yb = _fox(fast, first, qa, ka, vb, ct, proj,
                  jnp.tile(fox_on_w[l].astype(F32), 2)[None, :], tq=FOX_TQ, tk=FOX_TK)
        outs.append(_merge(xb, ya, yb, gates,
                           w_up_gdn[l].astype(BF16), w_up_fox[l].astype(BF16),
                           w_out[l].astype(BF16), final_norm_w[None, :].astype(F32)))
    out = outs[0] if batch == 1 else jnp.concatenate(outs, axis=0)
    return out.reshape(batch, s, d)
```

```python
import functools

import numpy as np
import jax
import jax.numpy as jnp
from jax import lax
from jax.experimental import pallas as pl
from jax.experimental.pallas import tpu as pltpu

F32 = jnp.float32
BF16 = jnp.bfloat16

EPS = 1e-6
LANES = 128
HEAD_DIM = 64
N_HEADS = 8
N_PAIRS = N_HEADS // 2
QK_W = N_HEADS * HEAD_DIM
CONV_K = 4
GDN_CHUNK = 128
GDN_ROWS = 256
FOX_TQ = 512
FOX_Q_TILES_PER_STEP = 2
FOX_TK = 512
FOX_TF = 256
VMEM_LIMIT = 56 * 1024 * 1024

COL_GDN_QKV = 0
COL_GDN_Z = 1536
COL_FOX_Q = 2048
COL_FOX_K = 2560
COL_FOX_V = 3072
COL_FOX_Z = 3584
COL_SMALL = 4096
P_WIDTH = 4224
SMALL_B, SMALL_A, SMALL_F = 0, 8, 16


def _dot(a, b):
    return jnp.dot(a, b, preferred_element_type=F32)


def _dot_nt(a, b):
    return lax.dot_general(a, b, (((1,), (1,)), ((), ())), preferred_element_type=F32)


def _split3(x):
    hi = x.astype(BF16)
    r = x - hi.astype(F32)
    mid = r.astype(BF16)
    lo = (r - mid.astype(F32)).astype(BF16)
    return hi, mid, lo


def _dot_sel_rhs(x, sel):
    hi, mid, lo = _split3(x)
    return _dot(hi, sel) + _dot(mid, sel) + _dot(lo, sel)


def _dot_sel_lhs(sel, x):
    hi, mid, lo = _split3(x)
    return _dot(sel, hi) + _dot(sel, mid) + _dot(sel, lo)


def _head_sumsq(x, bd):
    return _dot((x * x).astype(BF16), bd)


def _sigmoid(x):
    return 1.0 / (1.0 + jnp.exp(-x))


def _softplus(x):
    return jnp.maximum(x, 0.0) + jnp.log1p(jnp.exp(-jnp.abs(x)))


def _lane_iota(shape):
    return lax.broadcasted_iota(jnp.int32, shape, len(shape) - 1)


def _proj_kernel(x_ref, nw_ref, w_ref, gb_ref, o_ref, g_ref, *, n_chunk):
    x = x_ref[...]
    ms = jnp.mean(x * x, axis=-1, keepdims=True)
    h = (x * lax.rsqrt(ms + EPS) * nw_ref[...]).astype(BF16)
    width = o_ref.shape[1]
    for c0 in range(0, width, n_chunk):
        c1 = min(c0 + n_chunk, width)
        o_ref[:, c0:c1] = _dot(h, w_ref[:, c0:c1])
    for c0 in range(0, g_ref.shape[1], n_chunk):
        c1 = c0 + n_chunk
        logit = _dot(h, w_ref[:, width + c0:width + c1]) + gb_ref[:, c0:c1]
        g_ref[:, c0:c1] = _sigmoid(logit).astype(g_ref.dtype)


def _proj(x, norm_w, w_all, gate_b, *, tm=512):
    s, d = x.shape
    gate_w = gate_b.shape[1]
    width = w_all.shape[1] - gate_w
    return pl.pallas_call(
        functools.partial(_proj_kernel, n_chunk=1024),
        grid=(s // tm,),
        in_specs=[
            pl.BlockSpec((tm, d), lambda i: (i, 0)),
            pl.BlockSpec((1, d), lambda i: (0, 0)),
            pl.BlockSpec((d, width + gate_w), lambda i: (0, 0)),
            pl.BlockSpec((1, gate_w), lambda i: (0, 0)),
        ],
        out_specs=[pl.BlockSpec((tm, width), lambda i: (i, 0)),
                   pl.BlockSpec((tm, gate_w), lambda i: (i, 0))],
        out_shape=[jax.ShapeDtypeStruct((s, width), F32),
                   jax.ShapeDtypeStruct((s, gate_w), BF16)],
        compiler_params=pltpu.CompilerParams(
            dimension_semantics=("arbitrary",), vmem_limit_bytes=VMEM_LIMIT),
        name="proj",
    )(x, norm_w, w_all, gate_b)


INV_BASE_LOG2 = 5


def _shr(x, n):
    return lax.shift_right_logical(x, n)


def _inverse_masks(row, col):
    c = row.shape[0]
    lb = INV_BASE_LOG2
    diag = _shr(row, lb) == _shr(col, lb)
    levels = []
    b = lb
    while (1 << b) < c:
        in_pair = _shr(row, b + 1) == _shr(col, b + 1)
        levels.append(in_pair & ((_shr(row, b) & 1) == 1) & ((_shr(col, b) & 1) == 0))
        b += 1
    return diag, levels


def _unit_lower_inverses(ms, eye, masks):
    c = ms[0].shape[0]
    diag, levels = masks
    mds = [jnp.where(diag, m, 0.0) for m in ms]
    ts = [eye - md for md in mds]
    mps = [_dot(md.astype(BF16), md.astype(BF16)) for md in mds]
    n_sq = INV_BASE_LOG2 - 1
    for it in range(n_sq):
        if it < n_sq - 1:
            boths = [_dot(jnp.concatenate([t, mp], axis=0).astype(BF16), mp.astype(BF16))
                     for t, mp in zip(ts, mps)]
            ts = [t + both[0:c, :] for t, both in zip(ts, boths)]
            mps = [both[c:2 * c, :] for both in boths]
        else:
            ts = [t + _dot(t.astype(BF16), mp.astype(BF16)) for t, mp in zip(ts, mps)]
    b = 1 << INV_BASE_LOG2
    for level in levels:
        odd_rows = lambda x: jnp.concatenate([x[s:s + b, :] for s in range(b, c, 2 * b)], axis=0)
        t_odd = [odd_rows(t) for t in ts]
        tls = [_dot(to.astype(BF16), jnp.where(level, m, 0.0).astype(BF16))
               for to, m in zip(t_odd, ms)]
        new_odd = [to - _dot(tl.astype(BF16), t.astype(BF16)) for to, tl, t in zip(t_odd, tls, ts)]
        ts = [jnp.concatenate(
            [no[(blk // 2) * b:(blk // 2 + 1) * b, :] if blk % 2 else t[blk * b:(blk + 1) * b, :]
             for blk in range(c // b)], axis=0) for t, no in zip(ts, new_odd)]
        b *= 2
    return ts


def _gdn_kernel(qkv_ref, z_ref, sm_ref, cw_ref, alog_ref, dtb_ref, nw_ref,
                bd_ref, tri_ref, ex_ref, o_ref, xext_ref, state_ref):
    c = GDN_CHUNK
    r = qkv_ref.shape[0]
    chunks = [slice(ci * c, (ci + 1) * c) for ci in range(r // c)]
    step = pl.program_id(0)

    @pl.when(step == 0)
    def _():
        xext_ref[0:8, :] = jnp.zeros((8, xext_ref.shape[1]), F32)
        state_ref[...] = jnp.zeros(state_ref.shape, F32)

    xext_ref[8:8 + r, :] = qkv_ref[...]
    y = cw_ref[3:4, :] * xext_ref[8:8 + r, :]
    for j in range(CONV_K - 1):
        y = y + cw_ref[j:j + 1, :] * xext_ref[5 + j:5 + j + r, :]
    xext_ref[0:8, :] = qkv_ref[r - 8:r, :]
    y = y * _sigmoid(y)

    bd = bd_ref[...]
    q = y[:, 0:QK_W]
    k = y[:, QK_W:2 * QK_W]
    v = y[:, 2 * QK_W:3 * QK_W]
    q = q * lax.rsqrt(_head_sumsq(q, bd) + EPS) * (HEAD_DIM ** -0.5)
    k = k * lax.rsqrt(_head_sumsq(k, bd) + EPS)

    sm = sm_ref[...]
    beta_s = _sigmoid(sm)
    g_s = -jnp.exp(alog_ref[...]) * _softplus(sm + dtb_ref[...])
    g_cum = _dot_sel_lhs(tri_ref[...], g_s)
    beta = _dot(beta_s.astype(BF16), ex_ref[:, 0:QK_W])
    g_bc = _dot_sel_rhs(g_cum, ex_ref[:, QK_W:QK_W + N_HEADS * LANES])
    even_r = _lane_iota((r, LANES)) < HEAD_DIM
    g = jnp.concatenate(
        [jnp.where(even_r, g_bc[:, 2 * p * LANES:(2 * p + 1) * LANES],
                   g_bc[:, (2 * p + 1) * LANES:(2 * p + 2) * LANES]) for p in range(N_PAIRS)], axis=1)
    g_t = g_cum.T

    eg = jnp.exp(g)
    kb = k * beta
    vb = v * beta
    kbg = kb * eg
    qg = q * eg
    g_last = [g[rows.stop - 1:rows.stop, :] for rows in chunks]
    e_last = [jnp.exp(gl) for gl in g_last]
    kt = [k[rows] * jnp.exp(gl - g[rows]) for rows, gl in zip(chunks, g_last)]

    row = lax.broadcasted_iota(jnp.int32, (c, c), 0)
    col = lax.broadcasted_iota(jnp.int32, (c, c), 1)
    causal = col <= row
    strict = col < row
    eye = (col == row).astype(F32)
    inv_masks = _inverse_masks(row, col)
    lane = _lane_iota((c, LANES))
    is_even = lane < HEAD_DIM
    bd_mask = (lane < HEAD_DIM) == (lax.broadcasted_iota(jnp.int32, (c, LANES), 0) < HEAD_DIM)

    pair_lanes = [slice(p * LANES, (p + 1) * LANES) for p in range(N_PAIRS)]
    odd = jnp.logical_not(is_even)

    kq = []
    for rows in chunks:
        for ls in pair_lanes:
            kb_p, q_p = kb[rows, ls], q[rows, ls]
            lhs = jnp.concatenate([jnp.where(is_even, kb_p, 0.0), jnp.where(odd, kb_p, 0.0),
                                   jnp.where(is_even, q_p, 0.0), jnp.where(odd, q_p, 0.0)], axis=0)
            kq.append(_dot_nt(lhs.astype(BF16), k[rows, ls].astype(BF16)))
    ms, a_h = [], []
    for ci, rows in enumerate(chunks):
        for h in range(N_HEADS):
            p, hh = divmod(h, 2)
            kq_p = kq[ci * N_PAIRS + p]
            diff = g_bc[rows, h * LANES:(h + 1) * LANES] - g_t[SMALL_A + h:SMALL_A + h + 1, rows]
            dec = jnp.where(causal, jnp.exp(jnp.where(causal, diff, 0.0)), 0.0)
            ms.append(jnp.where(strict, kq_p[hh * c:(hh + 1) * c, :] * dec, 0.0))
            a_h.append((kq_p[(2 + hh) * c:(3 + hh) * c, :] * dec).astype(BF16))
    ts = _unit_lower_inverses(ms, eye, inv_masks)
    wk = []
    for ci, rows in enumerate(chunks):
        rhs = [jnp.concatenate([vb[rows, ls], kbg[rows, ls]], axis=1).astype(BF16)
               for ls in pair_lanes]
        wk += [_dot(ts[ci * N_HEADS + h].astype(BF16), rhs[h // 2]) for h in range(N_HEADS)]

    sts = [state_ref[p] for p in range(N_PAIRS)]
    for ci, rows in enumerate(chunks):
        stb = [st.astype(BF16) for st in sts]
        ub = []
        for p in range(N_PAIRS):
            wk_e, wk_o = wk[ci * N_HEADS + 2 * p], wk[ci * N_HEADS + 2 * p + 1]
            w_pair = jnp.where(is_even, wk_e[:, 0:LANES], wk_o[:, 0:LANES])
            kc_pair = jnp.where(is_even, wk_e[:, LANES:2 * LANES], wk_o[:, LANES:2 * LANES])
            ub.append((w_pair - _dot(kc_pair.astype(BF16), stb[p])).astype(BF16))
        for p, ls in enumerate(pair_lanes):
            a_e, a_o = a_h[ci * N_HEADS + 2 * p], a_h[ci * N_HEADS + 2 * p + 1]
            o = _dot(qg[rows, ls].astype(BF16), stb[p]) + jnp.where(
                is_even, _dot(a_e, ub[p]), _dot(a_o, ub[p]))
            upd = _dot(kt[ci][:, ls].T.astype(BF16), ub[p])
            sts[p] = sts[p] * e_last[ci][:, ls] + jnp.where(bd_mask, upd, 0.0)
            o_ref[rows, ls] = o.astype(o_ref.dtype)
    for p in range(N_PAIRS):
        state_ref[p] = sts[p]

    o_all = o_ref[...].astype(F32)
    z = z_ref[...]
    ms = _head_sumsq(o_all, bd) * (1.0 / HEAD_DIM)
    yn = o_all * lax.rsqrt(ms + EPS) * nw_ref[...]
    o_ref[...] = (yn * (z * _sigmoid(z))).astype(o_ref.dtype)


def _gdn(proj, conv_w, a_log_pad, dtb_pad, nw_tiled, bd, tri, expand):
    s = proj.shape[0]
    c = tri.shape[0]
    conv_width = 3 * QK_W
    const = lambda shape: pl.BlockSpec(shape, lambda i: (0,) * len(shape))
    return pl.pallas_call(
        _gdn_kernel,
        grid=(s // c,),
        in_specs=[
            pl.BlockSpec((c, conv_width), lambda i: (i, COL_GDN_QKV // conv_width)),
            pl.BlockSpec((c, QK_W), lambda i: (i, COL_GDN_Z // QK_W)),
            pl.BlockSpec((c, LANES), lambda i: (i, COL_SMALL // LANES)),
            const((CONV_K, conv_width)),
            const((1, LANES)),
            const((1, LANES)),
            const((1, QK_W)),
            const((QK_W, QK_W)),
            const((c, c)),
            const(expand.shape),
        ],
        out_specs=pl.BlockSpec((c, QK_W), lambda i: (i, 0)),
        out_shape=jax.ShapeDtypeStruct((s, QK_W), F32),
        scratch_shapes=[
            pltpu.VMEM((c + 8, conv_width), F32),
            pltpu.VMEM((N_PAIRS, LANES, LANES), F32),
        ],
        compiler_params=pltpu.CompilerParams(
            dimension_semantics=("arbitrary",), vmem_limit_bytes=VMEM_LIMIT),
        name="gdn",
    )(proj, proj, proj, conv_w, a_log_pad, dtb_pad, nw_tiled, bd, tri, expand)


X_C, X_EVEN, X_ODD, X_BOUND = 0, 3, 6, 9
SMALL_BOUND = 24
TERM_LANE_STEP = 16
N_EXTRA_COLS = (N_HEADS + N_PAIRS) * LANES


def _extras_selector():
    sel = np.zeros((LANES, N_EXTRA_COLS), np.float32)
    pat = np.zeros((1, N_EXTRA_COLS), np.float32)
    k0 = N_HEADS * LANES
    for t in range(3):
        r0 = t * TERM_LANE_STEP
        for h in range(N_HEADS):
            sel[r0 + SMALL_F + h, h * LANES + X_C + t] = 1.0
            sel[r0 + SMALL_BOUND + h, h * LANES + X_BOUND + t] = 1.0
            pat[0, h * LANES + (X_EVEN if h % 2 == 0 else X_ODD) + t] = 1.0
        for p in range(N_PAIRS):
            sel[r0 + SMALL_F + 2 * p, k0 + p * LANES + X_EVEN + t] = -1.0
            sel[r0 + SMALL_F + 2 * p + 1, k0 + p * LANES + X_ODD + t] = -1.0
            pat[0, k0 + p * LANES + X_C + t] = 1.0
            pat[0, k0 + p * LANES + X_BOUND + t] = 1.0
    return jnp.asarray(sel, dtype=BF16), jnp.asarray(pat, dtype=F32)


def _sumsq_selector():
    sel = np.zeros((QK_W, LANES), np.float32)
    for h in range(N_HEADS):
        sel[h * HEAD_DIM:(h + 1) * HEAD_DIM, SMALL_BOUND + h] = 1.0
    return jnp.asarray(sel, dtype=BF16)


def _foxprep_kernel(q_ref, k_ref, v_ref, sm_ref, qw_ref, kw_ref, fb_ref, bd_ref, tri_ref, ss_ref,
                    sel_ref, pat_ref,
                    qa_ref, ka_ref, vo_ref, ct_ref, bmax_ref, crange_ref, carry_ref, kmax_ref):
    @pl.when(pl.program_id(0) == 0)
    def _():
        carry_ref[...] = jnp.zeros(carry_ref.shape, F32)
        kmax_ref[...] = jnp.zeros(kmax_ref.shape, F32)
        bmax_ref[...] = jnp.zeros(bmax_ref.shape, F32)

    bd = bd_ref[...]

    def headnorm(x, w):
        ms = _head_sumsq(x, bd) * (1.0 / HEAD_DIM)
        return x * lax.rsqrt(ms + EPS) * w

    qb = (headnorm(q_ref[...], qw_ref[...]) * (HEAD_DIM ** -0.5)).astype(BF16)
    kb = headnorm(k_ref[...], kw_ref[...]).astype(BF16)
    vo_ref[...] = v_ref[...].astype(vo_ref.dtype)
    rows = qb.shape[0]

    xf = sm_ref[...] + fb_ref[...]
    log_f = jnp.minimum(xf, 0.0) - jnp.log1p(jnp.exp(-jnp.abs(xf)))
    cum = carry_ref[...] + _dot_sel_lhs(tri_ref[...], log_f)
    carry_ref[...] = cum[rows - 1:rows, :]
    ct_ref[0] = cum.T[SMALL_F:SMALL_F + N_HEADS, :]
    for f in range(rows // FOX_TF):
        seg = cum[f * FOX_TF:(f + 1) * FOX_TF, :]
        crange_ref[0, 2 * f:2 * f + 1, :] = jnp.max(seg, axis=0, keepdims=True)
        crange_ref[0, 2 * f + 1:2 * f + 2, :] = jnp.min(seg, axis=0, keepdims=True)

    qf = qb.astype(F32)
    kf = kb.astype(F32)
    ss = ss_ref[...]
    kss = _dot((kf * kf).astype(BF16), ss)
    kmax = jnp.maximum(kmax_ref[...], jnp.max(kss, axis=0, keepdims=True))
    kmax_ref[...] = kmax
    bound = jnp.sqrt(_dot((qf * qf).astype(BF16), ss) * kmax)
    bmax_ref[...] = jnp.maximum(bmax_ref[...], jnp.max(bound, axis=0, keepdims=True))

    lane = _lane_iota((rows, LANES))
    hi, mid, lo = _split3(jnp.where(lane < SMALL_BOUND, cum, -bound))
    first_mid = SMALL_F + TERM_LANE_STEP
    terms = jnp.where(lane < first_mid, hi.astype(F32), jnp.where(
        lane < first_mid + TERM_LANE_STEP, pltpu.roll(mid.astype(F32), TERM_LANE_STEP, 1),
        pltpu.roll(lo.astype(F32), 2 * TERM_LANE_STEP, 1)))
    extras = (_dot(terms.astype(BF16), sel_ref[...]) + pat_ref[...]).astype(BF16)
    is_even = lane < HEAD_DIM
    for p in range(N_PAIRS):
        ls = slice(p * LANES, (p + 1) * LANES)
        for hh in range(2):
            h = 2 * p + hh
            keep = is_even if hh == 0 else jnp.logical_not(is_even)
            qa_ref[p, hh, :, 0:LANES] = jnp.where(keep, qb[:, ls], jnp.zeros_like(qb[:, ls]))
            qa_ref[p, hh, :, LANES:2 * LANES] = extras[:, h * LANES:(h + 1) * LANES]
        ka_ref[:, 2 * p * LANES:(2 * p + 1) * LANES] = kb[:, ls]
        ka_ref[:, (2 * p + 1) * LANES:(2 * p + 2) * LANES] = extras[
            :, (N_HEADS + p) * LANES:(N_HEADS + p + 1) * LANES]


def _foxprep(proj, qw_tiled, kw_tiled, fb_pad, bd, tri, *, rows):
    s = proj.shape[0]
    const = lambda shape: pl.BlockSpec(shape, lambda i: (0,) * len(shape))
    ss_sel = _sumsq_selector()
    ex_sel, ex_pat = _extras_selector()
    return pl.pallas_call(
        _foxprep_kernel,
        grid=(s // rows,),
        in_specs=[
            pl.BlockSpec((rows, QK_W), lambda i: (i, COL_FOX_Q // QK_W)),
            pl.BlockSpec((rows, QK_W), lambda i: (i, COL_FOX_K // QK_W)),
            pl.BlockSpec((rows, QK_W), lambda i: (i, COL_FOX_V // QK_W)),
            pl.BlockSpec((rows, LANES), lambda i: (i, COL_SMALL // LANES)),
            const((1, QK_W)),
            const((1, QK_W)),
            const((1, LANES)),
            const((QK_W, QK_W)),
            const((rows, rows)),
            const(ss_sel.shape),
            const(ex_sel.shape),
            const(ex_pat.shape),
        ],
        out_specs=[
            pl.BlockSpec((N_PAIRS, 2, rows, 2 * LANES), lambda i: (0, 0, i, 0)),
            pl.BlockSpec((rows, 2 * QK_W), lambda i: (i, 0)),
            pl.BlockSpec((rows, QK_W), lambda i: (i, 0)),
            pl.BlockSpec((1, N_HEADS, rows), lambda i: (i, 0, 0)),
            pl.BlockSpec((1, LANES), lambda i: (0, 0)),
            pl.BlockSpec((1, 2 * (rows // FOX_TF), LANES), lambda i: (i, 0, 0)),
        ],
        out_shape=[
            jax.ShapeDtypeStruct((N_PAIRS, 2, s, 2 * LANES), BF16),
            jax.ShapeDtypeStruct((s, 2 * QK_W), BF16),
            jax.ShapeDtypeStruct((s, QK_W), BF16),
            jax.ShapeDtypeStruct((s // rows, N_HEADS, rows), F32),
            jax.ShapeDtypeStruct((1, LANES), F32),
            jax.ShapeDtypeStruct((s // rows, 2 * (rows // FOX_TF), LANES), F32),
        ],
        scratch_shapes=[pltpu.VMEM((1, LANES), F32), pltpu.VMEM((1, LANES), F32)],
        compiler_params=pltpu.CompilerParams(
            dimension_semantics=("arbitrary",), vmem_limit_bytes=VMEM_LIMIT),
        name="foxprep",
    )(proj, proj, proj, proj, qw_tiled, kw_tiled, fb_pad, bd, tri, ss_sel, ex_sel, ex_pat)


FAST_BOUND_MAX = 40.0
FAST_TAIL_BACK = 3


def _fox_kernel(fast_ref, first_ref, q_ref, k_ref, v_ref, c_ref, z_ref, nw_ref, o_ref, m_ref,
                acc_ref, *, tq, tk, tf):
    for t in range(q_ref.shape[2] // tq):
        _fox_query_tile(t, fast_ref, first_ref, q_ref, k_ref, v_ref, c_ref, z_ref, nw_ref, o_ref,
                        m_ref, acc_ref, tq=tq, tk=tk, tf=tf)


def _fox_query_tile(t, fast_ref, first_ref, q_ref, k_ref, v_ref, c_ref, z_ref, nw_ref, o_ref, m_ref,
                    acc_ref, *, tq, tk, tf):
    tiles = q_ref.shape[2] // tq
    rows = slice(t * tq, (t + 1) * tq)
    p = pl.program_id(0)
    i = pl.program_id(1) * tiles + t
    first = first_ref[p * (pl.num_programs(1) * tiles) + i]
    n_sub = tq // tk
    n_diag = tq // tf

    acc_ref[...] = jnp.zeros(acc_ref.shape, F32)

    def kv_rows(start, size):
        start = pl.multiple_of(start, tf)
        vj = jnp.concatenate([v_ref[pl.ds(start, size), :], jnp.ones((size, LANES), BF16)], axis=1)
        return k_ref[pl.ds(start, size), :], vj

    def causal_keep(j):
        q_pos = i * tq + lax.broadcasted_iota(jnp.int32, (tq, tk), 0)
        k_pos = j * tk + lax.broadcasted_iota(jnp.int32, (tq, tk), 1)
        return k_pos <= q_pos

    def fast_tile(j):
        kj, vj = kv_rows(j * tf, tf)
        for hh in range(2):
            s = _dot_nt(q_ref[0, hh, rows, :], kj)
            acc_ref[hh] += _dot(jnp.exp(s).astype(BF16), vj)

    def slow_tile(j, masked):
        kj, vj = kv_rows(j * tk, tk)
        for hh in range(2):
            ck = c_ref[j, pl.ds(2 * p + hh, 1), :]
            s = _dot_nt(q_ref[0, hh, rows, 0:LANES], kj[:, 0:LANES]) - ck
            if masked:
                s = jnp.where(causal_keep(j), s, -jnp.inf)
            m_prev = m_ref[hh]
            m_new = jnp.maximum(m_prev, jnp.max(s, axis=-1, keepdims=True))
            alpha = jnp.exp(m_prev - m_new)
            pr = jnp.exp(s - m_new)
            acc_ref[hh] = alpha * acc_ref[hh] + _dot(pr.astype(BF16), vj)
            m_ref[hh] = m_new

    def run(tile):
        def body(j, carry):
            tile(j, False)
            return carry
        lax.fori_loop(first // (tk // tf), i * n_sub, body, 0)
        for d in range(n_sub):
            tile(i * n_sub + d, True)

    def fast_tail(n_back):
        span = n_back * tf + tq
        ks, vs = kv_rows(i * tq - n_back * tf, span)
        q_rel = n_back * tf + lax.broadcasted_iota(jnp.int32, (tq, span), 0)
        keep = lax.broadcasted_iota(jnp.int32, (tq, span), 1) <= q_rel
        for hh in range(2):
            s = jnp.where(keep, _dot_nt(q_ref[0, hh, rows, :], ks), -jnp.inf)
            acc_ref[hh] += _dot(jnp.exp(s).astype(BF16), vs)

    def run_fast():
        def body(j, carry):
            fast_tile(j)
            return carry
        lax.fori_loop(first, jnp.maximum(first, i * n_diag - FAST_TAIL_BACK), body, 0)
        n_short = -(-FAST_TAIL_BACK // n_diag)
        for i_short in range(n_short):
            pl.when(i == i_short)(functools.partial(fast_tail, i_short * n_diag))
        pl.when(i >= n_short)(functools.partial(fast_tail, FAST_TAIL_BACK))

    @pl.when(fast_ref[0] == 1)
    def _():
        run_fast()

    @pl.when(fast_ref[0] != 1)
    def _():
        m_ref[...] = jnp.full(m_ref.shape, -jnp.inf, F32)
        run(slow_tile)

    acc_e = acc_ref[0]
    acc_o = acc_ref[1]
    is_even = _lane_iota((tq, LANES)) < HEAD_DIM
    o = jnp.where(is_even, acc_e[:, 0:LANES] / acc_e[:, LANES:2 * LANES],
                  acc_o[:, 0:LANES] / acc_o[:, LANES:2 * LANES])
    o2 = o * o
    ss_e = jnp.sum(jnp.where(is_even, o2, 0.0), axis=-1, keepdims=True)
    ss_o = jnp.sum(jnp.where(is_even, 0.0, o2), axis=-1, keepdims=True)
    ms = jnp.where(is_even, ss_e, ss_o) * (1.0 / HEAD_DIM)
    z = z_ref[rows, :]
    o_ref[rows, :] = (o * lax.rsqrt(ms + EPS) * nw_ref[...] * (z * _sigmoid(z))).astype(o_ref.dtype)


SKIP_LOG = 110.0


def _first_key_tile(crange, bmax, *, tq, tk):
    n_k = crange.shape[0]
    n_sub = tq // tk
    c_max = crange[:, 0, SMALL_F:SMALL_F + N_HEADS].T
    c_min = crange[:, 1, SMALL_F:SMALL_F + N_HEADS].T
    cq_max = c_max.reshape(N_HEADS, n_k // n_sub, n_sub).max(axis=-1)
    bound = bmax[0, SMALL_BOUND:SMALL_BOUND + N_HEADS]
    gap = cq_max[:, :, None] - c_min[:, None, :] + 2.0 * bound[:, None, None]
    dead = gap <= -SKIP_LOG
    tile_idx = lax.broadcasted_iota(jnp.int32, dead.shape, 2)
    first = jnp.min(jnp.where(dead, n_k, tile_idx), axis=-1)
    first = jnp.min(first.reshape(N_PAIRS, 2, -1), axis=1)
    first = jnp.minimum(first, jnp.arange(first.shape[1], dtype=jnp.int32) * n_sub)
    return first.reshape(-1).astype(jnp.int32)


def _fox(fast, first, qa, ka, vb, ct, proj, nw_pair, *, tq, tk):
    s = vb.shape[0]
    step_rows = FOX_Q_TILES_PER_STEP * tq
    grid_spec = pltpu.PrefetchScalarGridSpec(
        num_scalar_prefetch=2,
        grid=(N_PAIRS, s // step_rows),
        in_specs=[
            pl.BlockSpec((1, 2, step_rows, 2 * LANES), lambda p, i, *_: (p, 0, i, 0)),
            pl.BlockSpec((s, 2 * LANES), lambda p, i, *_: (0, p)),
            pl.BlockSpec((s, LANES), lambda p, i, *_: (0, p)),
            pl.BlockSpec(ct.shape, lambda p, i, *_: (0, 0, 0)),
            pl.BlockSpec((step_rows, LANES), lambda p, i, *_: (i, COL_FOX_Z // LANES + p)),
            pl.BlockSpec((1, LANES), lambda p, i, *_: (0, 0)),
        ],
        out_specs=pl.BlockSpec((step_rows, LANES), lambda p, i, *_: (i, p)),
        scratch_shapes=[
            pltpu.VMEM((2, tq, 1), F32),
            pltpu.VMEM((2, tq, 2 * LANES), F32),
        ],
    )
    return pl.pallas_call(
        functools.partial(_fox_kernel, tq=tq, tk=tk, tf=FOX_TF),
        grid_spec=grid_spec,
        out_shape=jax.ShapeDtypeStruct((s, QK_W), BF16),
        compiler_params=pltpu.CompilerParams(
            dimension_semantics=("arbitrary", "arbitrary"), vmem_limit_bytes=VMEM_LIMIT),
        name="fox",
    )(fast, first, qa, ka, vb, ct, proj, nw_pair)


def _merge_kernel(x_ref, ya_ref, yb_ref, ga_ref, gb_ref, pa_ref, pb_ref, wo_ref, fw_ref, o_ref):
    ua = _dot(ya_ref[...].astype(BF16), pa_ref[...])
    ub = _dot(yb_ref[...].astype(BF16), pb_ref[...])
    merged = ga_ref[...].astype(F32) * ua + gb_ref[...].astype(F32) * ub
    xn = x_ref[...] + _dot(merged.astype(BF16), wo_ref[...])
    ms = jnp.mean(xn * xn, axis=-1, keepdims=True)
    o_ref[...] = xn * lax.rsqrt(ms + EPS) * fw_ref[...]


def _merge(x, ya, yb, gates, w_up_a, w_up_b, w_out, out_w, *, tm=512):
    s, d = x.shape
    const = lambda shape: pl.BlockSpec(shape, lambda i: (0,) * len(shape))
    return pl.pallas_call(
        _merge_kernel,
        grid=(s // tm,),
        in_specs=[
            pl.BlockSpec((tm, d), lambda i: (i, 0)),
            pl.BlockSpec((tm, QK_W), lambda i: (i, 0)),
            pl.BlockSpec((tm, QK_W), lambda i: (i, 0)),
            pl.BlockSpec((tm, d), lambda i: (i, 0)),
            pl.BlockSpec((tm, d), lambda i: (i, 1)),
            const((QK_W, d)),
            const((QK_W, d)),
            const((d, d)),
            const((1, d)),
        ],
        out_specs=pl.BlockSpec((tm, d), lambda i: (i, 0)),
        out_shape=jax.ShapeDtypeStruct((s, d), F32),
        compiler_params=pltpu.CompilerParams(
            dimension_semantics=("arbitrary",), vmem_limit_bytes=VMEM_LIMIT),
        name="merge",
    )(x, ya, yb, gates, gates, w_up_a, w_up_b, w_out, out_w)


def _block_diag_ones(n, blk):
    idx = np.arange(n) // blk
    return jnp.asarray(idx[:, None] == idx[None, :], dtype=BF16)


def _lower_tri_ones(n, blk=None):
    idx = np.arange(n)
    tri = idx[None, :] <= idx[:, None]
    if blk is not None:
        tri = tri & (idx[None, :] // blk == idx[:, None] // blk)
    return jnp.asarray(tri, dtype=BF16)


def _expand_matrix():
    e = np.zeros((LANES, QK_W + N_HEADS * LANES), np.float32)
    for h in range(N_HEADS):
        e[SMALL_B + h, h * HEAD_DIM:(h + 1) * HEAD_DIM] = 1.0
        e[SMALL_A + h, QK_W + h * LANES:QK_W + (h + 1) * LANES] = 1.0
    return jnp.asarray(e, dtype=BF16)


def _pad_lanes(vec, offset):
    return jnp.zeros((1, LANES), F32).at[0, offset:offset + vec.shape[0]].set(vec.astype(F32))


def _regroup_w_in(w):
    d = w.shape[0]
    o_z = 3 * QK_W
    o_b = o_z + QK_W
    o_a = o_b + N_HEADS
    o_fq = o_a + N_HEADS
    o_ff = o_fq + 4 * QK_W
    o_gate = o_ff + N_HEADS
    small = jnp.concatenate(
        [w[:, o_b:o_a], w[:, o_a:o_fq], w[:, o_ff:o_gate],
         jnp.zeros((d, LANES - 3 * N_HEADS), w.dtype)], axis=1)
    return jnp.concatenate([w[:, 0:o_b], w[:, o_fq:o_ff], small, w[:, o_gate:]], axis=1).astype(BF16)


def kernel(x, norm_w, w_in, gate_b, conv_w, a_log, dt_bias, gdn_norm_w, f_bias,
           fox_qn_w, fox_kn_w, fox_on_w, w_up_gdn, w_up_fox, w_out, final_norm_w):
    batch, s, d = x.shape
    assert norm_w.shape[0] == 1, "the merge call fuses the final RMSNorm: single-layer trunk only"
    assert s % (FOX_TQ * FOX_Q_TILES_PER_STEP) == 0 and FOX_TQ % FOX_TK == 0 and FOX_TK % FOX_TF == 0
    assert s % GDN_ROWS == 0 and GDN_ROWS % GDN_CHUNK == 0
    assert w_in.shape[2] == P_WIDTH - LANES + 3 * N_HEADS + gate_b.shape[1]
    l = 0

    bd = _block_diag_ones(QK_W, HEAD_DIM)
    tri_gdn = _lower_tri_ones(GDN_ROWS, GDN_CHUNK)
    tri_fox = _lower_tri_ones(FOX_TK)
    expand = _expand_matrix()
    tile_heads = lambda w: jnp.tile(w.astype(F32), N_HEADS)[None, :]

    outs = []
    x_rows = x.reshape(batch * s, d)
    for b in range(batch):
        xb = x_rows if batch == 1 else x_rows[b * s:(b + 1) * s]
        proj, gates = _proj(xb, norm_w[l][None, :], _regroup_w_in(w_in.reshape(w_in.shape[1:])),
                            gate_b[l][None, :].astype(F32))
        ya = _gdn(proj, conv_w[l], _pad_lanes(a_log[l], SMALL_A), _pad_lanes(dt_bias[l], SMALL_A),
                  tile_heads(gdn_norm_w[l]), bd, tri_gdn, expand)
        qa, ka, vb, ct, bmax, crange = _foxprep(
            proj, tile_heads(fox_qn_w[l]), tile_heads(fox_kn_w[l]),
            _pad_lanes(f_bias[l], SMALL_F), bd, tri_fox, rows=FOX_TK)
        bound_max = jnp.max(bmax[0, SMALL_BOUND:SMALL_BOUND + N_HEADS])
        fast = (bound_max <= FAST_BOUND_MAX).astype(jnp.int32).reshape(1)
        first = _first_key_tile(crange.reshape(s // FOX_TF, 2, LANES), bmax, tq=FOX_TQ, tk=FOX_TF)
        yb = _fox(fast, first, qa, ka, vb, ct, proj,
                  jnp.tile(fox_on_w[l].astype(F32), 2)[None, :], tq=FOX_TQ, tk=FOX_TK)
        outs.append(_merge(xb, ya, yb, gates,
                           w_up_gdn[l].astype(BF16), w_up_fox[l].astype(BF16),
                           w_out[l].astype(BF16), final_norm_w[None, :].astype(F32)))
    out = outs[0] if batch == 1 else jnp.concatenate(outs, axis=0)
    return out.reshape(batch, s, d)
```

```python
import functools

import numpy as np
import jax
import jax.numpy as jnp
from jax import lax
from jax.experimental import pallas as pl
from jax.experimental.pallas import tpu as pltpu

F32 = jnp.float32
BF16 = jnp.bfloat16

EPS = 1e-6
LANES = 128
HEAD_DIM = 64
N_HEADS = 8
N_PAIRS = N_HEADS // 2
QK_W = N_HEADS * HEAD_DIM
CONV_K = 4
GDN_CHUNK = 128
GDN_ROWS = 256
FOX_TQ = 512
FOX_Q_TILES_PER_STEP = 2
FOX_TK = 512
FOX_TF = 256
VMEM_LIMIT = 56 * 1024 * 1024

COL_GDN_QKV = 0
COL_GDN_Z = 1536
COL_FOX_Q = 2048
COL_FOX_K = 2560
COL_FOX_V = 3072
COL_FOX_Z = 3584
COL_SMALL = 4096
P_WIDTH = 4224
SMALL_B, SMALL_A, SMALL_F = 0, 8, 16


def _dot(a, b):
    return jnp.dot(a, b, preferred_element_type=F32)


def _dot_nt(a, b):
    return lax.dot_general(a, b, (((1,), (1,)), ((), ())), preferred_element_type=F32)


def _split3(x):
    hi = x.astype(BF16)
    r = x - hi.astype(F32)
    mid = r.astype(BF16)
    lo = (r - mid.astype(F32)).astype(BF16)
    return hi, mid, lo


def _dot_sel_rhs(x, sel):
    hi, mid, lo = _split3(x)
    return _dot(hi, sel) + _dot(mid, sel) + _dot(lo, sel)


def _dot_sel_lhs(sel, x):
    hi, mid, lo = _split3(x)
    return _dot(sel, hi) + _dot(sel, mid) + _dot(sel, lo)


def _head_sumsq(x, bd):
    return _dot((x * x).astype(BF16), bd)


def _sigmoid(x):
    return 1.0 / (1.0 + jnp.exp(-x))


def _softplus(x):
    return jnp.maximum(x, 0.0) + jnp.log1p(jnp.exp(-jnp.abs(x)))


def _lane_iota(shape):
    return lax.broadcasted_iota(jnp.int32, shape, len(shape) - 1)


def _proj_kernel(x_ref, nw_ref, wa_ref, wb_ref, ws_ref, wg_ref, gb_ref, o_ref, g_ref, *, n_chunk):
    x = x_ref[...]
    ms = jnp.mean(x * x, axis=-1, keepdims=True)
    h = (x * lax.rsqrt(ms + EPS) * nw_ref[...]).astype(BF16)
    col = 0
    for w_ref in (wa_ref, wb_ref, ws_ref):
        for c0 in range(0, w_ref.shape[1], n_chunk):
            c1 = min(c0 + n_chunk, w_ref.shape[1])
            o_ref[:, col + c0:col + c1] = _dot(h, w_ref[:, c0:c1])
        col += w_ref.shape[1]
    for c0 in range(0, g_ref.shape[1], n_chunk):
        c1 = c0 + n_chunk
        logit = _dot(h, wg_ref[:, c0:c1]) + gb_ref[:, c0:c1]
        g_ref[:, c0:c1] = _sigmoid(logit).astype(g_ref.dtype)


def _proj(x, norm_w, w_groups, gate_b, *, tm=512):
    s, d = x.shape
    w_gdn, w_fox, w_small, w_gate = w_groups
    gate_w = gate_b.shape[1]
    width = w_gdn.shape[1] + w_fox.shape[1] + w_small.shape[1]
    assert width == P_WIDTH and w_gate.shape[1] == gate_w
    const = lambda shape: pl.BlockSpec(shape, lambda i: (0,) * len(shape))
    return pl.pallas_call(
        functools.partial(_proj_kernel, n_chunk=1024),
        grid=(s // tm,),
        in_specs=[
            pl.BlockSpec((tm, d), lambda i: (i, 0)),
            const((1, d)),
            const(w_gdn.shape), const(w_fox.shape), const(w_small.shape), const(w_gate.shape),
            const((1, gate_w)),
        ],
        out_specs=[pl.BlockSpec((tm, width), lambda i: (i, 0)),
                   pl.BlockSpec((tm, gate_w), lambda i: (i, 0))],
        out_shape=[jax.ShapeDtypeStruct((s, width), F32),
                   jax.ShapeDtypeStruct((s, gate_w), BF16)],
        compiler_params=pltpu.CompilerParams(
            dimension_semantics=("arbitrary",), vmem_limit_bytes=VMEM_LIMIT),
        name="proj",
    )(x, norm_w, w_gdn, w_fox, w_small, w_gate, gate_b)


INV_BASE_LOG2 = 5


def _shr(x, n):
    return lax.shift_right_logical(x, n)


def _inverse_masks(row, col):
    c = row.shape[0]
    lb = INV_BASE_LOG2
    diag = _shr(row, lb) == _shr(col, lb)
    levels = []
    b = lb
    while (1 << b) < c:
        in_pair = _shr(row, b + 1) == _shr(col, b + 1)
        levels.append(in_pair & ((_shr(row, b) & 1) == 1) & ((_shr(col, b) & 1) == 0))
        b += 1
    return diag, levels


def _unit_lower_inverses(ms, eye, masks):
    c = ms[0].shape[0]
    diag, levels = masks
    mds = [jnp.where(diag, m, 0.0) for m in ms]
    ts = [eye - md for md in mds]
    mps = [_dot(md.astype(BF16), md.astype(BF16)) for md in mds]
    n_sq = INV_BASE_LOG2 - 1
    for it in range(n_sq):
        if it < n_sq - 1:
            boths = [_dot(jnp.concatenate([t, mp], axis=0).astype(BF16), mp.astype(BF16))
                     for t, mp in zip(ts, mps)]
            ts = [t + both[0:c, :] for t, both in zip(ts, boths)]
            mps = [both[c:2 * c, :] for both in boths]
        else:
            ts = [t + _dot(t.astype(BF16), mp.astype(BF16)) for t, mp in zip(ts, mps)]
    b = 1 << INV_BASE_LOG2
    for level in levels:
        odd_rows = lambda x: jnp.concatenate([x[s:s + b, :] for s in range(b, c, 2 * b)], axis=0)
        t_odd = [odd_rows(t) for t in ts]
        tls = [_dot(to.astype(BF16), jnp.where(level, m, 0.0).astype(BF16))
               for to, m in zip(t_odd, ms)]
        new_odd = [to - _dot(tl.astype(BF16), t.astype(BF16)) for to, tl, t in zip(t_odd, tls, ts)]
        ts = [jnp.concatenate(
            [no[(blk // 2) * b:(blk // 2 + 1) * b, :] if blk % 2 else t[blk * b:(blk + 1) * b, :]
             for blk in range(c // b)], axis=0) for t, no in zip(ts, new_odd)]
        b *= 2
    return ts


def _gdn_kernel(qkv_ref, z_ref, sm_ref, cw_ref, alog_ref, dtb_ref, nw_ref,
                bd_ref, tri_ref, ex_ref, o_ref, xext_ref, state_ref):
    c = GDN_CHUNK
    r = qkv_ref.shape[0]
    chunks = [slice(ci * c, (ci + 1) * c) for ci in range(r // c)]
    step = pl.program_id(0)

    @pl.when(step == 0)
    def _():
        xext_ref[0:8, :] = jnp.zeros((8, xext_ref.shape[1]), F32)
        state_ref[...] = jnp.zeros(state_ref.shape, F32)

    xext_ref[8:8 + r, :] = qkv_ref[...]
    y = cw_ref[3:4, :] * xext_ref[8:8 + r, :]
    for j in range(CONV_K - 1):
        y = y + cw_ref[j:j + 1, :] * xext_ref[5 + j:5 + j + r, :]
    xext_ref[0:8, :] = qkv_ref[r - 8:r, :]
    y = y * _sigmoid(y)

    bd = bd_ref[...]
    q = y[:, 0:QK_W]
    k = y[:, QK_W:2 * QK_W]
    v = y[:, 2 * QK_W:3 * QK_W]
    q = q * lax.rsqrt(_head_sumsq(q, bd) + EPS) * (HEAD_DIM ** -0.5)
    k = k * lax.rsqrt(_head_sumsq(k, bd) + EPS)

    sm = sm_ref[...]
    beta_s = _sigmoid(sm)
    g_s = -jnp.exp(alog_ref[...]) * _softplus(sm + dtb_ref[...])
    g_cum = _dot_sel_lhs(tri_ref[...], g_s)
    beta = _dot(beta_s.astype(BF16), ex_ref[:, 0:QK_W])
    g_bc = _dot_sel_rhs(g_cum, ex_ref[:, QK_W:QK_W + N_HEADS * LANES])
    even_r = _lane_iota((r, LANES)) < HEAD_DIM
    g = jnp.concatenate(
        [jnp.where(even_r, g_bc[:, 2 * p * LANES:(2 * p + 1) * LANES],
                   g_bc[:, (2 * p + 1) * LANES:(2 * p + 2) * LANES]) for p in range(N_PAIRS)], axis=1)
    g_t = g_cum.T

    eg = jnp.exp(g)
    kb = k * beta
    vb = v * beta
    kbg = kb * eg
    qg = q * eg
    g_last = [g[rows.stop - 1:rows.stop, :] for rows in chunks]
    e_last = [jnp.exp(gl) for gl in g_last]
    kt = [k[rows] * jnp.exp(gl - g[rows]) for rows, gl in zip(chunks, g_last)]

    row = lax.broadcasted_iota(jnp.int32, (c, c), 0)
    col = lax.broadcasted_iota(jnp.int32, (c, c), 1)
    causal = col <= row
    strict = col < row
    eye = (col == row).astype(F32)
    inv_masks = _inverse_masks(row, col)
    lane = _lane_iota((c, LANES))
    is_even = lane < HEAD_DIM
    bd_mask = (lane < HEAD_DIM) == (lax.broadcasted_iota(jnp.int32, (c, LANES), 0) < HEAD_DIM)

    pair_lanes = [slice(p * LANES, (p + 1) * LANES) for p in range(N_PAIRS)]
    odd = jnp.logical_not(is_even)

    kq = []
    for rows in chunks:
        for ls in pair_lanes:
            kb_p, q_p = kb[rows, ls], q[rows, ls]
            lhs = jnp.concatenate([jnp.where(is_even, kb_p, 0.0), jnp.where(odd, kb_p, 0.0),
                                   jnp.where(is_even, q_p, 0.0), jnp.where(odd, q_p, 0.0)], axis=0)
            kq.append(_dot_nt(lhs.astype(BF16), k[rows, ls].astype(BF16)))
    ms, a_h = [], []
    for ci, rows in enumerate(chunks):
        for h in range(N_HEADS):
            p, hh = divmod(h, 2)
            kq_p = kq[ci * N_PAIRS + p]
            diff = g_bc[rows, h * LANES:(h + 1) * LANES] - g_t[SMALL_A + h:SMALL_A + h + 1, rows]
            dec = jnp.where(causal, jnp.exp(jnp.where(causal, diff, 0.0)), 0.0)
            ms.append(jnp.where(strict, kq_p[hh * c:(hh + 1) * c, :] * dec, 0.0))
            a_h.append((kq_p[(2 + hh) * c:(3 + hh) * c, :] * dec).astype(BF16))
    ts = _unit_lower_inverses(ms, eye, inv_masks)
    wk = []
    for ci, rows in enumerate(chunks):
        rhs = [jnp.concatenate([vb[rows, ls], kbg[rows, ls]], axis=1).astype(BF16)
               for ls in pair_lanes]
        wk += [_dot(ts[ci * N_HEADS + h].astype(BF16), rhs[h // 2]) for h in range(N_HEADS)]

    sts = [state_ref[p] for p in range(N_PAIRS)]
    for ci, rows in enumerate(chunks):
        stb = [st.astype(BF16) for st in sts]
        ub = []
        for p in range(N_PAIRS):
            wk_e, wk_o = wk[ci * N_HEADS + 2 * p], wk[ci * N_HEADS + 2 * p + 1]
            w_pair = jnp.where(is_even, wk_e[:, 0:LANES], wk_o[:, 0:LANES])
            kc_pair = jnp.where(is_even, wk_e[:, LANES:2 * LANES], wk_o[:, LANES:2 * LANES])
            ub.append((w_pair - _dot(kc_pair.astype(BF16), stb[p])).astype(BF16))
        for p, ls in enumerate(pair_lanes):
            a_e, a_o = a_h[ci * N_HEADS + 2 * p], a_h[ci * N_HEADS + 2 * p + 1]
            o = _dot(qg[rows, ls].astype(BF16), stb[p]) + jnp.where(
                is_even, _dot(a_e, ub[p]), _dot(a_o, ub[p]))
            upd = _dot(kt[ci][:, ls].T.astype(BF16), ub[p])
            sts[p] = sts[p] * e_last[ci][:, ls] + jnp.where(bd_mask, upd, 0.0)
            o_ref[rows, ls] = o.astype(o_ref.dtype)
    for p in range(N_PAIRS):
        state_ref[p] = sts[p]

    o_all = o_ref[...].astype(F32)
    z = z_ref[...]
    ms = _head_sumsq(o_all, bd) * (1.0 / HEAD_DIM)
    yn = o_all * lax.rsqrt(ms + EPS) * nw_ref[...]
    o_ref[...] = (yn * (z * _sigmoid(z))).astype(o_ref.dtype)


def _gdn(proj, conv_w, a_log_pad, dtb_pad, nw_tiled, bd, tri, expand):
    s = proj.shape[0]
    c = tri.shape[0]
    conv_width = 3 * QK_W
    const = lambda shape: pl.BlockSpec(shape, lambda i: (0,) * len(shape))
    return pl.pallas_call(
        _gdn_kernel,
        grid=(s // c,),
        in_specs=[
            pl.BlockSpec((c, conv_width), lambda i: (i, COL_GDN_QKV // conv_width)),
            pl.BlockSpec((c, QK_W), lambda i: (i, COL_GDN_Z // QK_W)),
            pl.BlockSpec((c, LANES), lambda i: (i, COL_SMALL // LANES)),
            const((CONV_K, conv_width)),
            const((1, LANES)),
            const((1, LANES)),
            const((1, QK_W)),
            const((QK_W, QK_W)),
            const((c, c)),
            const(expand.shape),
        ],
        out_specs=pl.BlockSpec((c, QK_W), lambda i: (i, 0)),
        out_shape=jax.ShapeDtypeStruct((s, QK_W), F32),
        scratch_shapes=[
            pltpu.VMEM((c + 8, conv_width), F32),
            pltpu.VMEM((N_PAIRS, LANES, LANES), F32),
        ],
        compiler_params=pltpu.CompilerParams(
            dimension_semantics=("arbitrary",), vmem_limit_bytes=VMEM_LIMIT),
        name="gdn",
    )(proj, proj, proj, conv_w, a_log_pad, dtb_pad, nw_tiled, bd, tri, expand)


X_C, X_EVEN, X_ODD, X_BOUND = 0, 3, 6, 9
SMALL_BOUND = 24
TERM_LANE_STEP = 16
N_EXTRA_COLS = (N_HEADS + N_PAIRS) * LANES


def _extras_selector():
    sel = np.zeros((LANES, N_EXTRA_COLS), np.float32)
    pat = np.zeros((1, N_EXTRA_COLS), np.float32)
    k0 = N_HEADS * LANES
    for t in range(3):
        r0 = t * TERM_LANE_STEP
        for h in range(N_HEADS):
            sel[r0 + SMALL_F + h, h * LANES + X_C + t] = 1.0
            sel[r0 + SMALL_BOUND + h, h * LANES + X_BOUND + t] = 1.0
            pat[0, h * LANES + (X_EVEN if h % 2 == 0 else X_ODD) + t] = 1.0
        for p in range(N_PAIRS):
            sel[r0 + SMALL_F + 2 * p, k0 + p * LANES + X_EVEN + t] = -1.0
            sel[r0 + SMALL_F + 2 * p + 1, k0 + p * LANES + X_ODD + t] = -1.0
            pat[0, k0 + p * LANES + X_C + t] = 1.0
            pat[0, k0 + p * LANES + X_BOUND + t] = 1.0
    return jnp.asarray(sel, dtype=BF16), jnp.asarray(pat, dtype=F32)


def _sumsq_selector():
    sel = np.zeros((QK_W, LANES), np.float32)
    for h in range(N_HEADS):
        sel[h * HEAD_DIM:(h + 1) * HEAD_DIM, SMALL_BOUND + h] = 1.0
    return jnp.asarray(sel, dtype=BF16)


def _foxprep_kernel(q_ref, k_ref, v_ref, sm_ref, qw_ref, kw_ref, fb_ref, bd_ref, tri_ref, ss_ref,
                    sel_ref, pat_ref,
                    qa_ref, ka_ref, vo_ref, ct_ref, bmax_ref, crange_ref, carry_ref, kmax_ref):
    @pl.when(pl.program_id(0) == 0)
    def _():
        carry_ref[...] = jnp.zeros(carry_ref.shape, F32)
        kmax_ref[...] = jnp.zeros(kmax_ref.shape, F32)
        bmax_ref[...] = jnp.zeros(bmax_ref.shape, F32)

    bd = bd_ref[...]

    def headnorm(x, w):
        ms = _head_sumsq(x, bd) * (1.0 / HEAD_DIM)
        return x * lax.rsqrt(ms + EPS) * w

    qb = (headnorm(q_ref[...], qw_ref[...]) * (HEAD_DIM ** -0.5)).astype(BF16)
    kb = headnorm(k_ref[...], kw_ref[...]).astype(BF16)
    vo_ref[...] = v_ref[...].astype(vo_ref.dtype)
    rows = qb.shape[0]

    xf = sm_ref[...] + fb_ref[...]
    log_f = jnp.minimum(xf, 0.0) - jnp.log1p(jnp.exp(-jnp.abs(xf)))
    cum = carry_ref[...] + _dot_sel_lhs(tri_ref[...], log_f)
    carry_ref[...] = cum[rows - 1:rows, :]
    ct_ref[0] = cum.T[SMALL_F:SMALL_F + N_HEADS, :]
    for f in range(rows // FOX_TF):
        seg = cum[f * FOX_TF:(f + 1) * FOX_TF, :]
        crange_ref[0, 2 * f:2 * f + 1, :] = jnp.max(seg, axis=0, keepdims=True)
        crange_ref[0, 2 * f + 1:2 * f + 2, :] = jnp.min(seg, axis=0, keepdims=True)

    qf = qb.astype(F32)
    kf = kb.astype(F32)
    ss = ss_ref[...]
    kss = _dot((kf * kf).astype(BF16), ss)
    kmax = jnp.maximum(kmax_ref[...], jnp.max(kss, axis=0, keepdims=True))
    kmax_ref[...] = kmax
    bound = jnp.sqrt(_dot((qf * qf).astype(BF16), ss) * kmax)
    bmax_ref[...] = jnp.maximum(bmax_ref[...], jnp.max(bound, axis=0, keepdims=True))

    lane = _lane_iota((rows, LANES))
    hi, mid, lo = _split3(jnp.where(lane < SMALL_BOUND, cum, -bound))
    first_mid = SMALL_F + TERM_LANE_STEP
    terms = jnp.where(lane < first_mid, hi.astype(F32), jnp.where(
        lane < first_mid + TERM_LANE_STEP, pltpu.roll(mid.astype(F32), TERM_LANE_STEP, 1),
        pltpu.roll(lo.astype(F32), 2 * TERM_LANE_STEP, 1)))
    extras = (_dot(terms.astype(BF16), sel_ref[...]) + pat_ref[...]).astype(BF16)
    is_even = lane < HEAD_DIM
    for p in range(N_PAIRS):
        ls = slice(p * LANES, (p + 1) * LANES)
        for hh in range(2):
            h = 2 * p + hh
            keep = is_even if hh == 0 else jnp.logical_not(is_even)
            qa_ref[p, hh, :, 0:LANES] = jnp.where(keep, qb[:, ls], jnp.zeros_like(qb[:, ls]))
            qa_ref[p, hh, :, LANES:2 * LANES] = extras[:, h * LANES:(h + 1) * LANES]
        ka_ref[:, 2 * p * LANES:(2 * p + 1) * LANES] = kb[:, ls]
        ka_ref[:, (2 * p + 1) * LANES:(2 * p + 2) * LANES] = extras[
            :, (N_HEADS + p) * LANES:(N_HEADS + p + 1) * LANES]


def _foxprep(proj, qw_tiled, kw_tiled, fb_pad, bd, tri, *, rows):
    s = proj.shape[0]
    const = lambda shape: pl.BlockSpec(shape, lambda i: (0,) * len(shape))
    ss_sel = _sumsq_selector()
    ex_sel, ex_pat = _extras_selector()
    return pl.pallas_call(
        _foxprep_kernel,
        grid=(s // rows,),
        in_specs=[
            pl.BlockSpec((rows, QK_W), lambda i: (i, COL_FOX_Q // QK_W)),
            pl.BlockSpec((rows, QK_W), lambda i: (i, COL_FOX_K // QK_W)),
            pl.BlockSpec((rows, QK_W), lambda i: (i, COL_FOX_V // QK_W)),
            pl.BlockSpec((rows, LANES), lambda i: (i, COL_SMALL // LANES)),
            const((1, QK_W)),
            const((1, QK_W)),
            const((1, LANES)),
            const((QK_W, QK_W)),
            const((rows, rows)),
            const(ss_sel.shape),
            const(ex_sel.shape),
            const(ex_pat.shape),
        ],
        out_specs=[
            pl.BlockSpec((N_PAIRS, 2, rows, 2 * LANES), lambda i: (0, 0, i, 0)),
            pl.BlockSpec((rows, 2 * QK_W), lambda i: (i, 0)),
            pl.BlockSpec((rows, QK_W), lambda i: (i, 0)),
            pl.BlockSpec((1, N_HEADS, rows), lambda i: (i, 0, 0)),
            pl.BlockSpec((1, LANES), lambda i: (0, 0)),
            pl.BlockSpec((1, 2 * (rows // FOX_TF), LANES), lambda i: (i, 0, 0)),
        ],
        out_shape=[
            jax.ShapeDtypeStruct((N_PAIRS, 2, s, 2 * LANES), BF16),
            jax.ShapeDtypeStruct((s, 2 * QK_W), BF16),
            jax.ShapeDtypeStruct((s, QK_W), BF16),
            jax.ShapeDtypeStruct((s // rows, N_HEADS, rows), F32),
            jax.ShapeDtypeStruct((1, LANES), F32),
            jax.ShapeDtypeStruct((s // rows, 2 * (rows // FOX_TF), LANES), F32),
        ],
        scratch_shapes=[pltpu.VMEM((1, LANES), F32), pltpu.VMEM((1, LANES), F32)],
        compiler_params=pltpu.CompilerParams(
            dimension_semantics=("arbitrary",), vmem_limit_bytes=VMEM_LIMIT),
        name="foxprep",
    )(proj, proj, proj, proj, qw_tiled, kw_tiled, fb_pad, bd, tri, ss_sel, ex_sel, ex_pat)


FAST_BOUND_MAX = 40.0
FAST_TAIL_BACK = 3


def _fox_kernel(fast_ref, first_ref, q_ref, k_ref, v_ref, c_ref, z_ref, nw_ref, o_ref, m_ref,
                acc_ref, *, tq, tk, tf):
    for t in range(q_ref.shape[2] // tq):
        _fox_query_tile(t, fast_ref, first_ref, q_ref, k_ref, v_ref, c_ref, z_ref, nw_ref, o_ref,
                        m_ref, acc_ref, tq=tq, tk=tk, tf=tf)


def _fox_query_tile(t, fast_ref, first_ref, q_ref, k_ref, v_ref, c_ref, z_ref, nw_ref, o_ref, m_ref,
                    acc_ref, *, tq, tk, tf):
    tiles = q_ref.shape[2] // tq
    rows = slice(t * tq, (t + 1) * tq)
    p = pl.program_id(0)
    i = pl.program_id(1) * tiles + t
    first = first_ref[p * (pl.num_programs(1) * tiles) + i]
    n_sub = tq // tk
    n_diag = tq // tf

    acc_ref[...] = jnp.zeros(acc_ref.shape, F32)

    def kv_rows(start, size):
        start = pl.multiple_of(start, tf)
        vj = jnp.concatenate([v_ref[pl.ds(start, size), :], jnp.ones((size, LANES), BF16)], axis=1)
        return k_ref[pl.ds(start, size), :], vj

    def causal_keep(j):
        q_pos = i * tq + lax.broadcasted_iota(jnp.int32, (tq, tk), 0)
        k_pos = j * tk + lax.broadcasted_iota(jnp.int32, (tq, tk), 1)
        return k_pos <= q_pos

    def fast_tile(j):
        kj, vj = kv_rows(j * tf, tf)
        for hh in range(2):
            s = _dot_nt(q_ref[0, hh, rows, :], kj)
            acc_ref[hh] += _dot(jnp.exp(s).astype(BF16), vj)

    def slow_tile(j, masked):
        kj, vj = kv_rows(j * tk, tk)
        for hh in range(2):
            ck = c_ref[j, pl.ds(2 * p + hh, 1), :]
            s = _dot_nt(q_ref[0, hh, rows, 0:LANES], kj[:, 0:LANES]) - ck
            if masked:
                s = jnp.where(causal_keep(j), s, -jnp.inf)
            m_prev = m_ref[hh]
            m_new = jnp.maximum(m_prev, jnp.max(s, axis=-1, keepdims=True))
            alpha = jnp.exp(m_prev - m_new)
            pr = jnp.exp(s - m_new)
            acc_ref[hh] = alpha * acc_ref[hh] + _dot(pr.astype(BF16), vj)
            m_ref[hh] = m_new

    def run(tile):
        def body(j, carry):
            tile(j, False)
            return carry
        lax.fori_loop(first // (tk // tf), i * n_sub, body, 0)
        for d in range(n_sub):
            tile(i * n_sub + d, True)

    def fast_tail(n_back):
        span = n_back * tf + tq
        ks, vs = kv_rows(i * tq - n_back * tf, span)
        q_rel = n_back * tf + lax.broadcasted_iota(jnp.int32, (tq, span), 0)
        keep = lax.broadcasted_iota(jnp.int32, (tq, span), 1) <= q_rel
        for hh in range(2):
            s = jnp.where(keep, _dot_nt(q_ref[0, hh, rows, :], ks), -jnp.inf)
            acc_ref[hh] += _dot(jnp.exp(s).astype(BF16), vs)

    def run_fast():
        def body(j, carry):
            fast_tile(j)
            return carry
        lax.fori_loop(first, jnp.maximum(first, i * n_diag - FAST_TAIL_BACK), body, 0)
        n_short = -(-FAST_TAIL_BACK // n_diag)
        for i_short in range(n_short):
            pl.when(i == i_short)(functools.partial(fast_tail, i_short * n_diag))
        pl.when(i >= n_short)(functools.partial(fast_tail, FAST_TAIL_BACK))

    @pl.when(fast_ref[0] == 1)
    def _():
        run_fast()

    @pl.when(fast_ref[0] != 1)
    def _():
        m_ref[...] = jnp.full(m_ref.shape, -jnp.inf, F32)
        run(slow_tile)

    acc_e = acc_ref[0]
    acc_o = acc_ref[1]
    is_even = _lane_iota((tq, LANES)) < HEAD_DIM
    o = jnp.where(is_even, acc_e[:, 0:LANES] / acc_e[:, LANES:2 * LANES],
                  acc_o[:, 0:LANES] / acc_o[:, LANES:2 * LANES])
    o2 = o * o
    ss_e = jnp.sum(jnp.where(is_even, o2, 0.0), axis=-1, keepdims=True)
    ss_o = jnp.sum(jnp.where(is_even, 0.0, o2), axis=-1, keepdims=True)
    ms = jnp.where(is_even, ss_e, ss_o) * (1.0 / HEAD_DIM)
    z = z_ref[rows, :]
    o_ref[rows, :] = (o * lax.rsqrt(ms + EPS) * nw_ref[...] * (z * _sigmoid(z))).astype(o_ref.dtype)


SKIP_LOG = 110.0


def _first_key_tile(crange, bmax, *, tq, tk):
    n_k = crange.shape[0]
    n_sub = tq // tk
    c_max = crange[:, 0, SMALL_F:SMALL_F + N_HEADS].T
    c_min = crange[:, 1, SMALL_F:SMALL_F + N_HEADS].T
    cq_max = c_max.reshape(N_HEADS, n_k // n_sub, n_sub).max(axis=-1)
    bound = bmax[0, SMALL_BOUND:SMALL_BOUND + N_HEADS]
    gap = cq_max[:, :, None] - c_min[:, None, :] + 2.0 * bound[:, None, None]
    dead = gap <= -SKIP_LOG
    tile_idx = lax.broadcasted_iota(jnp.int32, dead.shape, 2)
    first = jnp.min(jnp.where(dead, n_k, tile_idx), axis=-1)
    first = jnp.min(first.reshape(N_PAIRS, 2, -1), axis=1)
    first = jnp.minimum(first, jnp.arange(first.shape[1], dtype=jnp.int32) * n_sub)
    return first.reshape(-1).astype(jnp.int32)


def _fox(fast, first, qa, ka, vb, ct, proj, nw_pair, *, tq, tk):
    s = vb.shape[0]
    step_rows = FOX_Q_TILES_PER_STEP * tq
    grid_spec = pltpu.PrefetchScalarGridSpec(
        num_scalar_prefetch=2,
        grid=(N_PAIRS, s // step_rows),
        in_specs=[
            pl.BlockSpec((1, 2, step_rows, 2 * LANES), lambda p, i, *_: (p, 0, i, 0)),
            pl.BlockSpec((s, 2 * LANES), lambda p, i, *_: (0, p)),
            pl.BlockSpec((s, LANES), lambda p, i, *_: (0, p)),
            pl.BlockSpec(ct.shape, lambda p, i, *_: (0, 0, 0)),
            pl.BlockSpec((step_rows, LANES), lambda p, i, *_: (i, COL_FOX_Z // LANES + p)),
            pl.BlockSpec((1, LANES), lambda p, i, *_: (0, 0)),
        ],
        out_specs=pl.BlockSpec((step_rows, LANES), lambda p, i, *_: (i, p)),
        scratch_shapes=[
            pltpu.VMEM((2, tq, 1), F32),
            pltpu.VMEM((2, tq, 2 * LANES), F32),
        ],
    )
    return pl.pallas_call(
        functools.partial(_fox_kernel, tq=tq, tk=tk, tf=FOX_TF),
        grid_spec=grid_spec,
        out_shape=jax.ShapeDtypeStruct((s, QK_W), BF16),
        compiler_params=pltpu.CompilerParams(
            dimension_semantics=("arbitrary", "arbitrary"), vmem_limit_bytes=VMEM_LIMIT),
        name="fox",
    )(fast, first, qa, ka, vb, ct, proj, nw_pair)


def _merge_kernel(x_ref, ya_ref, yb_ref, ga_ref, gb_ref, pa_ref, pb_ref, wo_ref, fw_ref, o_ref):
    ua = _dot(ya_ref[...].astype(BF16), pa_ref[...])
    ub = _dot(yb_ref[...].astype(BF16), pb_ref[...])
    merged = ga_ref[...].astype(F32) * ua + gb_ref[...].astype(F32) * ub
    xn = x_ref[...] + _dot(merged.astype(BF16), wo_ref[...])
    ms = jnp.mean(xn * xn, axis=-1, keepdims=True)
    o_ref[...] = xn * lax.rsqrt(ms + EPS) * fw_ref[...]


def _merge(x, ya, yb, gates, w_up_a, w_up_b, w_out, out_w, *, tm=512):
    s, d = x.shape
    const = lambda shape: pl.BlockSpec(shape, lambda i: (0,) * len(shape))
    return pl.pallas_call(
        _merge_kernel,
        grid=(s // tm,),
        in_specs=[
            pl.BlockSpec((tm, d), lambda i: (i, 0)),
            pl.BlockSpec((tm, QK_W), lambda i: (i, 0)),
            pl.BlockSpec((tm, QK_W), lambda i: (i, 0)),
            pl.BlockSpec((tm, d), lambda i: (i, 0)),
            pl.BlockSpec((tm, d), lambda i: (i, 1)),
            const((QK_W, d)),
            const((QK_W, d)),
            const((d, d)),
            const((1, d)),
        ],
        out_specs=pl.BlockSpec((tm, d), lambda i: (i, 0)),
        out_shape=jax.ShapeDtypeStruct((s, d), F32),
        compiler_params=pltpu.CompilerParams(
            dimension_semantics=("arbitrary",), vmem_limit_bytes=VMEM_LIMIT),
        name="merge",
    )(x, ya, yb, gates, gates, w_up_a, w_up_b, w_out, out_w)


def _block_diag_ones(n, blk):
    idx = np.arange(n) // blk
    return jnp.asarray(idx[:, None] == idx[None, :], dtype=BF16)


def _lower_tri_ones(n, blk=None):
    idx = np.arange(n)
    tri = idx[None, :] <= idx[:, None]
    if blk is not None:
        tri = tri & (idx[None, :] // blk == idx[:, None] // blk)
    return jnp.asarray(tri, dtype=BF16)


def _expand_matrix():
    e = np.zeros((LANES, QK_W + N_HEADS * LANES), np.float32)
    for h in range(N_HEADS):
        e[SMALL_B + h, h * HEAD_DIM:(h + 1) * HEAD_DIM] = 1.0
        e[SMALL_A + h, QK_W + h * LANES:QK_W + (h + 1) * LANES] = 1.0
    return jnp.asarray(e, dtype=BF16)


def _pad_lanes(vec, offset):
    return jnp.zeros((1, LANES), F32).at[0, offset:offset + vec.shape[0]].set(vec.astype(F32))


def _regroup_w_in(w):
    d = w.shape[0]
    o_z = 3 * QK_W
    o_b = o_z + QK_W
    o_a = o_b + N_HEADS
    o_fq = o_a + N_HEADS
    o_ff = o_fq + 4 * QK_W
    o_gate = o_ff + N_HEADS
    small = jnp.concatenate(
        [w[:, o_b:o_a], w[:, o_a:o_fq], w[:, o_ff:o_gate],
         jnp.zeros((d, LANES - 3 * N_HEADS), w.dtype)], axis=1)
    return tuple(g.astype(BF16) for g in (w[:, 0:o_b], w[:, o_fq:o_ff], small, w[:, o_gate:]))


def kernel(x, norm_w, w_in, gate_b, conv_w, a_log, dt_bias, gdn_norm_w, f_bias,
           fox_qn_w, fox_kn_w, fox_on_w, w_up_gdn, w_up_fox, w_out, final_norm_w):
    batch, s, d = x.shape
    assert norm_w.shape[0] == 1, "the merge call fuses the final RMSNorm: single-layer trunk only"
    assert s % (FOX_TQ * FOX_Q_TILES_PER_STEP) == 0 and FOX_TQ % FOX_TK == 0 and FOX_TK % FOX_TF == 0
    assert s % GDN_ROWS == 0 and GDN_ROWS % GDN_CHUNK == 0
    assert w_in.shape[2] == P_WIDTH - LANES + 3 * N_HEADS + gate_b.shape[1]
    l = 0

    bd = _block_diag_ones(QK_W, HEAD_DIM)
    tri_gdn = _lower_tri_ones(GDN_ROWS, GDN_CHUNK)
    tri_fox = _lower_tri_ones(FOX_TK)
    expand = _expand_matrix()
    tile_heads = lambda w: jnp.tile(w.astype(F32), N_HEADS)[None, :]

    outs = []
    x_rows = x.reshape(batch * s, d)
    for b in range(batch):
        xb = x_rows if batch == 1 else x_rows[b * s:(b + 1) * s]
        proj, gates = _proj(xb, norm_w[l][None, :], _regroup_w_in(w_in.reshape(w_in.shape[1:])),
                            gate_b[l][None, :].astype(F32))
        ya = _gdn(proj, conv_w[l], _pad_lanes(a_log[l], SMALL_A), _pad_lanes(dt_bias[l], SMALL_A),
                  tile_heads(gdn_norm_w[l]), bd, tri_gdn, expand)
        qa, ka, vb, ct, bmax, crange = _foxprep(
            proj, tile_heads(fox_qn_w[l]), tile_heads(fox_kn_w[l]),
            _pad_lanes(f_bias[l], SMALL_F), bd, tri_fox, rows=FOX_TK)
        bound_max = jnp.max(bmax[0, SMALL_BOUND:SMALL_BOUND + N_HEADS])
        fast = (bound_max <= FAST_BOUND_MAX).astype(jnp.int32).reshape(1)
        first = _first_key_tile(crange.reshape(s // FOX_TF, 2, LANES), bmax, tq=FOX_TQ, tk=FOX_TF)
        yb = _fox(fast, first, qa, ka, vb, ct, proj,
                  jnp.tile(fox_on_w[l].astype(F32), 2)[None, :], tq=FOX_TQ, tk=FOX_TK)
        outs.append(_merge(xb, ya, yb, gates,
                           w_up_gdn[l].astype(BF16), w_up_fox[l].astype(BF16),
                           w_out[l].astype(BF16), final_norm_w[None, :].astype(F32)))
    out = outs[0] if batch == 1 else jnp.concatenate(outs, axis=0)
    return out.reshape(batch, s, d)
```

```python
import functools

import numpy as np
import jax
import jax.numpy as jnp
from jax import lax
from jax.experimental import pallas as pl
from jax.experimental.pallas import tpu as pltpu

F32 = jnp.float32
BF16 = jnp.bfloat16

EPS = 1e-6
LANES = 128
HEAD_DIM = 64
N_HEADS = 8
N_PAIRS = N_HEADS // 2
QK_W = N_HEADS * HEAD_DIM
CONV_K = 4
GDN_CHUNK = 128
GDN_ROWS = 256
FOX_TQ = 512
FOX_Q_TILES_PER_STEP = 4
FOX_TK = 512
FOX_TF = 256
VMEM_LIMIT = 56 * 1024 * 1024

COL_GDN_QKV = 0
COL_GDN_Z = 1536
COL_FOX_Q = 2048
COL_FOX_K = 2560
COL_FOX_V = 3072
COL_FOX_Z = 3584
COL_SMALL = 4096
P_WIDTH = 4224
SMALL_B, SMALL_A, SMALL_F = 0, 8, 16


def _dot(a, b):
    return jnp.dot(a, b, preferred_element_type=F32)


def _dot_nt(a, b):
    return lax.dot_general(a, b, (((1,), (1,)), ((), ())), preferred_element_type=F32)


def _split3(x):
    hi = x.astype(BF16)
    r = x - hi.astype(F32)
    mid = r.astype(BF16)
    lo = (r - mid.astype(F32)).astype(BF16)
    return hi, mid, lo


def _dot_sel_rhs(x, sel):
    hi, mid, lo = _split3(x)
    return _dot(hi, sel) + _dot(mid, sel) + _dot(lo, sel)


def _dot_sel_lhs(sel, x):
    hi, mid, lo = _split3(x)
    return _dot(sel, hi) + _dot(sel, mid) + _dot(sel, lo)


def _head_sumsq(x, bd):
    return _dot((x * x).astype(BF16), bd)


def _sigmoid(x):
    return 1.0 / (1.0 + jnp.exp(-x))


def _softplus(x):
    return jnp.maximum(x, 0.0) + jnp.log1p(jnp.exp(-jnp.abs(x)))


def _lane_iota(shape):
    return lax.broadcasted_iota(jnp.int32, shape, len(shape) - 1)


def _proj_kernel(x_ref, nw_ref, wa_ref, wb_ref, ws_ref, wg_ref, gb_ref, o_ref, g_ref, *, n_chunk):
    x = x_ref[...]
    ms = jnp.mean(x * x, axis=-1, keepdims=True)
    h = (x * lax.rsqrt(ms + EPS) * nw_ref[...]).astype(BF16)
    col = 0
    for w_ref in (wa_ref, wb_ref, ws_ref):
        for c0 in range(0, w_ref.shape[1], n_chunk):
            c1 = min(c0 + n_chunk, w_ref.shape[1])
            o_ref[:, col + c0:col + c1] = _dot(h, w_ref[:, c0:c1])
        col += w_ref.shape[1]
    for c0 in range(0, g_ref.shape[1], n_chunk):
        c1 = c0 + n_chunk
        logit = _dot(h, wg_ref[:, c0:c1]) + gb_ref[:, c0:c1]
        g_ref[:, c0:c1] = _sigmoid(logit).astype(g_ref.dtype)


def _proj(x, norm_w, w_groups, gate_b, *, tm=512):
    s, d = x.shape
    w_gdn, w_fox, w_small, w_gate = w_groups
    gate_w = gate_b.shape[1]
    width = w_gdn.shape[1] + w_fox.shape[1] + w_small.shape[1]
    assert width == P_WIDTH and w_gate.shape[1] == gate_w
    const = lambda shape: pl.BlockSpec(shape, lambda i: (0,) * len(shape))
    return pl.pallas_call(
        functools.partial(_proj_kernel, n_chunk=1024),
        grid=(s // tm,),
        in_specs=[
            pl.BlockSpec((tm, d), lambda i: (i, 0)),
            const((1, d)),
            const(w_gdn.shape), const(w_fox.shape), const(w_small.shape), const(w_gate.shape),
            const((1, gate_w)),
        ],
        out_specs=[pl.BlockSpec((tm, width), lambda i: (i, 0)),
                   pl.BlockSpec((tm, gate_w), lambda i: (i, 0))],
        out_shape=[jax.ShapeDtypeStruct((s, width), F32),
                   jax.ShapeDtypeStruct((s, gate_w), BF16)],
        compiler_params=pltpu.CompilerParams(
            dimension_semantics=("arbitrary",), vmem_limit_bytes=VMEM_LIMIT),
        name="proj",
    )(x, norm_w, w_gdn, w_fox, w_small, w_gate, gate_b)


INV_BASE_LOG2 = 5


def _shr(x, n):
    return lax.shift_right_logical(x, n)


def _inverse_masks(row, col):
    c = row.shape[0]
    lb = INV_BASE_LOG2
    diag = _shr(row, lb) == _shr(col, lb)
    levels = []
    b = lb
    while (1 << b) < c:
        in_pair = _shr(row, b + 1) == _shr(col, b + 1)
        levels.append(in_pair & ((_shr(row, b) & 1) == 1) & ((_shr(col, b) & 1) == 0))
        b += 1
    return diag, levels


def _unit_lower_inverses(ms, eye, masks):
    c = ms[0].shape[0]
    diag, levels = masks
    mds = [jnp.where(diag, m, 0.0) for m in ms]
    ts = [eye - md for md in mds]
    mps = [_dot(md.astype(BF16), md.astype(BF16)) for md in mds]
    n_sq = INV_BASE_LOG2 - 1
    for it in range(n_sq):
        if it < n_sq - 1:
            boths = [_dot(jnp.concatenate([t, mp], axis=0).astype(BF16), mp.astype(BF16))
                     for t, mp in zip(ts, mps)]
            ts = [t + both[0:c, :] for t, both in zip(ts, boths)]
            mps = [both[c:2 * c, :] for both in boths]
        else:
            ts = [t + _dot(t.astype(BF16), mp.astype(BF16)) for t, mp in zip(ts, mps)]
    b = 1 << INV_BASE_LOG2
    for level in levels:
        odd_rows = lambda x: jnp.concatenate([x[s:s + b, :] for s in range(b, c, 2 * b)], axis=0)
        t_odd = [odd_rows(t) for t in ts]
        tls = [_dot(to.astype(BF16), jnp.where(level, m, 0.0).astype(BF16))
               for to, m in zip(t_odd, ms)]
        new_odd = [to - _dot(tl.astype(BF16), t.astype(BF16)) for to, tl, t in zip(t_odd, tls, ts)]
        ts = [jnp.concatenate(
            [no[(blk // 2) * b:(blk // 2 + 1) * b, :] if blk % 2 else t[blk * b:(blk + 1) * b, :]
             for blk in range(c // b)], axis=0) for t, no in zip(ts, new_odd)]
        b *= 2
    return ts


def _gdn_kernel(qkv_ref, z_ref, sm_ref, cw_ref, alog_ref, dtb_ref, nw_ref,
                bd_ref, tri_ref, ex_ref, o_ref, xext_ref, state_ref):
    c = GDN_CHUNK
    r = qkv_ref.shape[0]
    chunks = [slice(ci * c, (ci + 1) * c) for ci in range(r // c)]
    step = pl.program_id(0)

    @pl.when(step == 0)
    def _():
        xext_ref[0:8, :] = jnp.zeros((8, xext_ref.shape[1]), F32)
        state_ref[...] = jnp.zeros(state_ref.shape, F32)

    xext_ref[8:8 + r, :] = qkv_ref[...]
    y = cw_ref[3:4, :] * xext_ref[8:8 + r, :]
    for j in range(CONV_K - 1):
        y = y + cw_ref[j:j + 1, :] * xext_ref[5 + j:5 + j + r, :]
    xext_ref[0:8, :] = qkv_ref[r - 8:r, :]
    y = y * _sigmoid(y)

    bd = bd_ref[...]
    q = y[:, 0:QK_W]
    k = y[:, QK_W:2 * QK_W]
    v = y[:, 2 * QK_W:3 * QK_W]
    q = q * lax.rsqrt(_head_sumsq(q, bd) + EPS) * (HEAD_DIM ** -0.5)
    k = k * lax.rsqrt(_head_sumsq(k, bd) + EPS)

    sm = sm_ref[...]
    beta_s = _sigmoid(sm)
    g_s = -jnp.exp(alog_ref[...]) * _softplus(sm + dtb_ref[...])
    g_cum = _dot_sel_lhs(tri_ref[...], g_s)
    beta = _dot(beta_s.astype(BF16), ex_ref[:, 0:QK_W])
    g_bc = _dot_sel_rhs(g_cum, ex_ref[:, QK_W:QK_W + N_HEADS * LANES])
    even_r = _lane_iota((r, LANES)) < HEAD_DIM
    g = jnp.concatenate(
        [jnp.where(even_r, g_bc[:, 2 * p * LANES:(2 * p + 1) * LANES],
                   g_bc[:, (2 * p + 1) * LANES:(2 * p + 2) * LANES]) for p in range(N_PAIRS)], axis=1)
    g_t = g_cum.T

    eg = jnp.exp(g)
    kb = k * beta
    vb = v * beta
    kbg = kb * eg
    qg = q * eg
    g_last = [g[rows.stop - 1:rows.stop, :] for rows in chunks]
    e_last = [jnp.exp(gl) for gl in g_last]
    kt = [k[rows] * jnp.exp(gl - g[rows]) for rows, gl in zip(chunks, g_last)]

    row = lax.broadcasted_iota(jnp.int32, (c, c), 0)
    col = lax.broadcasted_iota(jnp.int32, (c, c), 1)
    causal = col <= row
    strict = col < row
    eye = (col == row).astype(F32)
    inv_masks = _inverse_masks(row, col)
    lane = _lane_iota((c, LANES))
    is_even = lane < HEAD_DIM
    bd_mask = (lane < HEAD_DIM) == (lax.broadcasted_iota(jnp.int32, (c, LANES), 0) < HEAD_DIM)

    pair_lanes = [slice(p * LANES, (p + 1) * LANES) for p in range(N_PAIRS)]
    odd = jnp.logical_not(is_even)

    kq = []
    for rows in chunks:
        for ls in pair_lanes:
            kb_p, q_p = kb[rows, ls], q[rows, ls]
            lhs = jnp.concatenate([jnp.where(is_even, kb_p, 0.0), jnp.where(odd, kb_p, 0.0),
                                   jnp.where(is_even, q_p, 0.0), jnp.where(odd, q_p, 0.0)], axis=0)
            kq.append(_dot_nt(lhs.astype(BF16), k[rows, ls].astype(BF16)))
    ms, a_h = [], []
    for ci, rows in enumerate(chunks):
        for h in range(N_HEADS):
            p, hh = divmod(h, 2)
            kq_p = kq[ci * N_PAIRS + p]
            diff = g_bc[rows, h * LANES:(h + 1) * LANES] - g_t[SMALL_A + h:SMALL_A + h + 1, rows]
            dec = jnp.where(causal, jnp.exp(jnp.where(causal, diff, 0.0)), 0.0)
            ms.append(jnp.where(strict, kq_p[hh * c:(hh + 1) * c, :] * dec, 0.0))
            a_h.append((kq_p[(2 + hh) * c:(3 + hh) * c, :] * dec).astype(BF16))
    ts = _unit_lower_inverses(ms, eye, inv_masks)
    wk = []
    for ci, rows in enumerate(chunks):
        rhs = [jnp.concatenate([vb[rows, ls], kbg[rows, ls]], axis=1).astype(BF16)
               for ls in pair_lanes]
        wk += [_dot(ts[ci * N_HEADS + h].astype(BF16), rhs[h // 2]) for h in range(N_HEADS)]

    sts = [state_ref[p] for p in range(N_PAIRS)]
    for ci, rows in enumerate(chunks):
        stb = [st.astype(BF16) for st in sts]
        ub = []
        for p in range(N_PAIRS):
            wk_e, wk_o = wk[ci * N_HEADS + 2 * p], wk[ci * N_HEADS + 2 * p + 1]
            w_pair = jnp.where(is_even, wk_e[:, 0:LANES], wk_o[:, 0:LANES])
            kc_pair = jnp.where(is_even, wk_e[:, LANES:2 * LANES], wk_o[:, LANES:2 * LANES])
            ub.append((w_pair - _dot(kc_pair.astype(BF16), stb[p])).astype(BF16))
        for p, ls in enumerate(pair_lanes):
            a_e, a_o = a_h[ci * N_HEADS + 2 * p], a_h[ci * N_HEADS + 2 * p + 1]
            o = _dot(qg[rows, ls].astype(BF16), stb[p]) + jnp.where(
                is_even, _dot(a_e, ub[p]), _dot(a_o, ub[p]))
            upd = _dot(kt[ci][:, ls].T.astype(BF16), ub[p])
            sts[p] = sts[p] * e_last[ci][:, ls] + jnp.where(bd_mask, upd, 0.0)
            o_ref[rows, ls] = o.astype(o_ref.dtype)
    for p in range(N_PAIRS):
        state_ref[p] = sts[p]

    o_all = o_ref[...].astype(F32)
    z = z_ref[...]
    ms = _head_sumsq(o_all, bd) * (1.0 / HEAD_DIM)
    yn = o_all * lax.rsqrt(ms + EPS) * nw_ref[...]
    o_ref[...] = (yn * (z * _sigmoid(z))).astype(o_ref.dtype)


def _gdn(proj, conv_w, a_log_pad, dtb_pad, nw_tiled, bd, tri, expand):
    s = proj.shape[0]
    c = tri.shape[0]
    conv_width = 3 * QK_W
    const = lambda shape: pl.BlockSpec(shape, lambda i: (0,) * len(shape))
    return pl.pallas_call(
        _gdn_kernel,
        grid=(s // c,),
        in_specs=[
            pl.BlockSpec((c, conv_width), lambda i: (i, COL_GDN_QKV // conv_width)),
            pl.BlockSpec((c, QK_W), lambda i: (i, COL_GDN_Z // QK_W)),
            pl.BlockSpec((c, LANES), lambda i: (i, COL_SMALL // LANES)),
            const((CONV_K, conv_width)),
            const((1, LANES)),
            const((1, LANES)),
            const((1, QK_W)),
            const((QK_W, QK_W)),
            const((c, c)),
            const(expand.shape),
        ],
        out_specs=pl.BlockSpec((c, QK_W), lambda i: (i, 0)),
        out_shape=jax.ShapeDtypeStruct((s, QK_W), F32),
        scratch_shapes=[
            pltpu.VMEM((c + 8, conv_width), F32),
            pltpu.VMEM((N_PAIRS, LANES, LANES), F32),
        ],
        compiler_params=pltpu.CompilerParams(
            dimension_semantics=("arbitrary",), vmem_limit_bytes=VMEM_LIMIT),
        name="gdn",
    )(proj, proj, proj, conv_w, a_log_pad, dtb_pad, nw_tiled, bd, tri, expand)


X_C, X_EVEN, X_ODD, X_BOUND = 0, 3, 6, 9
SMALL_BOUND = 24
TERM_LANE_STEP = 16
N_EXTRA_COLS = (N_HEADS + N_PAIRS) * LANES


def _extras_selector():
    sel = np.zeros((LANES, N_EXTRA_COLS), np.float32)
    pat = np.zeros((1, N_EXTRA_COLS), np.float32)
    k0 = N_HEADS * LANES
    for t in range(3):
        r0 = t * TERM_LANE_STEP
        for h in range(N_HEADS):
            sel[r0 + SMALL_F + h, h * LANES + X_C + t] = 1.0
            sel[r0 + SMALL_BOUND + h, h * LANES + X_BOUND + t] = 1.0
            pat[0, h * LANES + (X_EVEN if h % 2 == 0 else X_ODD) + t] = 1.0
        for p in range(N_PAIRS):
            sel[r0 + SMALL_F + 2 * p, k0 + p * LANES + X_EVEN + t] = -1.0
            sel[r0 + SMALL_F + 2 * p + 1, k0 + p * LANES + X_ODD + t] = -1.0
            pat[0, k0 + p * LANES + X_C + t] = 1.0
            pat[0, k0 + p * LANES + X_BOUND + t] = 1.0
    return jnp.asarray(sel, dtype=BF16), jnp.asarray(pat, dtype=F32)


def _sumsq_selector():
    sel = np.zeros((QK_W, LANES), np.float32)
    for h in range(N_HEADS):
        sel[h * HEAD_DIM:(h + 1) * HEAD_DIM, SMALL_BOUND + h] = 1.0
    return jnp.asarray(sel, dtype=BF16)


def _foxprep_kernel(q_ref, k_ref, v_ref, sm_ref, qw_ref, kw_ref, fb_ref, bd_ref, tri_ref, ss_ref,
                    sel_ref, pat_ref,
                    qa_ref, ka_ref, vo_ref, ct_ref, bmax_ref, crange_ref, carry_ref, kmax_ref):
    @pl.when(pl.program_id(0) == 0)
    def _():
        carry_ref[...] = jnp.zeros(carry_ref.shape, F32)
        kmax_ref[...] = jnp.zeros(kmax_ref.shape, F32)
        bmax_ref[...] = jnp.zeros(bmax_ref.shape, F32)

    bd = bd_ref[...]

    def headnorm(x, w):
        ms = _head_sumsq(x, bd) * (1.0 / HEAD_DIM)
        return x * lax.rsqrt(ms + EPS) * w

    qb = (headnorm(q_ref[...], qw_ref[...]) * (HEAD_DIM ** -0.5)).astype(BF16)
    kb = headnorm(k_ref[...], kw_ref[...]).astype(BF16)
    vo_ref[...] = v_ref[...].astype(vo_ref.dtype)
    rows = qb.shape[0]

    xf = sm_ref[...] + fb_ref[...]
    log_f = jnp.minimum(xf, 0.0) - jnp.log1p(jnp.exp(-jnp.abs(xf)))
    cum = carry_ref[...] + _dot_sel_lhs(tri_ref[...], log_f)
    carry_ref[...] = cum[rows - 1:rows, :]
    ct_ref[0] = cum.T[SMALL_F:SMALL_F + N_HEADS, :]
    for f in range(rows // FOX_TF):
        seg = cum[f * FOX_TF:(f + 1) * FOX_TF, :]
        crange_ref[0, 2 * f:2 * f + 1, :] = jnp.max(seg, axis=0, keepdims=True)
        crange_ref[0, 2 * f + 1:2 * f + 2, :] = jnp.min(seg, axis=0, keepdims=True)

    qf = qb.astype(F32)
    kf = kb.astype(F32)
    ss = ss_ref[...]
    kss = _dot((kf * kf).astype(BF16), ss)
    kmax = jnp.maximum(kmax_ref[...], jnp.max(kss, axis=0, keepdims=True))
    kmax_ref[...] = kmax
    bound = jnp.sqrt(_dot((qf * qf).astype(BF16), ss) * kmax)
    bmax_ref[...] = jnp.maximum(bmax_ref[...], jnp.max(bound, axis=0, keepdims=True))

    lane = _lane_iota((rows, LANES))
    hi, mid, lo = _split3(jnp.where(lane < SMALL_BOUND, cum, -bound))
    first_mid = SMALL_F + TERM_LANE_STEP
    terms = jnp.where(lane < first_mid, hi.astype(F32), jnp.where(
        lane < first_mid + TERM_LANE_STEP, pltpu.roll(mid.astype(F32), TERM_LANE_STEP, 1),
        pltpu.roll(lo.astype(F32), 2 * TERM_LANE_STEP, 1)))
    extras = (_dot(terms.astype(BF16), sel_ref[...]) + pat_ref[...]).astype(BF16)
    is_even = lane < HEAD_DIM
    for p in range(N_PAIRS):
        ls = slice(p * LANES, (p + 1) * LANES)
        for hh in range(2):
            h = 2 * p + hh
            keep = is_even if hh == 0 else jnp.logical_not(is_even)
            qa_ref[p, hh, :, 0:LANES] = jnp.where(keep, qb[:, ls], jnp.zeros_like(qb[:, ls]))
            qa_ref[p, hh, :, LANES:2 * LANES] = extras[:, h * LANES:(h + 1) * LANES]
        ka_ref[:, 2 * p * LANES:(2 * p + 1) * LANES] = kb[:, ls]
        ka_ref[:, (2 * p + 1) * LANES:(2 * p + 2) * LANES] = extras[
            :, (N_HEADS + p) * LANES:(N_HEADS + p + 1) * LANES]


def _foxprep(proj, qw_tiled, kw_tiled, fb_pad, bd, tri, *, rows):
    s = proj.shape[0]
    const = lambda shape: pl.BlockSpec(shape, lambda i: (0,) * len(shape))
    ss_sel = _sumsq_selector()
    ex_sel, ex_pat = _extras_selector()
    return pl.pallas_call(
        _foxprep_kernel,
        grid=(s // rows,),
        in_specs=[
            pl.BlockSpec((rows, QK_W), lambda i: (i, COL_FOX_Q // QK_W)),
            pl.BlockSpec((rows, QK_W), lambda i: (i, COL_FOX_K // QK_W)),
            pl.BlockSpec((rows, QK_W), lambda i: (i, COL_FOX_V // QK_W)),
            pl.BlockSpec((rows, LANES), lambda i: (i, COL_SMALL // LANES)),
            const((1, QK_W)),
            const((1, QK_W)),
            const((1, LANES)),
            const((QK_W, QK_W)),
            const((rows, rows)),
            const(ss_sel.shape),
            const(ex_sel.shape),
            const(ex_pat.shape),
        ],
        out_specs=[
            pl.BlockSpec((N_PAIRS, 2, rows, 2 * LANES), lambda i: (0, 0, i, 0)),
            pl.BlockSpec((rows, 2 * QK_W), lambda i: (i, 0)),
            pl.BlockSpec((rows, QK_W), lambda i: (i, 0)),
            pl.BlockSpec((1, N_HEADS, rows), lambda i: (i, 0, 0)),
            pl.BlockSpec((1, LANES), lambda i: (0, 0)),
            pl.BlockSpec((1, 2 * (rows // FOX_TF), LANES), lambda i: (i, 0, 0)),
        ],
        out_shape=[
            jax.ShapeDtypeStruct((N_PAIRS, 2, s, 2 * LANES), BF16),
            jax.ShapeDtypeStruct((s, 2 * QK_W), BF16),
            jax.ShapeDtypeStruct((s, QK_W), BF16),
            jax.ShapeDtypeStruct((s // rows, N_HEADS, rows), F32),
            jax.ShapeDtypeStruct((1, LANES), F32),
            jax.ShapeDtypeStruct((s // rows, 2 * (rows // FOX_TF), LANES), F32),
        ],
        scratch_shapes=[pltpu.VMEM((1, LANES), F32), pltpu.VMEM((1, LANES), F32)],
        compiler_params=pltpu.CompilerParams(
            dimension_semantics=("arbitrary",), vmem_limit_bytes=VMEM_LIMIT),
        name="foxprep",
    )(proj, proj, proj, proj, qw_tiled, kw_tiled, fb_pad, bd, tri, ss_sel, ex_sel, ex_pat)


FAST_BOUND_MAX = 40.0
FAST_TAIL_BACK = 3


def _fox_kernel(fast_ref, first_ref, q_ref, k_ref, v_ref, c_ref, z_ref, nw_ref, o_ref, m_ref,
                acc_ref, *, tq, tk, tf):
    for t in range(q_ref.shape[2] // tq):
        _fox_query_tile(t, fast_ref, first_ref, q_ref, k_ref, v_ref, c_ref, z_ref, nw_ref, o_ref,
                        m_ref, acc_ref, tq=tq, tk=tk, tf=tf)


def _fox_query_tile(t, fast_ref, first_ref, q_ref, k_ref, v_ref, c_ref, z_ref, nw_ref, o_ref, m_ref,
                    acc_ref, *, tq, tk, tf):
    tiles = q_ref.shape[2] // tq
    rows = slice(t * tq, (t + 1) * tq)
    p = pl.program_id(0)
    i = pl.program_id(1) * tiles + t
    first = first_ref[p * (pl.num_programs(1) * tiles) + i]
    n_sub = tq // tk
    n_diag = tq // tf

    acc_ref[...] = jnp.zeros(acc_ref.shape, F32)

    def kv_rows(start, size):
        start = pl.multiple_of(start, tf)
        vj = jnp.concatenate([v_ref[pl.ds(start, size), :], jnp.ones((size, LANES), BF16)], axis=1)
        return k_ref[pl.ds(start, size), :], vj

    def causal_keep(j):
        q_pos = i * tq + lax.broadcasted_iota(jnp.int32, (tq, tk), 0)
        k_pos = j * tk + lax.broadcasted_iota(jnp.int32, (tq, tk), 1)
        return k_pos <= q_pos

    def fast_tile(j):
        kj, vj = kv_rows(j * tf, tf)
        for hh in range(2):
            s = _dot_nt(q_ref[0, hh, rows, :], kj)
            acc_ref[hh] += _dot(jnp.exp(s).astype(BF16), vj)

    def slow_tile(j, masked):
        kj, vj = kv_rows(j * tk, tk)
        for hh in range(2):
            ck = c_ref[j, pl.ds(2 * p + hh, 1), :]
            s = _dot_nt(q_ref[0, hh, rows, 0:LANES], kj[:, 0:LANES]) - ck
            if masked:
                s = jnp.where(causal_keep(j), s, -jnp.inf)
            m_prev = m_ref[hh]
            m_new = jnp.maximum(m_prev, jnp.max(s, axis=-1, keepdims=True))
            alpha = jnp.exp(m_prev - m_new)
            pr = jnp.exp(s - m_new)
            acc_ref[hh] = alpha * acc_ref[hh] + _dot(pr.astype(BF16), vj)
            m_ref[hh] = m_new

    def run(tile):
        def body(j, carry):
            tile(j, False)
            return carry
        lax.fori_loop(first // (tk // tf), i * n_sub, body, 0)
        for d in range(n_sub):
            tile(i * n_sub + d, True)

    def fast_tail(n_back):
        span = n_back * tf + tq
        ks, vs = kv_rows(i * tq - n_back * tf, span)
        q_rel = n_back * tf + lax.broadcasted_iota(jnp.int32, (tq, span), 0)
        keep = lax.broadcasted_iota(jnp.int32, (tq, span), 1) <= q_rel
        for hh in range(2):
            s = jnp.where(keep, _dot_nt(q_ref[0, hh, rows, :], ks), -jnp.inf)
            acc_ref[hh] += _dot(jnp.exp(s).astype(BF16), vs)

    def run_fast():
        def body(j, carry):
            fast_tile(j)
            return carry
        lax.fori_loop(first, jnp.maximum(first, i * n_diag - FAST_TAIL_BACK), body, 0)
        n_short = -(-FAST_TAIL_BACK // n_diag)
        for i_short in range(n_short):
            pl.when(i == i_short)(functools.partial(fast_tail, i_short * n_diag))
        pl.when(i >= n_short)(functools.partial(fast_tail, FAST_TAIL_BACK))

    @pl.when(fast_ref[0] == 1)
    def _():
        run_fast()

    @pl.when(fast_ref[0] != 1)
    def _():
        m_ref[...] = jnp.full(m_ref.shape, -jnp.inf, F32)
        run(slow_tile)

    acc_e = acc_ref[0]
    acc_o = acc_ref[1]
    is_even = _lane_iota((tq, LANES)) < HEAD_DIM
    o = jnp.where(is_even, acc_e[:, 0:LANES] / acc_e[:, LANES:2 * LANES],
                  acc_o[:, 0:LANES] / acc_o[:, LANES:2 * LANES])
    o2 = o * o
    ss_e = jnp.sum(jnp.where(is_even, o2, 0.0), axis=-1, keepdims=True)
    ss_o = jnp.sum(jnp.where(is_even, 0.0, o2), axis=-1, keepdims=True)
    ms = jnp.where(is_even, ss_e, ss_o) * (1.0 / HEAD_DIM)
    z = z_ref[rows, :]
    o_ref[rows, :] = (o * lax.rsqrt(ms + EPS) * nw_ref[...] * (z * _sigmoid(z))).astype(o_ref.dtype)


SKIP_LOG = 110.0


def _first_key_tile(crange, bmax, *, tq, tk):
    n_k = crange.shape[0]
    n_sub = tq // tk
    c_max = crange[:, 0, SMALL_F:SMALL_F + N_HEADS].T
    c_min = crange[:, 1, SMALL_F:SMALL_F + N_HEADS].T
    cq_max = c_max.reshape(N_HEADS, n_k // n_sub, n_sub).max(axis=-1)
    bound = bmax[0, SMALL_BOUND:SMALL_BOUND + N_HEADS]
    gap = cq_max[:, :, None] - c_min[:, None, :] + 2.0 * bound[:, None, None]
    dead = gap <= -SKIP_LOG
    tile_idx = lax.broadcasted_iota(jnp.int32, dead.shape, 2)
    first = jnp.min(jnp.where(dead, n_k, tile_idx), axis=-1)
    first = jnp.min(first.reshape(N_PAIRS, 2, -1), axis=1)
    first = jnp.minimum(first, jnp.arange(first.shape[1], dtype=jnp.int32) * n_sub)
    return first.reshape(-1).astype(jnp.int32)


def _fox(fast, first, qa, ka, vb, ct, proj, nw_pair, *, tq, tk):
    s = vb.shape[0]
    step_rows = FOX_Q_TILES_PER_STEP * tq
    grid_spec = pltpu.PrefetchScalarGridSpec(
        num_scalar_prefetch=2,
        grid=(N_PAIRS, s // step_rows),
        in_specs=[
            pl.BlockSpec((1, 2, step_rows, 2 * LANES), lambda p, i, *_: (p, 0, i, 0)),
            pl.BlockSpec((s, 2 * LANES), lambda p, i, *_: (0, p)),
            pl.BlockSpec((s, LANES), lambda p, i, *_: (0, p)),
            pl.BlockSpec(ct.shape, lambda p, i, *_: (0, 0, 0)),
            pl.BlockSpec((step_rows, LANES), lambda p, i, *_: (i, COL_FOX_Z // LANES + p)),
            pl.BlockSpec((1, LANES), lambda p, i, *_: (0, 0)),
        ],
        out_specs=pl.BlockSpec((step_rows, LANES), lambda p, i, *_: (i, p)),
        scratch_shapes=[
            pltpu.VMEM((2, tq, 1), F32),
            pltpu.VMEM((2, tq, 2 * LANES), F32),
        ],
    )
    return pl.pallas_call(
        functools.partial(_fox_kernel, tq=tq, tk=tk, tf=FOX_TF),
        grid_spec=grid_spec,
        out_shape=jax.ShapeDtypeStruct((s, QK_W), BF16),
        compiler_params=pltpu.CompilerParams(
            dimension_semantics=("arbitrary", "arbitrary"), vmem_limit_bytes=VMEM_LIMIT),
        name="fox",
    )(fast, first, qa, ka, vb, ct, proj, nw_pair)


def _merge_kernel(x_ref, ya_ref, yb_ref, ga_ref, gb_ref, pa_ref, pb_ref, wo_ref, fw_ref, o_ref):
    ua = _dot(ya_ref[...].astype(BF16), pa_ref[...])
    ub = _dot(yb_ref[...].astype(BF16), pb_ref[...])
    merged = ga_ref[...].astype(F32) * ua + gb_ref[...].astype(F32) * ub
    xn = x_ref[...] + _dot(merged.astype(BF16), wo_ref[...])
    ms = jnp.mean(xn * xn, axis=-1, keepdims=True)
    o_ref[...] = xn * lax.rsqrt(ms + EPS) * fw_ref[...]


def _merge(x, ya, yb, gates, w_up_a, w_up_b, w_out, out_w, *, tm=512):
    s, d = x.shape
    const = lambda shape: pl.BlockSpec(shape, lambda i: (0,) * len(shape))
    return pl.pallas_call(
        _merge_kernel,
        grid=(s // tm,),
        in_specs=[
            pl.BlockSpec((tm, d), lambda i: (i, 0)),
            pl.BlockSpec((tm, QK_W), lambda i: (i, 0)),
            pl.BlockSpec((tm, QK_W), lambda i: (i, 0)),
            pl.BlockSpec((tm, d), lambda i: (i, 0)),
            pl.BlockSpec((tm, d), lambda i: (i, 1)),
            const((QK_W, d)),
            const((QK_W, d)),
            const((d, d)),
            const((1, d)),
        ],
        out_specs=pl.BlockSpec((tm, d), lambda i: (i, 0)),
        out_shape=jax.ShapeDtypeStruct((s, d), F32),
        compiler_params=pltpu.CompilerParams(
            dimension_semantics=("arbitrary",), vmem_limit_bytes=VMEM_LIMIT),
        name="merge",
    )(x, ya, yb, gates, gates, w_up_a, w_up_b, w_out, out_w)


def _block_diag_ones(n, blk):
    idx = np.arange(n) // blk
    return jnp.asarray(idx[:, None] == idx[None, :], dtype=BF16)


def _lower_tri_ones(n, blk=None):
    idx = np.arange(n)
    tri = idx[None, :] <= idx[:, None]
    if blk is not None:
        tri = tri & (idx[None, :] // blk == idx[:, None] // blk)
    return jnp.asarray(tri, dtype=BF16)


def _expand_matrix():
    e = np.zeros((LANES, QK_W + N_HEADS * LANES), np.float32)
    for h in range(N_HEADS):
        e[SMALL_B + h, h * HEAD_DIM:(h + 1) * HEAD_DIM] = 1.0
        e[SMALL_A + h, QK_W + h * LANES:QK_W + (h + 1) * LANES] = 1.0
    return jnp.asarray(e, dtype=BF16)


def _pad_lanes(vec, offset):
    return jnp.zeros((1, LANES), F32).at[0, offset:offset + vec.shape[0]].set(vec.astype(F32))


def _regroup_w_in(w):
    d = w.shape[0]
    o_z = 3 * QK_W
    o_b = o_z + QK_W
    o_a = o_b + N_HEADS
    o_fq = o_a + N_HEADS
    o_ff = o_fq + 4 * QK_W
    o_gate = o_ff + N_HEADS
    small = jnp.concatenate(
        [w[:, o_b:o_a], w[:, o_a:o_fq], w[:, o_ff:o_gate],
         jnp.zeros((d, LANES - 3 * N_HEADS), w.dtype)], axis=1)
    return tuple(g.astype(BF16) for g in (w[:, 0:o_b], w[:, o_fq:o_ff], small, w[:, o_gate:]))


def kernel(x, norm_w, w_in, gate_b, conv_w, a_log, dt_bias, gdn_norm_w, f_bias,
           fox_qn_w, fox_kn_w, fox_on_w, w_up_gdn, w_up_fox, w_out, final_norm_w):
    batch, s, d = x.shape
    assert norm_w.shape[0] == 1, "the merge call fuses the final RMSNorm: single-layer trunk only"
    assert s % (FOX_TQ * FOX_Q_TILES_PER_STEP) == 0 and FOX_TQ % FOX_TK == 0 and FOX_TK % FOX_TF == 0
    assert s % GDN_ROWS == 0 and GDN_ROWS % GDN_CHUNK == 0
    assert w_in.shape[2] == P_WIDTH - LANES + 3 * N_HEADS + gate_b.shape[1]
    l = 0

    bd = _block_diag_ones(QK_W, HEAD_DIM)
    tri_gdn = _lower_tri_ones(GDN_ROWS, GDN_CHUNK)
    tri_fox = _lower_tri_ones(FOX_TK)
    expand = _expand_matrix()
    tile_heads = lambda w: jnp.tile(w.astype(F32), N_HEADS)[None, :]

    outs = []
    x_rows = x.reshape(batch * s, d)
    for b in range(batch):
        xb = x_rows if batch == 1 else x_rows[b * s:(b + 1) * s]
        proj, gates = _proj(xb, norm_w[l][None, :], _regroup_w_in(w_in.reshape(w_in.shape[1:])),
                            gate_b[l][None, :].astype(F32))
        ya = _gdn(proj, conv_w[l], _pad_lanes(a_log[l], SMALL_A), _pad_lanes(dt_bias[l], SMALL_A),
                  tile_heads(gdn_norm_w[l]), bd, tri_gdn, expand)
        qa, ka, vb, ct, bmax, crange = _foxprep(
            proj, tile_heads(fox_qn_w[l]), tile_heads(fox_kn_w[l]),
            _pad_lanes(f_bias[l], SMALL_F), bd, tri_fox, rows=FOX_TK)
        bound_max = jnp.max(bmax[0, SMALL_BOUND:SMALL_BOUND + N_HEADS])
        fast = (bound_max <= FAST_BOUND_MAX).astype(jnp.int32).reshape(1)
        first = _first_key_tile(crange.reshape(s // FOX_TF, 2, LANES), bmax, tq=FOX_TQ, tk=FOX_TF)
        yb = _fox(fast, first, qa, ka, vb, ct, proj,
                  jnp.tile(fox_on_w[l].astype(F32), 2)[None, :], tq=FOX_TQ, tk=FOX_TK)
        outs.append(_merge(xb, ya, yb, gates,
                           w_up_gdn[l].astype(BF16), w_up_fox[l].astype(BF16),
                           w_out[l].astype(BF16), final_norm_w[None, :].astype(F32)))
    out = outs[0] if batch == 1 else jnp.concatenate(outs, axis=0)
    return out.reshape(batch, s, d)
```

```python
import functools

import numpy as np
import jax
import jax.numpy as jnp
from jax import lax
from jax.experimental import pallas as pl
from jax.experimental.pallas import tpu as pltpu

F32 = jnp.float32
BF16 = jnp.bfloat16

EPS = 1e-6
LANES = 128
HEAD_DIM = 64
N_HEADS = 8
N_PAIRS = N_HEADS // 2
QK_W = N_HEADS * HEAD_DIM
CONV_K = 4
GDN_CHUNK = 128
GDN_ROWS = 256
FOX_TQ = 512
FOX_Q_TILES_PER_STEP = 2
FOX_TK = 512
FOX_TF = 256
VMEM_LIMIT = 56 * 1024 * 1024

COL_GDN_QKV = 0
COL_GDN_Z = 1536
COL_FOX_Q = 2048
COL_FOX_K = 2560
COL_FOX_V = 3072
COL_FOX_Z = 3584
COL_SMALL = 4096
P_WIDTH = 4224
SMALL_B, SMALL_A, SMALL_F = 0, 8, 16


def _dot(a, b):
    return jnp.dot(a, b, preferred_element_type=F32)


def _dot_nt(a, b):
    return lax.dot_general(a, b, (((1,), (1,)), ((), ())), preferred_element_type=F32)


def _split3(x):
    hi = x.astype(BF16)
    r = x - hi.astype(F32)
    mid = r.astype(BF16)
    lo = (r - mid.astype(F32)).astype(BF16)
    return hi, mid, lo


def _dot_sel_rhs(x, sel):
    hi, mid, lo = _split3(x)
    return _dot(hi, sel) + _dot(mid, sel) + _dot(lo, sel)


def _dot_sel_lhs(sel, x):
    hi, mid, lo = _split3(x)
    return _dot(sel, hi) + _dot(sel, mid) + _dot(sel, lo)


def _head_sumsq(x, bd):
    return _dot((x * x).astype(BF16), bd)


def _sigmoid(x):
    return 1.0 / (1.0 + jnp.exp(-x))


def _softplus(x):
    return jnp.maximum(x, 0.0) + jnp.log1p(jnp.exp(-jnp.abs(x)))


def _lane_iota(shape):
    return lax.broadcasted_iota(jnp.int32, shape, len(shape) - 1)


def _proj_kernel(x_ref, nw_ref, wa_ref, wb_ref, ws_ref, wg_ref, gb_ref, o_ref, g_ref, *, n_chunk):
    x = x_ref[...]
    ms = jnp.mean(x * x, axis=-1, keepdims=True)
    h = (x * lax.rsqrt(ms + EPS) * nw_ref[...]).astype(BF16)
    col = 0
    for w_ref in (wa_ref, wb_ref, ws_ref):
        for c0 in range(0, w_ref.shape[1], n_chunk):
            c1 = min(c0 + n_chunk, w_ref.shape[1])
            o_ref[:, col + c0:col + c1] = _dot(h, w_ref[:, c0:c1])
        col += w_ref.shape[1]
    for c0 in range(0, g_ref.shape[1], n_chunk):
        c1 = c0 + n_chunk
        logit = _dot(h, wg_ref[:, c0:c1]) + gb_ref[:, c0:c1]
        g_ref[:, c0:c1] = _sigmoid(logit).astype(g_ref.dtype)


def _proj(x, norm_w, w_groups, gate_b, *, tm=512):
    s, d = x.shape
    w_gdn, w_fox, w_small, w_gate = w_groups
    gate_w = gate_b.shape[1]
    width = w_gdn.shape[1] + w_fox.shape[1] + w_small.shape[1]
    assert width == P_WIDTH and w_gate.shape[1] == gate_w
    const = lambda shape: pl.BlockSpec(shape, lambda i: (0,) * len(shape))
    return pl.pallas_call(
        functools.partial(_proj_kernel, n_chunk=1024),
        grid=(s // tm,),
        in_specs=[
            pl.BlockSpec((tm, d), lambda i: (i, 0)),
            const((1, d)),
            const(w_gdn.shape), const(w_fox.shape), const(w_small.shape), const(w_gate.shape),
            const((1, gate_w)),
        ],
        out_specs=[pl.BlockSpec((tm, width), lambda i: (i, 0)),
                   pl.BlockSpec((tm, gate_w), lambda i: (i, 0))],
        out_shape=[jax.ShapeDtypeStruct((s, width), F32),
                   jax.ShapeDtypeStruct((s, gate_w), BF16)],
        compiler_params=pltpu.CompilerParams(
            dimension_semantics=("arbitrary",), vmem_limit_bytes=VMEM_LIMIT),
        name="proj",
    )(x, norm_w, w_gdn, w_fox, w_small, w_gate, gate_b)


INV_BASE_LOG2 = 5


def _shr(x, n):
    return lax.shift_right_logical(x, n)


def _inverse_masks(row, col):
    c = row.shape[0]
    lb = INV_BASE_LOG2
    diag = _shr(row, lb) == _shr(col, lb)
    levels = []
    b = lb
    while (1 << b) < c:
        in_pair = _shr(row, b + 1) == _shr(col, b + 1)
        levels.append(in_pair & ((_shr(row, b) & 1) == 1) & ((_shr(col, b) & 1) == 0))
        b += 1
    return diag, levels


def _unit_lower_inverses(ms, eye, masks):
    c = ms[0].shape[0]
    diag, levels = masks
    mds = [jnp.where(diag, m, 0.0) for m in ms]
    ts = [eye - md for md in mds]
    mps = [_dot(md.astype(BF16), md.astype(BF16)) for md in mds]
    n_sq = INV_BASE_LOG2 - 1
    for it in range(n_sq):
        if it < n_sq - 1:
            boths = [_dot(jnp.concatenate([t, mp], axis=0).astype(BF16), mp.astype(BF16))
                     for t, mp in zip(ts, mps)]
            ts = [t + both[0:c, :] for t, both in zip(ts, boths)]
            mps = [both[c:2 * c, :] for both in boths]
        else:
            ts = [t + _dot(t.astype(BF16), mp.astype(BF16)) for t, mp in zip(ts, mps)]
    b = 1 << INV_BASE_LOG2
    for level in levels:
        odd_rows = lambda x: jnp.concatenate([x[s:s + b, :] for s in range(b, c, 2 * b)], axis=0)
        t_odd = [odd_rows(t) for t in ts]
        tls = [_dot(to.astype(BF16), jnp.where(level, m, 0.0).astype(BF16))
               for to, m in zip(t_odd, ms)]
        new_odd = [to - _dot(tl.astype(BF16), t.astype(BF16)) for to, tl, t in zip(t_odd, tls, ts)]
        ts = [jnp.concatenate(
            [no[(blk // 2) * b:(blk // 2 + 1) * b, :] if blk % 2 else t[blk * b:(blk + 1) * b, :]
             for blk in range(c // b)], axis=0) for t, no in zip(ts, new_odd)]
        b *= 2
    return ts


def _gdn_kernel(qkv_ref, z_ref, sm_ref, cw_ref, alog_ref, dtb_ref, nw_ref,
                bd_ref, tri_ref, ex_ref, o_ref, xext_ref, state_ref):
    c = GDN_CHUNK
    r = qkv_ref.shape[0]
    chunks = [slice(ci * c, (ci + 1) * c) for ci in range(r // c)]
    step = pl.program_id(0)

    @pl.when(step == 0)
    def _():
        xext_ref[0:8, :] = jnp.zeros((8, xext_ref.shape[1]), F32)
        state_ref[...] = jnp.zeros(state_ref.shape, F32)

    xext_ref[8:8 + r, :] = qkv_ref[...]
    y = cw_ref[3:4, :] * xext_ref[8:8 + r, :]
    for j in range(CONV_K - 1):
        y = y + cw_ref[j:j + 1, :] * xext_ref[5 + j:5 + j + r, :]
    xext_ref[0:8, :] = qkv_ref[r - 8:r, :]
    y = y * _sigmoid(y)

    bd = bd_ref[...]
    q = y[:, 0:QK_W]
    k = y[:, QK_W:2 * QK_W]
    v = y[:, 2 * QK_W:3 * QK_W]
    q = q * lax.rsqrt(_head_sumsq(q, bd) + EPS) * (HEAD_DIM ** -0.5)
    k = k * lax.rsqrt(_head_sumsq(k, bd) + EPS)

    sm = sm_ref[...]
    beta_s = _sigmoid(sm)
    g_s = -jnp.exp(alog_ref[...]) * _softplus(sm + dtb_ref[...])
    g_cum = _dot_sel_lhs(tri_ref[...], g_s)
    beta = _dot(beta_s.astype(BF16), ex_ref[:, 0:QK_W])
    g_bc = _dot_sel_rhs(g_cum, ex_ref[:, QK_W:QK_W + N_HEADS * LANES])
    even_r = _lane_iota((r, LANES)) < HEAD_DIM
    g = jnp.concatenate(
        [jnp.where(even_r, g_bc[:, 2 * p * LANES:(2 * p + 1) * LANES],
                   g_bc[:, (2 * p + 1) * LANES:(2 * p + 2) * LANES]) for p in range(N_PAIRS)], axis=1)
    g_t = g_cum.T

    eg = jnp.exp(g)
    kb = k * beta
    vb = v * beta
    kbg = kb * eg
    qg = q * eg
    g_last = [g[rows.stop - 1:rows.stop, :] for rows in chunks]
    e_last = [jnp.exp(gl) for gl in g_last]
    kt = [k[rows] * jnp.exp(gl - g[rows]) for rows, gl in zip(chunks, g_last)]

    row = lax.broadcasted_iota(jnp.int32, (c, c), 0)
    col = lax.broadcasted_iota(jnp.int32, (c, c), 1)
    causal = col <= row
    strict = col < row
    eye = (col == row).astype(F32)
    inv_masks = _inverse_masks(row, col)
    lane = _lane_iota((c, LANES))
    is_even = lane < HEAD_DIM
    bd_mask = (lane < HEAD_DIM) == (lax.broadcasted_iota(jnp.int32, (c, LANES), 0) < HEAD_DIM)

    pair_lanes = [slice(p * LANES, (p + 1) * LANES) for p in range(N_PAIRS)]
    odd = jnp.logical_not(is_even)

    kq = []
    for rows in chunks:
        for ls in pair_lanes:
            kb_p, q_p = kb[rows, ls], q[rows, ls]
            lhs = jnp.concatenate([jnp.where(is_even, kb_p, 0.0), jnp.where(odd, kb_p, 0.0),
                                   jnp.where(is_even, q_p, 0.0), jnp.where(odd, q_p, 0.0)], axis=0)
            kq.append(_dot_nt(lhs.astype(BF16), k[rows, ls].astype(BF16)))
    ms, a_h = [], []
    for ci, rows in enumerate(chunks):
        for h in range(N_HEADS):
            p, hh = divmod(h, 2)
            kq_p = kq[ci * N_PAIRS + p]
            diff = g_bc[rows, h * LANES:(h + 1) * LANES] - g_t[SMALL_A + h:SMALL_A + h + 1, rows]
            dec = jnp.where(causal, jnp.exp(jnp.where(causal, diff, 0.0)), 0.0)
            ms.append(jnp.where(strict, kq_p[hh * c:(hh + 1) * c, :] * dec, 0.0))
            a_h.append((kq_p[(2 + hh) * c:(3 + hh) * c, :] * dec).astype(BF16))
    ts = _unit_lower_inverses(ms, eye, inv_masks)
    wk = []
    for ci, rows in enumerate(chunks):
        rhs = [jnp.concatenate([vb[rows, ls], kbg[rows, ls]], axis=1).astype(BF16)
               for ls in pair_lanes]
        wk += [_dot(ts[ci * N_HEADS + h].astype(BF16), rhs[h // 2]) for h in range(N_HEADS)]

    sts = [state_ref[p] for p in range(N_PAIRS)]
    for ci, rows in enumerate(chunks):
        stb = [st.astype(BF16) for st in sts]
        ub = []
        for p in range(N_PAIRS):
            wk_e, wk_o = wk[ci * N_HEADS + 2 * p], wk[ci * N_HEADS + 2 * p + 1]
            w_pair = jnp.where(is_even, wk_e[:, 0:LANES], wk_o[:, 0:LANES])
            kc_pair = jnp.where(is_even, wk_e[:, LANES:2 * LANES], wk_o[:, LANES:2 * LANES])
            ub.append((w_pair - _dot(kc_pair.astype(BF16), stb[p])).astype(BF16))
        for p, ls in enumerate(pair_lanes):
            a_e, a_o = a_h[ci * N_HEADS + 2 * p], a_h[ci * N_HEADS + 2 * p + 1]
            o = _dot(qg[rows, ls].astype(BF16), stb[p]) + jnp.where(
                is_even, _dot(a_e, ub[p]), _dot(a_o, ub[p]))
            upd = _dot(kt[ci][:, ls].T.astype(BF16), ub[p])
            sts[p] = sts[p] * e_last[ci][:, ls] + jnp.where(bd_mask, upd, 0.0)
            o_ref[rows, ls] = o.astype(o_ref.dtype)
    for p in range(N_PAIRS):
        state_ref[p] = sts[p]

    o_all = o_ref[...].astype(F32)
    z = z_ref[...]
    ms = _head_sumsq(o_all, bd) * (1.0 / HEAD_DIM)
    yn = o_all * lax.rsqrt(ms + EPS) * nw_ref[...]
    o_ref[...] = (yn * (z * _sigmoid(z))).astype(o_ref.dtype)


def _gdn(proj, conv_w, a_log_pad, dtb_pad, nw_tiled, bd, tri, expand):
    s = proj.shape[0]
    c = tri.shape[0]
    conv_width = 3 * QK_W
    const = lambda shape: pl.BlockSpec(shape, lambda i: (0,) * len(shape))
    return pl.pallas_call(
        _gdn_kernel,
        grid=(s // c,),
        in_specs=[
            pl.BlockSpec((c, conv_width), lambda i: (i, COL_GDN_QKV // conv_width)),
            pl.BlockSpec((c, QK_W), lambda i: (i, COL_GDN_Z // QK_W)),
            pl.BlockSpec((c, LANES), lambda i: (i, COL_SMALL // LANES)),
            const((CONV_K, conv_width)),
            const((1, LANES)),
            const((1, LANES)),
            const((1, QK_W)),
            const((QK_W, QK_W)),
            const((c, c)),
            const(expand.shape),
        ],
        out_specs=pl.BlockSpec((c, QK_W), lambda i: (i, 0)),
        out_shape=jax.ShapeDtypeStruct((s, QK_W), F32),
        scratch_shapes=[
            pltpu.VMEM((c + 8, conv_width), F32),
            pltpu.VMEM((N_PAIRS, LANES, LANES), F32),
        ],
        compiler_params=pltpu.CompilerParams(
            dimension_semantics=("arbitrary",), vmem_limit_bytes=VMEM_LIMIT),
        name="gdn",
    )(proj, proj, proj, conv_w, a_log_pad, dtb_pad, nw_tiled, bd, tri, expand)


X_C, X_EVEN, X_ODD, X_BOUND = 0, 3, 6, 9
SMALL_BOUND = 24
TERM_LANE_STEP = 16
N_EXTRA_COLS = (N_HEADS + N_PAIRS) * LANES


def _extras_selector():
    sel = np.zeros((LANES, N_EXTRA_COLS), np.float32)
    pat = np.zeros((1, N_EXTRA_COLS), np.float32)
    k0 = N_HEADS * LANES
    for t in range(3):
        r0 = t * TERM_LANE_STEP
        for h in range(N_HEADS):
            sel[r0 + SMALL_F + h, h * LANES + X_C + t] = 1.0
            sel[r0 + SMALL_BOUND + h, h * LANES + X_BOUND + t] = 1.0
            pat[0, h * LANES + (X_EVEN if h % 2 == 0 else X_ODD) + t] = 1.0
        for p in range(N_PAIRS):
            sel[r0 + SMALL_F + 2 * p, k0 + p * LANES + X_EVEN + t] = -1.0
            sel[r0 + SMALL_F + 2 * p + 1, k0 + p * LANES + X_ODD + t] = -1.0
            pat[0, k0 + p * LANES + X_C + t] = 1.0
            pat[0, k0 + p * LANES + X_BOUND + t] = 1.0
    return jnp.asarray(sel, dtype=BF16), jnp.asarray(pat, dtype=F32)


def _sumsq_selector():
    sel = np.zeros((QK_W, LANES), np.float32)
    for h in range(N_HEADS):
        sel[h * HEAD_DIM:(h + 1) * HEAD_DIM, SMALL_BOUND + h] = 1.0
    return jnp.asarray(sel, dtype=BF16)


def _foxprep_kernel(q_ref, k_ref, v_ref, sm_ref, qw_ref, kw_ref, fb_ref, bd_ref, tri_ref, ss_ref,
                    sel_ref, pat_ref,
                    qa_ref, ka_ref, vo_ref, ct_ref, bmax_ref, crange_ref, carry_ref, kmax_ref):
    @pl.when(pl.program_id(0) == 0)
    def _():
        carry_ref[...] = jnp.zeros(carry_ref.shape, F32)
        kmax_ref[...] = jnp.zeros(kmax_ref.shape, F32)
        bmax_ref[...] = jnp.zeros(bmax_ref.shape, F32)

    bd = bd_ref[...]

    def headnorm(x, w):
        ms = _head_sumsq(x, bd) * (1.0 / HEAD_DIM)
        return x * lax.rsqrt(ms + EPS) * w

    qb = (headnorm(q_ref[...], qw_ref[...]) * (HEAD_DIM ** -0.5)).astype(BF16)
    kb = headnorm(k_ref[...], kw_ref[...]).astype(BF16)
    vo_ref[...] = v_ref[...].astype(vo_ref.dtype)
    rows = qb.shape[0]

    xf = sm_ref[...] + fb_ref[...]
    log_f = jnp.minimum(xf, 0.0) - jnp.log1p(jnp.exp(-jnp.abs(xf)))
    cum = carry_ref[...] + _dot_sel_lhs(tri_ref[...], log_f)
    carry_ref[...] = cum[rows - 1:rows, :]
    ct_ref[0] = cum.T[SMALL_F:SMALL_F + N_HEADS, :]
    for f in range(rows // FOX_TF):
        seg = cum[f * FOX_TF:(f + 1) * FOX_TF, :]
        crange_ref[0, 2 * f:2 * f + 1, :] = jnp.max(seg, axis=0, keepdims=True)
        crange_ref[0, 2 * f + 1:2 * f + 2, :] = jnp.min(seg, axis=0, keepdims=True)

    qf = qb.astype(F32)
    kf = kb.astype(F32)
    ss = ss_ref[...]
    kss = _dot((kf * kf).astype(BF16), ss)
    kmax = jnp.maximum(kmax_ref[...], jnp.max(kss, axis=0, keepdims=True))
    kmax_ref[...] = kmax
    bound = jnp.sqrt(_dot((qf * qf).astype(BF16), ss) * kmax)
    bmax_ref[...] = jnp.maximum(bmax_ref[...], jnp.max(bound, axis=0, keepdims=True))

    lane = _lane_iota((rows, LANES))
    hi, mid, lo = _split3(jnp.where(lane < SMALL_BOUND, cum, -bound))
    first_mid = SMALL_F + TERM_LANE_STEP
    terms = jnp.where(lane < first_mid, hi.astype(F32), jnp.where(
        lane < first_mid + TERM_LANE_STEP, pltpu.roll(mid.astype(F32), TERM_LANE_STEP, 1),
        pltpu.roll(lo.astype(F32), 2 * TERM_LANE_STEP, 1)))
    extras = (_dot(terms.astype(BF16), sel_ref[...]) + pat_ref[...]).astype(BF16)
    is_even = lane < HEAD_DIM
    for p in range(N_PAIRS):
        ls = slice(p * LANES, (p + 1) * LANES)
        for hh in range(2):
            h = 2 * p + hh
            keep = is_even if hh == 0 else jnp.logical_not(is_even)
            qa_ref[p, hh, :, 0:LANES] = jnp.where(keep, qb[:, ls], jnp.zeros_like(qb[:, ls]))
            qa_ref[p, hh, :, LANES:2 * LANES] = extras[:, h * LANES:(h + 1) * LANES]
        ka_ref[:, 2 * p * LANES:(2 * p + 1) * LANES] = kb[:, ls]
        ka_ref[:, (2 * p + 1) * LANES:(2 * p + 2) * LANES] = extras[
            :, (N_HEADS + p) * LANES:(N_HEADS + p + 1) * LANES]


def _foxprep(proj, qw_tiled, kw_tiled, fb_pad, bd, tri, *, rows):
    s = proj.shape[0]
    const = lambda shape: pl.BlockSpec(shape, lambda i: (0,) * len(shape))
    ss_sel = _sumsq_selector()
    ex_sel, ex_pat = _extras_selector()
    return pl.pallas_call(
        _foxprep_kernel,
        grid=(s // rows,),
        in_specs=[
            pl.BlockSpec((rows, QK_W), lambda i: (i, COL_FOX_Q // QK_W)),
            pl.BlockSpec((rows, QK_W), lambda i: (i, COL_FOX_K // QK_W)),
            pl.BlockSpec((rows, QK_W), lambda i: (i, COL_FOX_V // QK_W)),
            pl.BlockSpec((rows, LANES), lambda i: (i, COL_SMALL // LANES)),
            const((1, QK_W)),
            const((1, QK_W)),
            const((1, LANES)),
            const((QK_W, QK_W)),
            const((rows, rows)),
            const(ss_sel.shape),
            const(ex_sel.shape),
            const(ex_pat.shape),
        ],
        out_specs=[
            pl.BlockSpec((N_PAIRS, 2, rows, 2 * LANES), lambda i: (0, 0, i, 0)),
            pl.BlockSpec((rows, 2 * QK_W), lambda i: (i, 0)),
            pl.BlockSpec((rows, QK_W), lambda i: (i, 0)),
            pl.BlockSpec((1, N_HEADS, rows), lambda i: (i, 0, 0)),
            pl.BlockSpec((1, LANES), lambda i: (0, 0)),
            pl.BlockSpec((1, 2 * (rows // FOX_TF), LANES), lambda i: (i, 0, 0)),
        ],
        out_shape=[
            jax.ShapeDtypeStruct((N_PAIRS, 2, s, 2 * LANES), BF16),
            jax.ShapeDtypeStruct((s, 2 * QK_W), BF16),
            jax.ShapeDtypeStruct((s, QK_W), BF16),
            jax.ShapeDtypeStruct((s // rows, N_HEADS, rows), F32),
            jax.ShapeDtypeStruct((1, LANES), F32),
            jax.ShapeDtypeStruct((s // rows, 2 * (rows // FOX_TF), LANES), F32),
        ],
        scratch_shapes=[pltpu.VMEM((1, LANES), F32), pltpu.VMEM((1, LANES), F32)],
        compiler_params=pltpu.CompilerParams(
            dimension_semantics=("arbitrary",), vmem_limit_bytes=VMEM_LIMIT),
        name="foxprep",
    )(proj, proj, proj, proj, qw_tiled, kw_tiled, fb_pad, bd, tri, ss_sel, ex_sel, ex_pat)


FAST_BOUND_MAX = 40.0
FAST_TAIL_BACK = 3


def _fox_kernel(fast_ref, first_ref, q_ref, k_ref, v_ref, c_ref, z_ref, nw_ref, o_ref, m_ref,
                acc_ref, *, tq, tk, tf):
    for t in range(q_ref.shape[2] // tq):
        _fox_query_tile(t, fast_ref, first_ref, q_ref, k_ref, v_ref, c_ref, z_ref, nw_ref, o_ref,
                        m_ref, acc_ref, tq=tq, tk=tk, tf=tf)


def _fox_query_tile(t, fast_ref, first_ref, q_ref, k_ref, v_ref, c_ref, z_ref, nw_ref, o_ref, m_ref,
                    acc_ref, *, tq, tk, tf):
    tiles = q_ref.shape[2] // tq
    rows = slice(t * tq, (t + 1) * tq)
    p = pl.program_id(0)
    i = pl.program_id(1) * tiles + t
    first = first_ref[p * (pl.num_programs(1) * tiles) + i]
    n_sub = tq // tk
    n_diag = tq // tf

    acc_ref[...] = jnp.zeros(acc_ref.shape, F32)

    def kv_rows(start, size):
        start = pl.multiple_of(start, tf)
        vj = jnp.concatenate([v_ref[pl.ds(start, size), :], jnp.ones((size, LANES), BF16)], axis=1)
        return k_ref[pl.ds(start, size), :], vj

    def causal_keep(j):
        q_pos = i * tq + lax.broadcasted_iota(jnp.int32, (tq, tk), 0)
        k_pos = j * tk + lax.broadcasted_iota(jnp.int32, (tq, tk), 1)
        return k_pos <= q_pos

    def fast_tile(j):
        kj, vj = kv_rows(j * tf, tf)
        for hh in range(2):
            s = _dot_nt(q_ref[0, hh, rows, :], kj)
            acc_ref[hh] += _dot(jnp.exp(s).astype(BF16), vj)

    def slow_tile(j, masked):
        kj, vj = kv_rows(j * tk, tk)
        for hh in range(2):
            ck = c_ref[j, pl.ds(2 * p + hh, 1), :]
            s = _dot_nt(q_ref[0, hh, rows, 0:LANES], kj[:, 0:LANES]) - ck
            if masked:
                s = jnp.where(causal_keep(j), s, -jnp.inf)
            m_prev = m_ref[hh]
            m_new = jnp.maximum(m_prev, jnp.max(s, axis=-1, keepdims=True))
            alpha = jnp.exp(m_prev - m_new)
            pr = jnp.exp(s - m_new)
            acc_ref[hh] = alpha * acc_ref[hh] + _dot(pr.astype(BF16), vj)
            m_ref[hh] = m_new

    def run(tile):
        def body(j, carry):
            tile(j, False)
            return carry
        lax.fori_loop(first // (tk // tf), i * n_sub, body, 0)
        for d in range(n_sub):
            tile(i * n_sub + d, True)

    def fast_tail(n_back):
        span = n_back * tf + tq
        ks, vs = kv_rows(i * tq - n_back * tf, span)
        q_rel = n_back * tf + lax.broadcasted_iota(jnp.int32, (tq, span), 0)
        keep = lax.broadcasted_iota(jnp.int32, (tq, span), 1) <= q_rel
        for hh in range(2):
            s = jnp.where(keep, _dot_nt(q_ref[0, hh, rows, :], ks), -jnp.inf)
            acc_ref[hh] += _dot(jnp.exp(s).astype(BF16), vs)

    def run_fast():
        def body(j, carry):
            fast_tile(j)
            return carry
        lax.fori_loop(first, jnp.maximum(first, i * n_diag - FAST_TAIL_BACK), body, 0)
        n_short = -(-FAST_TAIL_BACK // n_diag)
        for i_short in range(n_short):
            pl.when(i == i_short)(functools.partial(fast_tail, i_short * n_diag))
        pl.when(i >= n_short)(functools.partial(fast_tail, FAST_TAIL_BACK))

    @pl.when(fast_ref[0] == 1)
    def _():
        run_fast()

    @pl.when(fast_ref[0] != 1)
    def _():
        m_ref[...] = jnp.full(m_ref.shape, -jnp.inf, F32)
        run(slow_tile)

    acc_e = acc_ref[0]
    acc_o = acc_ref[1]
    is_even = _lane_iota((tq, LANES)) < HEAD_DIM
    o = jnp.where(is_even, acc_e[:, 0:LANES] / acc_e[:, LANES:2 * LANES],
                  acc_o[:, 0:LANES] / acc_o[:, LANES:2 * LANES])
    o2 = o * o
    ss_e = jnp.sum(jnp.where(is_even, o2, 0.0), axis=-1, keepdims=True)
    ss_o = jnp.sum(jnp.where(is_even, 0.0, o2), axis=-1, keepdims=True)
    ms = jnp.where(is_even, ss_e, ss_o) * (1.0 / HEAD_DIM)
    z = z_ref[rows, :]
    o_ref[rows, :] = (o * lax.rsqrt(ms + EPS) * nw_ref[...] * (z * _sigmoid(z))).astype(o_ref.dtype)


SKIP_LOG = 110.0


def _first_key_tile(crange, bmax, *, tq, tk):
    n_k = crange.shape[0]
    n_sub = tq // tk
    c_max = crange[:, 0, SMALL_F:SMALL_F + N_HEADS].T
    c_min = crange[:, 1, SMALL_F:SMALL_F + N_HEADS].T
    cq_max = c_max.reshape(N_HEADS, n_k // n_sub, n_sub).max(axis=-1)
    bound = bmax[0, SMALL_BOUND:SMALL_BOUND + N_HEADS]
    gap = cq_max[:, :, None] - c_min[:, None, :] + 2.0 * bound[:, None, None]
    dead = gap <= -SKIP_LOG
    tile_idx = lax.broadcasted_iota(jnp.int32, dead.shape, 2)
    first = jnp.min(jnp.where(dead, n_k, tile_idx), axis=-1)
    first = jnp.min(first.reshape(N_PAIRS, 2, -1), axis=1)
    first = jnp.minimum(first, jnp.arange(first.shape[1], dtype=jnp.int32) * n_sub)
    return first.reshape(-1).astype(jnp.int32)


def _fox(fast, first, qa, ka, vb, ct, proj, nw_pair, *, tq, tk):
    s = vb.shape[0]
    step_rows = FOX_Q_TILES_PER_STEP * tq
    grid_spec = pltpu.PrefetchScalarGridSpec(
        num_scalar_prefetch=2,
        grid=(N_PAIRS, s // step_rows),
        in_specs=[
            pl.BlockSpec((1, 2, step_rows, 2 * LANES), lambda p, i, *_: (p, 0, i, 0)),
            pl.BlockSpec((s, 2 * LANES), lambda p, i, *_: (0, p)),
            pl.BlockSpec((s, LANES), lambda p, i, *_: (0, p)),
            pl.BlockSpec(ct.shape, lambda p, i, *_: (0, 0, 0)),
            pl.BlockSpec((step_rows, LANES), lambda p, i, *_: (i, COL_FOX_Z // LANES + p)),
            pl.BlockSpec((1, LANES), lambda p, i, *_: (0, 0)),
        ],
        out_specs=pl.BlockSpec((step_rows, LANES), lambda p, i, *_: (i, p)),
        scratch_shapes=[
            pltpu.VMEM((2, tq, 1), F32),
            pltpu.VMEM((2, tq, 2 * LANES), F32),
        ],
    )
    return pl.pallas_call(
        functools.partial(_fox_kernel, tq=tq, tk=tk, tf=FOX_TF),
        grid_spec=grid_spec,
        out_shape=jax.ShapeDtypeStruct((s, QK_W), BF16),
        compiler_params=pltpu.CompilerParams(
            dimension_semantics=("arbitrary", "arbitrary"), vmem_limit_bytes=VMEM_LIMIT),
        name="fox",
    )(fast, first, qa, ka, vb, ct, proj, nw_pair)


def _merge_kernel(x_ref, ya_ref, yb_ref, ga_ref, gb_ref, pa_ref, pb_ref, wo_ref, fw_ref, o_ref):
    ua = _dot(ya_ref[...].astype(BF16), pa_ref[...])
    ub = _dot(yb_ref[...].astype(BF16), pb_ref[...])
    merged = ga_ref[...].astype(F32) * ua + gb_ref[...].astype(F32) * ub
    xn = x_ref[...] + _dot(merged.astype(BF16), wo_ref[...])
    ms = jnp.mean(xn * xn, axis=-1, keepdims=True)
    o_ref[...] = xn * lax.rsqrt(ms + EPS) * fw_ref[...]


def _merge(x, ya, yb, gates, w_up_a, w_up_b, w_out, out_w, *, tm=1024):
    s, d = x.shape
    const = lambda shape: pl.BlockSpec(shape, lambda i: (0,) * len(shape))
    return pl.pallas_call(
        _merge_kernel,
        grid=(s // tm,),
        in_specs=[
            pl.BlockSpec((tm, d), lambda i: (i, 0)),
            pl.BlockSpec((tm, QK_W), lambda i: (i, 0)),
            pl.BlockSpec((tm, QK_W), lambda i: (i, 0)),
            pl.BlockSpec((tm, d), lambda i: (i, 0)),
            pl.BlockSpec((tm, d), lambda i: (i, 1)),
            const((QK_W, d)),
            const((QK_W, d)),
            const((d, d)),
            const((1, d)),
        ],
        out_specs=pl.BlockSpec((tm, d), lambda i: (i, 0)),
        out_shape=jax.ShapeDtypeStruct((s, d), F32),
        compiler_params=pltpu.CompilerParams(
            dimension_semantics=("arbitrary",), vmem_limit_bytes=VMEM_LIMIT),
        name="merge",
    )(x, ya, yb, gates, gates, w_up_a, w_up_b, w_out, out_w)


def _block_diag_ones(n, blk):
    idx = np.arange(n) // blk
    return jnp.asarray(idx[:, None] == idx[None, :], dtype=BF16)


def _lower_tri_ones(n, blk=None):
    idx = np.arange(n)
    tri = idx[None, :] <= idx[:, None]
    if blk is not None:
        tri = tri & (idx[None, :] // blk == idx[:, None] // blk)
    return jnp.asarray(tri, dtype=BF16)


def _expand_matrix():
    e = np.zeros((LANES, QK_W + N_HEADS * LANES), np.float32)
    for h in range(N_HEADS):
        e[SMALL_B + h, h * HEAD_DIM:(h + 1) * HEAD_DIM] = 1.0
        e[SMALL_A + h, QK_W + h * LANES:QK_W + (h + 1) * LANES] = 1.0
    return jnp.asarray(e, dtype=BF16)


def _pad_lanes(vec, offset):
    return jnp.zeros((1, LANES), F32).at[0, offset:offset + vec.shape[0]].set(vec.astype(F32))


def _regroup_w_in(w):
    d = w.shape[0]
    o_z = 3 * QK_W
    o_b = o_z + QK_W
    o_a = o_b + N_HEADS
    o_fq = o_a + N_HEADS
    o_ff = o_fq + 4 * QK_W
    o_gate = o_ff + N_HEADS
    small = jnp.concatenate(
        [w[:, o_b:o_a], w[:, o_a:o_fq], w[:, o_ff:o_gate],
         jnp.zeros((d, LANES - 3 * N_HEADS), w.dtype)], axis=1)
    return tuple(g.astype(BF16) for g in (w[:, 0:o_b], w[:, o_fq:o_ff], small, w[:, o_gate:]))


def kernel(x, norm_w, w_in, gate_b, conv_w, a_log, dt_bias, gdn_norm_w, f_bias,
           fox_qn_w, fox_kn_w, fox_on_w, w_up_gdn, w_up_fox, w_out, final_norm_w):
    batch, s, d = x.shape
    assert norm_w.shape[0] == 1, "the merge call fuses the final RMSNorm: single-layer trunk only"
    assert s % (FOX_TQ * FOX_Q_TILES_PER_STEP) == 0 and FOX_TQ % FOX_TK == 0 and FOX_TK % FOX_TF == 0
    assert s % GDN_ROWS == 0 and GDN_ROWS % GDN_CHUNK == 0
    assert w_in.shape[2] == P_WIDTH - LANES + 3 * N_HEADS + gate_b.shape[1]
    l = 0

    bd = _block_diag_ones(QK_W, HEAD_DIM)
    tri_gdn = _lower_tri_ones(GDN_ROWS, GDN_CHUNK)
    tri_fox = _lower_tri_ones(FOX_TK)
    expand = _expand_matrix()
    tile_heads = lambda w: jnp.tile(w.astype(F32), N_HEADS)[None, :]

    outs = []
    x_rows = x.reshape(batch * s, d)
    for b in range(batch):
        xb = x_rows if batch == 1 else x_rows[b * s:(b + 1) * s]
        proj, gates = _proj(xb, norm_w[l][None, :], _regroup_w_in(w_in.reshape(w_in.shape[1:])),
                            gate_b[l][None, :].astype(F32))
        ya = _gdn(proj, conv_w[l], _pad_lanes(a_log[l], SMALL_A), _pad_lanes(dt_bias[l], SMALL_A),
                  tile_heads(gdn_norm_w[l]), bd, tri_gdn, expand)
        qa, ka, vb, ct, bmax, crange = _foxprep(
            proj, tile_heads(fox_qn_w[l]), tile_heads(fox_kn_w[l]),
            _pad_lanes(f_bias[l], SMALL_F), bd, tri_fox, rows=FOX_TK)
        bound_max = jnp.max(bmax[0, SMALL_BOUND:SMALL_BOUND + N_HEADS])
        fast = (bound_max <= FAST_BOUND_MAX).astype(jnp.int32).reshape(1)
        first = _first_key_tile(crange.reshape(s // FOX_TF, 2, LANES), bmax, tq=FOX_TQ, tk=FOX_TF)
        yb = _fox(fast, first, qa, ka, vb, ct, proj,
                  jnp.tile(fox_on_w[l].astype(F32), 2)[None, :], tq=FOX_TQ, tk=FOX_TK)
        outs.append(_merge(xb, ya, yb, gates,
                           w_up_gdn[l].astype(BF16), w_up_fox[l].astype(BF16),
                           w_out[l].astype(BF16), final_norm_w[None, :].astype(F32)))
    out = outs[0] if batch == 1 else jnp.concatenate(outs, axis=0)
    return out.reshape(batch, s, d)
```
